```python
import jax, jax.numpy as jnp
from jax import lax
import numpy as np


D_MODEL = 1024
BATCH = 2
SEQ = 16384
DEPTH = 2

N_MEM = 256
EPS = 1e-6
ROPE_THETA = 10000.0
ATT_HEADS = 8
ATT_KV_HEADS = 2
ATT_HEAD_DIM = D_MODEL // ATT_HEADS
IDX_HEADS = 4
IDX_DIM = 64
TOPK_MAX = 256
Q_BLOCK = 128
GDN_HEADS = 8
GDN_DK = 128
GDN_DV = 128
CONV_K = 4
CHUNK = 64
XATT_HEADS = 4
XATT_HEAD_DIM = D_MODEL // XATT_HEADS
FF_DIM = -(-8 * D_MODEL // (3 * 256)) * 256

A_Q = ATT_HEADS * ATT_HEAD_DIM
A_KV = ATT_KV_HEADS * ATT_HEAD_DIM
I_Q = IDX_HEADS * IDX_DIM
I_K = IDX_DIM
I_W = IDX_HEADS
B_QK = GDN_HEADS * GDN_DK
B_V = GDN_HEADS * GDN_DV
B_QKV = 2 * B_QK + B_V
B_A = GDN_HEADS
B_B = GDN_HEADS
G_A = A_Q
G_B = B_V
IN_SPLITS = (A_Q, A_KV, A_KV, I_Q, I_K, I_W, B_QKV, B_A, B_B, G_A, G_B)
IN_COLS = A_Q + 2 * A_KV + I_Q + I_K + I_W + B_QKV + B_A + B_B + G_A + G_B

kernel_name = "hybrid_dsa_gdn_parallel_block"


def _split_points(sizes):
    return [int(v) for v in np.cumsum(sizes)[:-1]]


def _rms_norm(x, g):
    xf = x.astype(jnp.float32)
    y = xf * lax.rsqrt(jnp.mean(xf * xf, axis=-1, keepdims=True) + EPS)
    return (y * g.astype(jnp.float32)).astype(x.dtype)


def _l2norm(x):
    return x * lax.rsqrt(jnp.sum(x * x, axis=-1, keepdims=True) + EPS)


def _rope(x, positions):
    d = x.shape[-1]
    half = d // 2
    inv_freq = ROPE_THETA ** (-jnp.arange(half, dtype=jnp.float32) * 2.0 / d)
    ang = positions.astype(jnp.float32)[:, :, None] * inv_freq
    cos = jnp.cos(ang)[:, :, None, :]
    sin = jnp.sin(ang)[:, :, None, :]
    xf = x.astype(jnp.float32)
    x1, x2 = xf[..., :half], xf[..., half:]
    return jnp.concatenate([x1 * cos - x2 * sin, x2 * cos + x1 * sin], axis=-1).astype(x.dtype)


def _dsa_attention(q, k, v, q_idx, k_idx, w_idx, topk):
    B, S, H, Dh = q.shape
    G = ATT_KV_HEADS
    R = H // G
    n_blk = S // Q_BLOCK

    def to_blocks(a):
        return jnp.swapaxes(a.reshape((B, n_blk, Q_BLOCK) + a.shape[2:]), 0, 1)

    starts = jnp.arange(n_blk, dtype=jnp.int32) * Q_BLOCK
    key_pos = jnp.arange(S, dtype=jnp.int32)
    k_idx_f = k_idx.astype(jnp.float32)
    scale = Dh ** -0.5
    idx_scale = IDX_DIM ** -0.5

    def one_block(args):
        qb, qib, wb, t0 = args
        t_pos = t0 + jnp.arange(Q_BLOCK, dtype=jnp.int32)
        logits = jnp.einsum('bthd,bsd->bths', qib.astype(jnp.float32), k_idx_f) * idx_scale
        score = jnp.einsum('bth,bths->bts', wb.astype(jnp.float32), jax.nn.relu(logits))
        causal = key_pos[None, :] <= t_pos[:, None]
        score = jnp.where(causal[None], score, -jnp.inf)
        _, sel = lax.top_k(score, topk)
        valid = sel <= t_pos[None, :, None]
        kg = jax.vmap(lambda a, i: a[i])(k, sel)
        vg = jax.vmap(lambda a, i: a[i])(v, sel)
        qg = qb.reshape(B, Q_BLOCK, G, R, Dh).astype(jnp.float32)
        s = jnp.einsum('btgrd,btngd->btgrn', qg, kg.astype(jnp.float32)) * scale
        s = jnp.where(valid[:, :, None, None, :], s, -jnp.inf)
        p = jax.nn.softmax(s, axis=-1)
        o = jnp.einsum('btgrn,btngd->btgrd', p, vg.astype(jnp.float32))
        return o.reshape(B, Q_BLOCK, H * Dh).astype(q.dtype)

    out = lax.map(one_block, (to_blocks(q), to_blocks(q_idx), to_blocks(w_idx), starts))
    return jnp.swapaxes(out, 0, 1).reshape(B, S, H * Dh)


def _causal_conv_silu(x, w):
    K = w.shape[0]
    S = x.shape[1]
    xp = jnp.pad(x, ((0, 0), (K - 1, 0), (0, 0)))
    y = xp[:, 0:S] * w[0]
    for j in range(1, K):
        y = y + xp[:, j:j + S] * w[j]
    return jax.nn.silu(y)


def _gated_delta_rule(q, k, v, g, beta):
    B, S, H, Dk = q.shape
    Dv = v.shape[-1]
    C = CHUNK
    N = S // C

    def chunks(a):
        a = a.reshape((B, N, C, H) + a.shape[3:])
        return jnp.moveaxis(a, 3, 1)

    q = chunks(q) * (Dk ** -0.5)
    k = chunks(k)
    v = chunks(v)
    g = chunks(g)
    beta = chunks(beta)
    g_cum = jnp.cumsum(g, axis=-1)
    tril = jnp.tril(jnp.ones((C, C), dtype=bool))
    strict = jnp.tril(jnp.ones((C, C), dtype=bool), -1)
    diff = g_cum[..., :, None] - g_cum[..., None, :]
    decay = jnp.where(tril, jnp.exp(jnp.where(tril, diff, 0.0)), 0.0)
    k_beta = k * beta[..., None]
    v_beta = v * beta[..., None]
    m = jnp.where(strict, jnp.einsum('bhnid,bhnjd->bhnij', k_beta, k) * decay, 0.0)
    a_mat = m + jnp.eye(C, dtype=jnp.float32)
    rhs = jnp.concatenate([v_beta, k_beta * jnp.exp(g_cum)[..., None]], axis=-1)
    sol = lax.linalg.triangular_solve(a_mat, rhs, left_side=True, lower=True, unit_diagonal=True)
    u = sol[..., :Dv]
    w = sol[..., Dv:]
    attn = jnp.einsum('bhnid,bhnjd->bhnij', q, k) * decay
    q_dec = q * jnp.exp(g_cum)[..., None]
    k_dec = k * jnp.exp(g_cum[..., -1:] - g_cum)[..., None]
    g_last = jnp.exp(g_cum[..., -1])

    def step(state, xs):
        u_c, w_c, attn_c, qd_c, kd_c, gl_c = xs
        v_new = u_c - jnp.einsum('bhcd,bhde->bhce', w_c, state)
        o = jnp.einsum('bhcd,bhde->bhce', qd_c, state) + jnp.einsum('bhij,bhje->bhie', attn_c, v_new)
        state = state * gl_c[..., None, None] + jnp.einsum('bhcd,bhce->bhde', kd_c, v_new)
        return state, o

    xs = (jnp.moveaxis(u, 2, 0), jnp.moveaxis(w, 2, 0), jnp.moveaxis(attn, 2, 0),
          jnp.moveaxis(q_dec, 2, 0), jnp.moveaxis(k_dec, 2, 0), jnp.moveaxis(g_last, 2, 0))
    state0 = jnp.zeros((B, H, Dk, Dv), dtype=jnp.float32)
    _, o = lax.scan(step, state0, xs)
    return jnp.transpose(o, (1, 0, 3, 2, 4)).reshape(B, S, H, Dv)


def _hybrid_mixer(h, positions, w_in, conv_w, a_log, dt_bias, gdn_norm, w_out, topk):
    B, S, _ = h.shape
    proj = h @ w_in
    (a_q, a_k, a_v, i_q, i_k, i_w, b_qkv, b_a, b_b, g_a, g_b) = jnp.split(
        proj, _split_points(IN_SPLITS), axis=-1)

    q = _rope(a_q.reshape(B, S, ATT_HEADS, ATT_HEAD_DIM), positions)
    k = _rope(a_k.reshape(B, S, ATT_KV_HEADS, ATT_HEAD_DIM), positions)
    v = a_v.reshape(B, S, ATT_KV_HEADS, ATT_HEAD_DIM)
    qi = _rope(i_q.reshape(B, S, IDX_HEADS, IDX_DIM), positions)
    ki = _rope(i_k.reshape(B, S, 1, IDX_DIM), positions)[:, :, 0]
    wi = i_w * (IDX_HEADS ** -0.5)
    y_a = _dsa_attention(q, k, v, qi, ki, wi, topk)

    qkv = _causal_conv_silu(b_qkv, conv_w)
    bq, bk, bv = jnp.split(qkv, [B_QK, 2 * B_QK], axis=-1)
    bq = _l2norm(bq.astype(jnp.float32).reshape(B, S, GDN_HEADS, GDN_DK))
    bk = _l2norm(bk.astype(jnp.float32).reshape(B, S, GDN_HEADS, GDN_DK))
    bv = bv.astype(jnp.float32).reshape(B, S, GDN_HEADS, GDN_DV)
    g = -jnp.exp(a_log.astype(jnp.float32)) * jax.nn.softplus(
        b_a.astype(jnp.float32) + dt_bias.astype(jnp.float32))
    beta = jax.nn.sigmoid(b_b.astype(jnp.float32))
    o_b = _gated_delta_rule(bq, bk, bv, g, beta)
    y_b = _rms_norm(o_b, gdn_norm).reshape(B, S, B_V).astype(h.dtype)

    y = jax.nn.sigmoid(g_a) * y_a + jax.nn.sigmoid(g_b) * y_b
    return y @ w_out


def _memory_cross_attention(h, mem_n, w_xq, w_xkv, w_xo):
    B, S, _ = h.shape
    M = mem_n.shape[1]
    q = (h @ w_xq).reshape(B, S, XATT_HEADS, XATT_HEAD_DIM)
    kv = mem_n @ w_xkv
    k, v = jnp.split(kv, 2, axis=-1)
    k = k.reshape(B, M, XATT_HEADS, XATT_HEAD_DIM)
    v = v.reshape(B, M, XATT_HEADS, XATT_HEAD_DIM)
    s = jnp.einsum('bshd,bmhd->bhsm', q.astype(jnp.float32), k.astype(jnp.float32)) * (XATT_HEAD_DIM ** -0.5)
    p = jax.nn.softmax(s, axis=-1)
    o = jnp.einsum('bhsm,bmhd->bshd', p, v.astype(jnp.float32)).astype(h.dtype)
    return o.reshape(B, S, D_MODEL) @ w_xo


def _swiglu(h, w_gu, w_down):
    gate, up = jnp.split(h @ w_gu, 2, axis=-1)
    return (jax.nn.silu(gate) * up) @ w_down


def setup_inputs(seed: int = 0) -> dict:
    key = jax.random.key(seed)
    ks = jax.random.split(key, 24)
    f32 = jnp.float32

    def dense(k, shape, fan_in):
        return jax.random.normal(k, shape, f32) * (fan_in ** -0.5)

    def gain(k):
        return 1.0 + 0.02 * jax.random.normal(k, (DEPTH, D_MODEL), f32)

    x = jax.random.normal(ks[0], (BATCH, SEQ, D_MODEL), f32)
    mem = jax.random.normal(ks[1], (BATCH, N_MEM, D_MODEL), f32)
    offsets = jax.random.randint(ks[2], (BATCH, 1), 0, 1024, dtype=jnp.int32)
    positions = offsets + jnp.arange(SEQ, dtype=jnp.int32)[None, :]
    dt = jnp.exp(jax.random.uniform(ks[6], (DEPTH, GDN_HEADS), f32,
                                    jnp.log(0.001), jnp.log(0.1)))
    return {
        "x": x,
        "mem": mem,
        "positions": positions,
        "norm_mix_pre": gain(ks[3]),
        "w_in": dense(ks[4], (DEPTH, D_MODEL, IN_COLS), D_MODEL),
        "conv_w": 0.5 * jax.random.normal(ks[5], (DEPTH, CONV_K, B_QKV), f32),
        "a_log": jnp.log(jax.random.uniform(ks[7], (DEPTH, GDN_HEADS), f32, 1.0, 16.0)),
        "dt_bias": dt + jnp.log(-jnp.expm1(-dt)),
        "gdn_norm": 1.0 + 0.02 * jax.random.normal(ks[8], (DEPTH, GDN_DV), f32),
        "w_out": dense(ks[9], (DEPTH, A_Q, D_MODEL), A_Q),
        "norm_mix_post": gain(ks[10]),
        "norm_x_pre": gain(ks[11]),
        "norm_mem": gain(ks[12]),
        "w_xq": dense(ks[13], (DEPTH, D_MODEL, D_MODEL), D_MODEL),
        "w_xkv": dense(ks[14], (DEPTH, D_MODEL, 2 * D_MODEL), D_MODEL),
        "w_xo": dense(ks[15], (DEPTH, D_MODEL, D_MODEL), D_MODEL),
        "norm_x_post": gain(ks[16]),
        "norm_ffn_pre": gain(ks[17]),
        "w_gu": dense(ks[18], (DEPTH, D_MODEL, 2 * FF_DIM), D_MODEL),
        "w_down": dense(ks[19], (DEPTH, FF_DIM, D_MODEL), FF_DIM),
        "norm_ffn_post": gain(ks[20]),
    }


def reference(x, mem, positions, norm_mix_pre, w_in, conv_w, a_log, dt_bias, gdn_norm, w_out,
              norm_mix_post, norm_x_pre, norm_mem, w_xq, w_xkv, w_xo, norm_x_post,
              norm_ffn_pre, w_gu, w_down, norm_ffn_post):
    S = x.shape[1]
    topk = min(TOPK_MAX, S // 4)
    for l in range(DEPTH):
        h = _rms_norm(x, norm_mix_pre[l])
        y = _hybrid_mixer(h, positions, w_in[l], conv_w[l], a_log[l], dt_bias[l], gdn_norm[l],
                          w_out[l], topk)
        x = x + _rms_norm(y, norm_mix_post[l])
        h = _rms_norm(x, norm_x_pre[l])
        mem_n = _rms_norm(mem, norm_mem[l])
        y = _memory_cross_attention(h, mem_n, w_xq[l], w_xkv[l], w_xo[l])
        x = x + _rms_norm(y, norm_x_post[l])
        h = _rms_norm(x, norm_ffn_pre[l])
        y = _swiglu(h, w_gu[l], w_down[l])
        x = x + _rms_norm(y, norm_ffn_post[l])
    return x
```

```python
import functools

import jax
import jax.numpy as jnp
import numpy as np
from jax import lax
from jax.experimental import pallas as pl
from jax.experimental.pallas import tpu as pltpu

EPS = 1e-6
ROPE_THETA = 10000.0
ATT_HEADS = 8
ATT_KV_HEADS = 2
ATT_HEAD_DIM = 128
IDX_HEADS = 4
IDX_DIM = 64
TOPK_MAX = 256
GDN_HEADS = 8
GDN_DK = 128
GDN_DV = 128
CONV_K = 4
CHUNK = 64
XATT_HEADS = 4

LANES = 128
VMEM_LIMIT = 56 * 1024 * 1024

F32 = jnp.float32
BF16 = jnp.bfloat16
I32 = jnp.int32

NEG_BIG = -1e30
INT_MIN = -2147483648
NEG_INF_KEY = -2139095041


def _cparams(sem):
    return pltpu.CompilerParams(dimension_semantics=sem, vmem_limit_bytes=VMEM_LIMIT)


def _const_spec(shape):
    nd = len(shape)
    return pl.BlockSpec(shape, lambda *_: (0,) * nd, pipeline_mode=pl.Buffered(1))


def _rms(x, g):
    return x * lax.rsqrt(jnp.mean(x * x, axis=-1, keepdims=True) + EPS) * g


def _dot(a, b):
    return jnp.dot(a, b, preferred_element_type=F32)


def _dot_nt(a, b):
    return lax.dot_general(a, b, (((1,), (1,)), ((), ())), preferred_element_type=F32)


def _split2(a):
    hi = a.astype(BF16)
    lo = (a - hi.astype(F32)).astype(BF16)
    return hi, lo


def _split3(a):
    a1 = a.astype(BF16)
    r1 = a - a1.astype(F32)
    a2 = r1.astype(BF16)
    a3 = (r1 - a2.astype(F32)).astype(BF16)
    return a1, a2, a3


def _mm3(a, b):
    ah, al = _split2(a)
    bh, bl = _split2(b)
    return _dot(ah, bh) + (_dot(ah, bl) + _dot(al, bh))


def _rope128(x, cos, sin_signed):
    return x * cos + pltpu.roll(x, 64, 1) * sin_signed


def _rope64(x, cos, sin_signed, first_half):
    partner = jnp.where(first_half, pltpu.roll(x, 96, 1), pltpu.roll(x, 32, 1))
    return x * cos + partner * sin_signed


def _proj_att_kernel(x_ref, g_ref, w_ref, ca_ref, sa_ref, ci_ref, si_ref,
                     q_ref, k_ref, v_ref, qi_ref, ki_ref, wi_ref):
    h = _rms(x_ref[...], g_ref[...]).astype(BF16)
    p = _dot(h, w_ref[...])
    ca, sa = ca_ref[...], sa_ref[...]
    ci, si = ci_ref[...], si_ref[...]
    q_scale = ATT_HEAD_DIM ** -0.5
    for hd in range(ATT_HEADS):
        sl = slice(hd * 128, (hd + 1) * 128)
        q_ref[:, sl] = (_rope128(p[:, sl], ca, sa) * q_scale).astype(BF16)
    for g in range(ATT_KV_HEADS):
        sl = slice(g * 128, (g + 1) * 128)
        k_ref[:, sl] = _rope128(p[:, 1024 + g * 128:1024 + (g + 1) * 128], ca, sa).astype(BF16)
    v_ref[...] = p[:, 1280:1536].astype(BF16)
    lane = lax.broadcasted_iota(I32, ci.shape, 1)
    first_half = (lane & 63) < 32
    idx_scale = IDX_DIM ** -0.5
    for hd in range(IDX_HEADS):
        off = 1536 + hd * 128
        qi_ref[:, hd * 128:(hd + 1) * 128] = (
            _rope64(p[:, off:off + 128], ci, si, first_half) * idx_scale).astype(BF16)
    ki_ref[...] = _rope64(p[:, 2048:2176], ci, si, first_half).astype(BF16)
    wi_ref[...] = p[:, 2176:2304] * (IDX_HEADS ** -0.5)


def _proj_att(x2, g, w, ca, sa, ci, si, tm):
    n, d = x2.shape
    row = lambda c: pl.BlockSpec((tm, c), lambda i: (i, 0))
    outs = [(1024, BF16), (256, BF16), (256, BF16), (512, BF16), (128, BF16), (128, F32)]
    return pl.pallas_call(
        _proj_att_kernel,
        grid=(n // tm,),
        in_specs=[row(d), _const_spec((1, d)), _const_spec(w.shape),
                  row(128), row(128), row(128), row(128)],
        out_specs=[row(c) for c, _ in outs],
        out_shape=[jax.ShapeDtypeStruct((n, c), dt) for c, dt in outs],
        compiler_params=_cparams(("parallel",)),
        name="proj_att",
    )(x2, g, w, ca, sa, ci, si)


def _proj_gdn_kernel(x_ref, g_ref, w_ref, qkv_ref, ab_ref):
    h = _rms(x_ref[...], g_ref[...]).astype(BF16)
    p = _dot(h, w_ref[...])
    qkv_ref[...] = p[:, :3072]
    ab_ref[...] = p[:, 3072:3200]


def _proj_gdn(x2, g, w, tm):
    n, d = x2.shape
    row = lambda c: pl.BlockSpec((tm, c), lambda i: (i, 0))
    return pl.pallas_call(
        _proj_gdn_kernel,
        grid=(n // tm,),
        in_specs=[row(d), _const_spec((1, d)), _const_spec(w.shape)],
        out_specs=[row(3072), row(128)],
        out_shape=[jax.ShapeDtypeStruct((n, 3072), F32), jax.ShapeDtypeStruct((n, 128), F32)],
        compiler_params=_cparams(("parallel",)),
        name="proj_gdn",
    )(x2, g, w)


def _dsa_kernel(q_ref, qi_ref, wi_ref, kt_ref, v_ref, kit_ref, o_ref,
                keys_ref, acc_ref, m_ref, l_ref, j_ref, *, tq, kb, topk, seq):
    t0 = pl.program_id(1) * tq
    nkb = (t0 + tq + kb - 1) // kb
    n_lane_tiles = kb // LANES
    row_ids = t0 + lax.broadcasted_iota(I32, (tq, kb), 0)
    col_iota = lax.broadcasted_iota(I32, (tq, kb), 1)

    qi = qi_ref[...]
    wi = wi_ref[...]
    w_heads = [wi[:, h:h + 1] for h in range(IDX_HEADS)]

    def score_block(j, carry):
        c0 = pl.multiple_of(j * kb, kb)
        kit = kit_ref[:, pl.ds(c0, kb)]
        sc = jnp.zeros((tq, kb), F32)
        for h in range(IDX_HEADS):
            lg = _dot(qi[:, h * 128:(h + 1) * 128], kit)
            sc = sc + w_heads[h] * jnp.maximum(lg, 0.0)
        sc = jnp.where(c0 + col_iota <= row_ids, sc, -jnp.inf)
        bits = pltpu.bitcast(sc, I32)
        keys_ref[:, pl.ds(c0, kb)] = jnp.where(bits < 0, bits ^ 0x7FFFFFFF, bits)
        return carry

    lax.fori_loop(0, nkb, score_block, 0)

    def count(pred):
        def body(j, acc):
            c0 = pl.multiple_of(j * kb, kb)
            m = jnp.where(pred(keys_ref[:, pl.ds(c0, kb)], c0), 1.0, 0.0)
            for i in range(n_lane_tiles):
                acc = acc + m[:, i * LANES:(i + 1) * LANES]
            return acc
        acc = lax.fori_loop(0, nkb, body, jnp.zeros((tq, LANES), F32))
        return jnp.sum(acc, axis=1, keepdims=True)

    kf = float(topk)

    def value_bit(i, carry):
        thr, cge = carry
        cand = thr + lax.shift_left(jnp.int32(1), 31 - i)
        c = count(lambda blk, c0: blk >= cand)
        ok = c >= kf
        return jnp.where(ok, cand, thr), jnp.where(ok, c, cge)

    thr0 = jnp.full((tq, 1), INT_MIN, I32)
    cge0 = jnp.zeros((tq, 1), F32) + (nkb * kb).astype(F32)
    thr, cge = lax.fori_loop(0, 32, value_bit, (thr0, cge0))

    tie_rows = jnp.logical_and(cge > kf, thr != NEG_INF_KEY)
    j_ref[...] = jnp.full((tq, 1), seq, I32)

    @pl.when(jnp.max(jnp.where(tie_rows, 1.0, 0.0)) > 0.0)
    def _():
        need = kf - count(lambda blk, c0: blk > thr)
        n_bits = max(1, (seq - 1).bit_length())

        def index_bit(i, pos):
            cand = pos + lax.shift_left(jnp.int32(1), n_bits - 1 - i)
            c = count(lambda blk, c0: jnp.logical_and(blk == thr, c0 + col_iota < cand))
            return jnp.where(c < need, cand, pos)

        pos = lax.fori_loop(0, n_bits, index_bit, jnp.zeros((tq, 1), I32))
        j_ref[...] = jnp.where(tie_rows, pos, seq)

    j_last = j_ref[...]

    rep = ATT_HEADS // ATT_KV_HEADS
    q = q_ref[...]
    q_groups = [jnp.concatenate([q[:, (g * rep + r) * 128:(g * rep + r + 1) * 128] for r in range(rep)],
                                axis=0) for g in range(ATT_KV_HEADS)]
    acc_ref[...] = jnp.zeros(acc_ref.shape, F32)
    l_ref[...] = jnp.zeros(l_ref.shape, F32)
    m_ref[...] = jnp.full(m_ref.shape, NEG_BIG, F32)

    def attend_block(j, carry):
        c0 = pl.multiple_of(j * kb, kb)
        keys = keys_ref[:, pl.ds(c0, kb)]
        cols = c0 + col_iota
        sel = jnp.logical_or(keys > thr, jnp.logical_and(keys == thr, cols <= j_last))
        sel = jnp.logical_and(sel, cols <= row_ids)
        bias = jnp.where(sel, 0.0, NEG_BIG)
        bias = jnp.concatenate([bias] * rep, axis=0)
        for g in range(ATT_KV_HEADS):
            s = _dot(q_groups[g], kt_ref[g * 128:(g + 1) * 128, pl.ds(c0, kb)]) + bias
            m_prev = m_ref[g]
            m_next = jnp.maximum(m_prev, jnp.max(s, axis=1, keepdims=True))
            p = jnp.exp(s - m_next[:, :1])
            alpha = jnp.exp(m_prev - m_next)
            l_ref[g] = alpha * l_ref[g] + jnp.sum(p, axis=1, keepdims=True)
            m_ref[g] = m_next
            pv = _dot(p.astype(BF16), v_ref[pl.ds(c0, kb), g * 128:(g + 1) * 128])
            acc_ref[g] = acc_ref[g] * alpha + pv
        return carry

    lax.fori_loop(0, nkb, attend_block, 0)

    for g in range(ATT_KV_HEADS):
        out = acc_ref[g] / l_ref[g]
        for r in range(rep):
            hd = g * rep + r
            o_ref[:, hd * 128:(hd + 1) * 128] = out[r * tq:(r + 1) * tq]


def _dsa(q, qi, wi, kt, v, kit, topk, tq, kb):
    b, s, _ = q.shape
    rep = ATT_HEADS // ATT_KV_HEADS
    qspec = lambda c: pl.BlockSpec((None, tq, c), lambda bi, i: (bi, i, 0))
    per_batch = lambda r, c: pl.BlockSpec((None, r, c), lambda bi, i: (bi, 0, 0),
                                          pipeline_mode=pl.Buffered(1))
    return pl.pallas_call(
        functools.partial(_dsa_kernel, tq=tq, kb=kb, topk=topk, seq=s),
        grid=(b, s // tq),
        in_specs=[qspec(1024), qspec(512), qspec(128),
                  per_batch(256, s), per_batch(s, 256), per_batch(128, s)],
        out_specs=qspec(1024),
        out_shape=jax.ShapeDtypeStruct((b, s, 1024), F32),
        scratch_shapes=[pltpu.VMEM((tq, s), I32),
                        pltpu.VMEM((ATT_KV_HEADS, rep * tq, 128), F32),
                        pltpu.VMEM((ATT_KV_HEADS, rep * tq, 128), F32),
                        pltpu.VMEM((ATT_KV_HEADS, rep * tq, 128), F32),
                        pltpu.VMEM((tq, 1), I32)],
        compiler_params=_cparams(("parallel", "arbitrary")),
        name="dsa_attention",
    )(q, qi, wi, kt, v, kit)


def _softplus(x):
    return jnp.maximum(x, 0.0) + jnp.log(1.0 + jnp.exp(-jnp.abs(x)))


def _gdn_kernel(qkv_ref, ab_ref, abt_ref, cw_ref, ac_ref, dc_ref, ar_ref, dr_ref, gn_ref,
                y_ref, xbuf_ref, act_ref, state_ref, *, tb):
    n_chunks = tb // CHUNK

    @pl.when(pl.program_id(1) == 0)
    def _():
        xbuf_ref[0:8, :] = jnp.zeros((8, xbuf_ref.shape[1]), F32)
        state_ref[...] = jnp.zeros(state_ref.shape, F32)

    xbuf_ref[8:8 + tb, :] = qkv_ref[...]
    for c in range(xbuf_ref.shape[1] // LANES):
        sl = slice(c * LANES, (c + 1) * LANES)
        y = xbuf_ref[5:5 + tb, sl] * cw_ref[0:1, sl]
        for j in range(1, CONV_K):
            y = y + xbuf_ref[5 + j:5 + j + tb, sl] * cw_ref[j:j + 1, sl]
        act_ref[:, sl] = y * (1.0 / (1.0 + jnp.exp(-y)))
    xbuf_ref[0:8, :] = xbuf_ref[tb:tb + 8, :]

    ab = ab_ref[...]
    g_col = -jnp.exp(ac_ref[...]) * _softplus(ab + dc_ref[...])
    beta_col = 1.0 / (1.0 + jnp.exp(-ab))
    g_row = -jnp.exp(ar_ref[...]) * _softplus(abt_ref[0:8, :] + dr_ref[...])

    r = lax.broadcasted_iota(I32, (tb, tb), 0)
    c = lax.broadcasted_iota(I32, (tb, tb), 1)
    chunk_shift = CHUNK.bit_length() - 1
    same = jnp.right_shift(r, chunk_shift) == jnp.right_shift(c, chunk_shift)
    lower = jnp.logical_and(same, r >= c)
    strict = jnp.logical_and(same, r > c)
    lower_m = jnp.where(lower, 1.0, 0.0).astype(BF16)
    upper_m = jnp.where(jnp.logical_and(same, r <= c), 1.0, 0.0).astype(BF16)
    same_m = jnp.where(same, 1.0, 0.0).astype(BF16)
    eye = jnp.where(r == c, 1.0, 0.0)

    gparts = _split3(g_col)
    gc_col = _dot(lower_m, gparts[0]) + (_dot(lower_m, gparts[1]) + _dot(lower_m, gparts[2]))
    gl_col = _dot(same_m, gparts[0]) + (_dot(same_m, gparts[1]) + _dot(same_m, gparts[2]))
    rparts = _split3(g_row)
    gc_row = _dot(rparts[0], upper_m) + (_dot(rparts[1], upper_m) + _dot(rparts[2], upper_m))

    gn = gn_ref[...]
    for h in range(GDN_HEADS):
        hs = slice(h * 128, (h + 1) * 128)
        q = act_ref[:, hs]
        k = act_ref[:, 1024 + h * 128:1024 + (h + 1) * 128]
        v = act_ref[:, 2048 + h * 128:2048 + (h + 1) * 128]
        q = q * lax.rsqrt(jnp.sum(q * q, axis=-1, keepdims=True) + EPS) * (GDN_DK ** -0.5)
        k = k * lax.rsqrt(jnp.sum(k * k, axis=-1, keepdims=True) + EPS)
        gc = gc_col[:, h:h + 1]
        gl = gl_col[:, h:h + 1]
        beta = beta_col[:, 8 + h:9 + h]
        diff = gc - gc_row[h:h + 1, :]
        decay = jnp.where(lower, jnp.exp(jnp.where(lower, diff, 0.0)), 0.0)
        kbeta = k * beta
        vbeta = v * beta
        kb16 = k.astype(BF16)
        nm = jnp.where(strict, -(_dot_nt(kbeta.astype(BF16), kb16) * decay), 0.0)
        inv = eye + nm
        pw = nm
        for _ in range(5):
            pw = _mm3(pw, pw)
            inv = inv + _mm3(pw, inv)
        egc = jnp.exp(gc)
        rhs = jnp.concatenate([vbeta, kbeta * egc], axis=1)
        sol = _mm3(inv, rhs)
        u = sol[:, :128]
        w = sol[:, 128:]
        attn = jnp.where(lower, _dot_nt(q.astype(BF16), kb16) * decay, 0.0).astype(BF16)
        q_dec = (q * egc).astype(BF16)
        k_dec = k * jnp.exp(gl - gc)
        w16 = w.astype(BF16)
        st = state_ref[h]
        outs = []
        for ci in range(n_chunks):
            rc = slice(ci * CHUNK, (ci + 1) * CHUNK)
            st16 = st.astype(BF16)
            v_new = u[rc] - _dot(w16[rc], st16)
            v_new16 = v_new.astype(BF16)
            outs.append(_dot(q_dec[rc], st16) + _dot(attn[rc, rc], v_new16))
            kd_t = jnp.transpose(k_dec[rc]).astype(BF16)
            st = st * jnp.exp(gl[ci * CHUNK:ci * CHUNK + 1]) + _dot(kd_t, v_new16)
        state_ref[h] = st
        o = jnp.concatenate(outs, axis=0)
        y_ref[:, hs] = _rms(o, gn)


def _gdn(qkv, ab, abt, conv_w, ac, dc, ar, dr, gn, tb):
    b, s, _ = qkv.shape
    blk = lambda c: pl.BlockSpec((None, tb, c), lambda bi, i: (bi, i, 0))
    return pl.pallas_call(
        functools.partial(_gdn_kernel, tb=tb),
        grid=(b, s // tb),
        in_specs=[blk(3072), blk(128), pl.BlockSpec((None, 16, tb), lambda bi, i: (bi, 0, i)),
                  _const_spec(conv_w.shape), _const_spec(ac.shape), _const_spec(dc.shape),
                  _const_spec(ar.shape), _const_spec(dr.shape), _const_spec(gn.shape)],
        out_specs=blk(1024),
        out_shape=jax.ShapeDtypeStruct((b, s, 1024), F32),
        scratch_shapes=[pltpu.VMEM((tb + 8, 3072), F32),
                        pltpu.VMEM((tb, 3072), F32),
                        pltpu.VMEM((GDN_HEADS, GDN_DK, GDN_DV), F32)],
        compiler_params=_cparams(("parallel", "arbitrary")),
        name="gated_delta",
    )(qkv, ab, abt, conv_w, ac, dc, ar, dr, gn)


def _sigmoid(x):
    return 1.0 / (1.0 + jnp.exp(-x))


def _merge_kernel(x_ref, ya_ref, yb_ref, gpre_ref, wg_ref, wo_ref, gpost_ref, o_ref):
    x = x_ref[...]
    h = _rms(x, gpre_ref[...]).astype(BF16)
    gates = _dot(h, wg_ref[...])
    y = _sigmoid(gates[:, :1024]) * ya_ref[...] + _sigmoid(gates[:, 1024:]) * yb_ref[...]
    z = _dot(y.astype(BF16), wo_ref[...])
    o_ref[...] = x + _rms(z, gpost_ref[...])


def _merge(x2, ya, yb, gpre, wg, wo, gpost, tm):
    n, d = x2.shape
    row = pl.BlockSpec((tm, d), lambda i: (i, 0))
    return pl.pallas_call(
        _merge_kernel,
        grid=(n // tm,),
        in_specs=[row, row, row, _const_spec((1, d)), _const_spec(wg.shape),
                  _const_spec(wo.shape), _const_spec((1, d))],
        out_specs=row,
        out_shape=jax.ShapeDtypeStruct((n, d), F32),
        compiler_params=_cparams(("parallel",)),
        name="merge_out",
    )(x2, ya, yb, gpre, wg, wo, gpost)


def _mem_kv_kernel(m_ref, g_ref, w_ref, kv_ref):
    h = _rms(m_ref[...], g_ref[...]).astype(BF16)
    kv_ref[...] = _dot(h, w_ref[...]).astype(BF16)


def _mem_kv(mem2, g, w):
    n, d = mem2.shape
    return pl.pallas_call(
        _mem_kv_kernel,
        out_shape=jax.ShapeDtypeStruct((n, w.shape[1]), BF16),
        compiler_params=pltpu.CompilerParams(vmem_limit_bytes=VMEM_LIMIT),
        name="mem_kv",
    )(mem2, g, w)


def _xattn_kernel(x_ref, kt_ref, v_ref, gpre_ref, wq_ref, wo_ref, gpost_ref, o_ref):
    x = x_ref[...]
    d = x.shape[1]
    hd = d // XATT_HEADS
    h = _rms(x, gpre_ref[...]).astype(BF16)
    q = (_dot(h, wq_ref[...]) * (hd ** -0.5)).astype(BF16)
    heads = []
    for i in range(XATT_HEADS):
        sl = slice(i * hd, (i + 1) * hd)
        s = _dot(q[:, sl], kt_ref[sl, :])
        p = jnp.exp(s - jnp.max(s, axis=1, keepdims=True))
        o = _dot(p.astype(BF16), v_ref[:, sl]) / jnp.sum(p, axis=1, keepdims=True)
        heads.append(o.astype(BF16))
    z = _dot(jnp.concatenate(heads, axis=1), wo_ref[...])
    o_ref[...] = x + _rms(z, gpost_ref[...])


def _xattn(x, kt, v, gpre, wq, wo, gpost, tm):
    b, s, d = x.shape
    n_mem = v.shape[1]
    row = pl.BlockSpec((None, tm, d), lambda bi, i: (bi, i, 0))
    return pl.pallas_call(
        _xattn_kernel,
        grid=(b, s // tm),
        in_specs=[row,
                  pl.BlockSpec((None, d, n_mem), lambda bi, i: (bi, 0, 0)),
                  pl.BlockSpec((None, n_mem, d), lambda bi, i: (bi, 0, 0)),
                  _const_spec((1, d)), _const_spec(wq.shape), _const_spec(wo.shape),
                  _const_spec((1, d))],
        out_specs=row,
        out_shape=jax.ShapeDtypeStruct((b, s, d), F32),
        compiler_params=_cparams(("parallel", "parallel")),
        name="mem_xattn",
    )(x, kt, v, gpre, wq, wo, gpost)


def _ffn_kernel(x_ref, gpre_ref, wgu_ref, wd_ref, gpost_ref, o_ref, *, ff, fc):
    x = x_ref[...]
    h = _rms(x, gpre_ref[...]).astype(BF16)
    z = jnp.zeros(x.shape, F32)
    for c0 in range(0, ff, fc):
        gate = _dot(h, wgu_ref[:, c0:c0 + fc])
        up = _dot(h, wgu_ref[:, ff + c0:ff + c0 + fc])
        act = (gate * _sigmoid(gate) * up).astype(BF16)
        z = z + _dot(act, wd_ref[c0:c0 + fc, :])
    o_ref[...] = x + _rms(z, gpost_ref[...])


def _ffn(x2, gpre, wgu, wd, gpost, tm):
    n, d = x2.shape
    ff = wd.shape[0]
    fc = ff // 2 if (ff // 2) % LANES == 0 else ff
    row = pl.BlockSpec((tm, d), lambda i: (i, 0))
    return pl.pallas_call(
        functools.partial(_ffn_kernel, ff=ff, fc=fc),
        grid=(n // tm,),
        in_specs=[row, _const_spec((1, d)), _const_spec(wgu.shape), _const_spec(wd.shape),
                  _const_spec((1, d))],
        out_specs=row,
        out_shape=jax.ShapeDtypeStruct((n, d), F32),
        compiler_params=_cparams(("parallel",)),
        name="swiglu",
    )(x2, gpre, wgu, wd, gpost)


def _rope_tables(positions):
    pos = positions.astype(F32).reshape(-1)[:, None]

    def tables(dim, reps):
        half = dim // 2
        inv_freq = ROPE_THETA ** (-jnp.arange(half, dtype=F32) * 2.0 / dim)
        ang = pos * inv_freq
        cos, sin = jnp.cos(ang), jnp.sin(ang)
        return (jnp.tile(jnp.concatenate([cos, cos], axis=1), (1, reps)),
                jnp.tile(jnp.concatenate([-sin, sin], axis=1), (1, reps)))

    return tables(ATT_HEAD_DIM, 1) + tables(IDX_DIM, 2)


def _pick_tile(n, pref):
    t = min(pref, n)
    while n % t:
        t //= 2
    return t


def kernel(x, mem, positions, norm_mix_pre, w_in, conv_w, a_log, dt_bias, gdn_norm, w_out,
           norm_mix_post, norm_x_pre, norm_mem, w_xq, w_xkv, w_xo, norm_x_post,
           norm_ffn_pre, w_gu, w_down, norm_ffn_post):
    b, s, d = x.shape
    n = b * s
    n_mem = mem.shape[1]
    depth = w_in.shape[0]
    topk = min(TOPK_MAX, s // 4)
    tm = _pick_tile(s, 512)
    tq = _pick_tile(s, 128)
    kb = _pick_tile(s, 512)
    tb = _pick_tile(s, 256)

    ca, sa, ci, si = _rope_tables(positions)

    a_q, a_kv, i_q, i_k, i_w = 1024, 256, 256, 64, 4
    b_qkv = 3072
    o_k = a_q
    o_v = o_k + a_kv
    o_iq = o_v + a_kv
    o_ik = o_iq + i_q
    o_iw = o_ik + i_k
    o_b = o_iw + i_w
    o_ba = o_b + b_qkv
    o_bb = o_ba + GDN_HEADS
    o_ga = o_bb + GDN_HEADS
    row1 = lambda v: v.reshape(1, -1).astype(F32)

    x2 = x.reshape(n, d)
    for l in range(depth):
        w = w_in[l]
        w_iq = jnp.pad(w[:, o_iq:o_ik].reshape(d, IDX_HEADS, IDX_DIM), ((0, 0), (0, 0), (0, 128 - IDX_DIM)))
        w_att = jnp.concatenate([
            w[:, :o_iq], w_iq.reshape(d, IDX_HEADS * 128),
            jnp.pad(w[:, o_ik:o_iw], ((0, 0), (0, 128 - i_k))),
            jnp.pad(w[:, o_iw:o_b], ((0, 0), (0, 128 - i_w)))], axis=1).astype(BF16)
        w_gdn = jnp.concatenate([
            w[:, o_b:o_ba], jnp.pad(w[:, o_ba:o_ga], ((0, 0), (0, 128 - 2 * GDN_HEADS)))], axis=1).astype(BF16)
        w_gate = w[:, o_ga:].astype(BF16)

        g_pre = row1(norm_mix_pre[l])
        q, k, v, qi, ki, wi = _proj_att(x2, g_pre, w_att, ca, sa, ci, si, tm)
        kt = jnp.swapaxes(k.reshape(b, s, 256), 1, 2)
        kit = jnp.swapaxes(ki.reshape(b, s, 128), 1, 2)
        ya = _dsa(q.reshape(b, s, 1024), qi.reshape(b, s, 512), wi.reshape(b, s, 128),
                  kt, v.reshape(b, s, 256), kit, topk, tq, kb)

        qkv, ab = _proj_gdn(x2, g_pre, w_gdn, tm)
        abt = jnp.swapaxes(ab.reshape(b, s, 128)[:, :, :16], 1, 2)
        pad_lanes = lambda vec: jnp.pad(vec.astype(F32), (0, 128 - GDN_HEADS)).reshape(1, 128)
        bcast_rows = lambda vec: jnp.broadcast_to(vec.astype(F32)[:, None], (GDN_HEADS, tb))
        yb = _gdn(qkv.reshape(b, s, 3072), ab.reshape(b, s, 128), abt, conv_w[l].astype(F32),
                  pad_lanes(a_log[l]), pad_lanes(dt_bias[l]), bcast_rows(a_log[l]), bcast_rows(dt_bias[l]),
                  row1(gdn_norm[l]), tb)

        x2 = _merge(x2, ya.reshape(n, 1024), yb.reshape(n, 1024), g_pre, w_gate,
                    w_out[l].astype(BF16), row1(norm_mix_post[l]), tm)

        kv = _mem_kv(mem.reshape(b * n_mem, d), row1(norm_mem[l]), w_xkv[l].astype(BF16))
        kv = kv.reshape(b, n_mem, 2 * d)
        kt_mem = jnp.swapaxes(kv[:, :, :d], 1, 2)
        x2 = _xattn(x2.reshape(b, s, d), kt_mem, kv[:, :, d:], row1(norm_x_pre[l]),
                    w_xq[l].astype(BF16), w_xo[l].astype(BF16), row1(norm_x_post[l]), tm).reshape(n, d)

        x2 = _ffn(x2, row1(norm_ffn_pre[l]), w_gu[l].astype(BF16), w_down[l].astype(BF16),
                  row1(norm_ffn_post[l]), tm)
    return x2.reshape(b, s, d)
```

```python
import functools

import jax
import jax.numpy as jnp
import numpy as np
from jax import lax
from jax.experimental import pallas as pl
from jax.experimental.pallas import tpu as pltpu

EPS = 1e-6
ROPE_THETA = 10000.0
ATT_HEADS = 8
ATT_KV_HEADS = 2
ATT_HEAD_DIM = 128
IDX_HEADS = 4
IDX_DIM = 64
TOPK_MAX = 256
GDN_HEADS = 8
GDN_DK = 128
GDN_DV = 128
CONV_K = 4
CHUNK = 64
XATT_HEADS = 4

LANES = 128
VMEM_LIMIT = 56 * 1024 * 1024

F32 = jnp.float32
BF16 = jnp.bfloat16
I32 = jnp.int32

NEG_BIG = -1e30
LOG2E = 1.4426950408889634
INT_MIN = -2147483648
NEG_INF_KEY = -2139095041


def _cparams(sem):
    return pltpu.CompilerParams(dimension_semantics=sem, vmem_limit_bytes=VMEM_LIMIT)


def _const_spec(shape):
    nd = len(shape)
    return pl.BlockSpec(shape, lambda *_: (0,) * nd, pipeline_mode=pl.Buffered(1))


def _rms(x, g):
    return x * lax.rsqrt(jnp.mean(x * x, axis=-1, keepdims=True) + EPS) * g


def _dot(a, b):
    return jnp.dot(a, b, preferred_element_type=F32)


def _dot_nt(a, b):
    return lax.dot_general(a, b, (((1,), (1,)), ((), ())), preferred_element_type=F32)


def _split3(a):
    a1 = a.astype(BF16)
    r1 = a - a1.astype(F32)
    a2 = r1.astype(BF16)
    a3 = (r1 - a2.astype(F32)).astype(BF16)
    return a1, a2, a3


def _rope128(x, cos, sin_signed):
    return x * cos + pltpu.roll(x, 64, 1) * sin_signed


def _rope64(x, cos, sin_signed, first_half):
    partner = jnp.where(first_half, pltpu.roll(x, 96, 1), pltpu.roll(x, 32, 1))
    return x * cos + partner * sin_signed


def _proj_att_kernel(x_ref, g_ref, w_ref, ca_ref, sa_ref, ci_ref, si_ref,
                     q_ref, kt_ref, v_ref, qi_ref, kit_ref, wi_ref):
    h = _rms(x_ref[...], g_ref[...]).astype(BF16)
    p = _dot(h, w_ref[...])
    ca, sa = ca_ref[...], sa_ref[...]
    ci, si = ci_ref[...], si_ref[...]
    q_scale = ATT_HEAD_DIM ** -0.5 * LOG2E
    for hd in range(ATT_HEADS):
        sl = slice(hd * 128, (hd + 1) * 128)
        q_ref[:, sl] = (_rope128(p[:, sl], ca, sa) * q_scale).astype(BF16)
    for g in range(ATT_KV_HEADS):
        k_g = _rope128(p[:, 1024 + g * 128:1024 + (g + 1) * 128], ca, sa)
        kt_ref[g * 128:(g + 1) * 128, :] = jnp.transpose(k_g).astype(BF16)
    v_ref[...] = p[:, 1280:1536].astype(BF16)
    lane = lax.broadcasted_iota(I32, ci.shape, 1)
    first_half = (lane & 63) < 32
    idx_scale = IDX_DIM ** -0.5
    for hd in range(IDX_HEADS):
        off = 1536 + hd * 128
        qi_ref[:, hd * 128:(hd + 1) * 128] = (
            _rope64(p[:, off:off + 128], ci, si, first_half) * idx_scale).astype(BF16)
    kit_ref[...] = jnp.transpose(_rope64(p[:, 2048:2176], ci, si, first_half)).astype(BF16)
    wi_ref[...] = p[:, 2176:2304] * (IDX_HEADS ** -0.5)


def _proj_att(x, g, w, ca, sa, ci, si, tm):
    b, s, d = x.shape
    row = lambda c: pl.BlockSpec((None, tm, c), lambda bi, i: (bi, i, 0))
    col = lambda r: pl.BlockSpec((None, r, tm), lambda bi, i: (bi, 0, i))
    sds = jax.ShapeDtypeStruct
    return pl.pallas_call(
        _proj_att_kernel,
        grid=(b, s // tm),
        in_specs=[row(d), _const_spec((1, d)), _const_spec(w.shape),
                  row(128), row(128), row(128), row(128)],
        out_specs=[row(1024), col(256), row(256), row(512), col(128), row(128)],
        out_shape=[sds((b, s, 1024), BF16), sds((b, 256, s), BF16), sds((b, s, 256), BF16),
                   sds((b, s, 512), BF16), sds((b, 128, s), BF16), sds((b, s, 128), F32)],
        compiler_params=_cparams(("parallel", "parallel")),
        name="proj_att",
    )(x, g, w, ca, sa, ci, si)


def _proj_gdn_kernel(x_ref, g_ref, w_ref, qkv_ref, ab_ref, abt_ref):
    h = _rms(x_ref[...], g_ref[...]).astype(BF16)
    p = _dot(h, w_ref[...])
    qkv_ref[...] = p[:, :3072]
    ab = p[:, 3072:3200]
    ab_ref[...] = ab
    abt_ref[...] = jnp.transpose(ab)[:16, :]


def _proj_gdn(x, g, w, tm):
    b, s, d = x.shape
    row = lambda c: pl.BlockSpec((None, tm, c), lambda bi, i: (bi, i, 0))
    sds = jax.ShapeDtypeStruct
    return pl.pallas_call(
        _proj_gdn_kernel,
        grid=(b, s // tm),
        in_specs=[row(d), _const_spec((1, d)), _const_spec(w.shape)],
        out_specs=[row(3072), row(128), pl.BlockSpec((None, 16, tm), lambda bi, i: (bi, 0, i))],
        out_shape=[sds((b, s, 3072), F32), sds((b, s, 128), F32), sds((b, 16, s), F32)],
        compiler_params=_cparams(("parallel", "parallel")),
        name="proj_gdn",
    )(x, g, w)


def _key_of(v):
    b = pltpu.bitcast(v, I32)
    return jnp.where(b < 0, b ^ 0x7FFFFFFF, b)


def _val_of(k):
    return pltpu.bitcast(jnp.where(k < 0, k ^ 0x7FFFFFFF, k), F32)


def _dsa_kernel(q_ref, qi_ref, wi_ref, kt_ref, v_ref, kit_ref, o_ref,
                keys_ref, qg_ref, bias_ref, s_ref, p_ref, acc_ref, m_ref, l_ref, run_ref, tri_ref,
                *, tq, kb, topk, seq):
    t0 = pl.program_id(1) * tq
    nkb = (t0 + tq + kb - 1) // kb
    n_lane_tiles = kb // LANES
    row_ids = t0 + lax.broadcasted_iota(I32, (tq, kb), 0)
    col_iota = lax.broadcasted_iota(I32, (tq, kb), 1)
    kf = float(topk)

    qi = qi_ref[...]
    wi = wi_ref[...]
    w_heads = [wi[:, h:h + 1] for h in range(IDX_HEADS)]

    def score_block(j, carry):
        mx, mn = carry
        c0 = pl.multiple_of(j * kb, kb)
        kit = kit_ref[:, pl.ds(c0, kb)]
        sc = jnp.zeros((tq, kb), F32)
        for h in range(IDX_HEADS):
            lg = _dot(qi[:, h * 128:(h + 1) * 128], kit)
            sc = sc + w_heads[h] * jnp.maximum(lg, 0.0)
        causal = c0 + col_iota <= row_ids
        sc_hi = jnp.where(causal, sc, -jnp.inf)
        sc_lo = jnp.where(causal, sc, jnp.inf)
        for i in range(n_lane_tiles):
            mx = jnp.maximum(mx, sc_hi[:, i * LANES:(i + 1) * LANES])
            mn = jnp.minimum(mn, sc_lo[:, i * LANES:(i + 1) * LANES])
        keys_ref[:, pl.ds(c0, kb)] = _key_of(sc_hi)
        return mx, mn

    mx, mn = lax.fori_loop(0, nkb, score_block,
                           (jnp.full((tq, LANES), -jnp.inf, F32), jnp.full((tq, LANES), jnp.inf, F32)))
    row_max = jnp.max(mx, axis=1, keepdims=True)
    row_min = jnp.min(mn, axis=1, keepdims=True)

    n_row_tiles = tq // LANES if tq % LANES == 0 else 1
    rt = tq // n_row_tiles

    def count_ge(cands):
        nc = len(cands)
        tiles = range(n_row_tiles)
        cb = [[c if isinstance(c, int) else jnp.broadcast_to(c[t * rt:(t + 1) * rt], (rt, LANES))
               for t in tiles] for c in cands]

        def body(j, accs):
            c0 = pl.multiple_of(j * kb, kb)
            accs = list(accs)
            for t in tiles:
                for i in range(n_lane_tiles):
                    blk = keys_ref[t * rt:(t + 1) * rt, pl.ds(c0 + i * LANES, LANES)]
                    for k in range(nc):
                        accs[t * nc + k] = accs[t * nc + k] + jnp.where(blk >= cb[k][t], 1.0, 0.0)
            return tuple(accs)

        z = jnp.zeros((rt, LANES), F32)
        accs = lax.fori_loop(0, nkb, body, (z,) * (n_row_tiles * nc))
        return [jnp.concatenate([jnp.sum(accs[t * nc + k], axis=1, keepdims=True) for t in tiles], axis=0)
                for k in range(nc)]

    n_visible = (row_ids[:, :1] + 1).astype(F32)
    few = n_visible < kf
    n_cols = jnp.zeros((tq, 1), F32) + (nkb * kb).astype(F32)
    f_pos, f_zero = count_ge([1, 0])
    above = f_pos >= kf
    below = f_zero < kf
    at_zero = jnp.logical_and(jnp.logical_not(above), jnp.logical_not(below))
    lo0 = jnp.where(few, NEG_INF_KEY, jnp.where(above, 1, jnp.where(below, _key_of(row_min), 0)))
    flo0 = jnp.where(few, n_cols, jnp.where(above, f_pos, jnp.where(below, n_visible, f_zero)))
    hi0 = jnp.where(above, jnp.minimum(_key_of(row_max), 2147483646) + 1, jnp.where(below, 0, 1))
    fhi0 = jnp.where(above, 0.0, jnp.where(below, f_zero, f_pos))
    log_target = float(np.log(topk + 0.5))
    glog = lambda c: jnp.log(jnp.maximum(c, 0.5)) - log_target
    done0 = jnp.where(jnp.logical_or(jnp.logical_or(few, at_zero), flo0 == kf), 1.0, 0.0)
    n_interp = 24

    def search_cond(st):
        return jnp.logical_and(st[0] < n_interp + 34, st[1] > 0.0)

    def search_body(st):
        it, _, lo, hi, flo, fhi, glo, ghi, side, done = st
        v_lo, v_hi = _val_of(lo), _val_of(hi)
        c_interp = _key_of((v_lo * ghi - v_hi * glo) / (ghi - glo))
        c_mid = jnp.right_shift(lo, 1) + jnp.right_shift(hi, 1) + (lo & hi & 1)
        cand = jnp.where(it >= n_interp, c_mid, c_interp)
        cand = jnp.minimum(jnp.maximum(cand, lo + 1), hi - 1)
        c = count_ge([cand])[0]
        active = done < 0.5
        to_lo = jnp.logical_and(active, c >= kf)
        to_hi = jnp.logical_and(active, c < kf)
        gc = glog(c)
        ghi_n = jnp.where(to_hi, gc, jnp.where(jnp.logical_and(to_lo, side > 0.0), ghi * 0.5, ghi))
        glo_n = jnp.where(to_lo, gc, jnp.where(jnp.logical_and(to_hi, side < 0.0), glo * 0.5, glo))
        side = jnp.where(to_lo, 1.0, jnp.where(to_hi, -1.0, side))
        lo = jnp.where(to_lo, cand, lo)
        flo = jnp.where(to_lo, c, flo)
        hi = jnp.where(to_hi, cand, hi)
        fhi = jnp.where(to_hi, c, fhi)
        fin = jnp.logical_or(flo == kf, hi - 1 == lo)
        done = jnp.where(fin, 1.0, done)
        return it + 1, jnp.sum(1.0 - done), lo, hi, flo, fhi, glo_n, ghi_n, side, done

    st0 = (jnp.int32(0), jnp.sum(1.0 - done0), lo0, hi0, flo0, fhi0, glog(flo0), glog(fhi0),
           jnp.zeros((tq, 1), F32), done0)
    st = lax.while_loop(search_cond, search_body, st0)
    thr, cge, cgt = st[2], st[4], st[5]

    tie_rows = jnp.logical_and(cge > kf, jnp.logical_not(few))
    need = jnp.where(tie_rows, kf - cgt, float(2 * seq))
    any_tie = jnp.max(jnp.where(tie_rows, 1.0, 0.0)) > 0.0
    no_tie = jnp.logical_not(any_tie)
    run_ref[...] = jnp.zeros(run_ref.shape, F32)

    @pl.when(any_tie)
    def _():
        rr = lax.broadcasted_iota(I32, (kb, kb), 0)
        cc = lax.broadcasted_iota(I32, (kb, kb), 1)
        tri_ref[...] = jnp.where(rr <= cc, 1.0, 0.0).astype(BF16)

    rep = ATT_HEADS // ATT_KV_HEADS
    rows = rep * tq
    rc = 32
    for g in range(ATT_KV_HEADS):
        for r in range(rep):
            hd = g * rep + r
            qg_ref[g, r * tq:(r + 1) * tq, :] = q_ref[:, hd * 128:(hd + 1) * 128]
    acc_ref[...] = jnp.zeros(acc_ref.shape, F32)
    l_ref[...] = jnp.zeros(l_ref.shape, F32)
    m_ref[...] = jnp.full(m_ref.shape, NEG_BIG, F32)

    def attend_block(j, carry):
        c0 = pl.multiple_of(j * kb, kb)
        causal = c0 + col_iota <= row_ids

        @pl.when(no_tie)
        def _():
            sel = jnp.logical_and(keys_ref[:, pl.ds(c0, kb)] >= thr, causal)
            bias_ref[...] = jnp.where(sel, 0.0, NEG_BIG)

        @pl.when(any_tie)
        def _():
            keys = keys_ref[:, pl.ds(c0, kb)]
            tied = jnp.where(keys == thr, 1.0, 0.0)
            prefix = _dot(tied.astype(BF16), tri_ref[...])
            rank = run_ref[...] + prefix
            sel = jnp.where(keys > thr, 1.0, jnp.where(rank <= need, tied, 0.0))
            bias_ref[...] = jnp.where(jnp.logical_and(sel > 0.5, causal), 0.0, NEG_BIG)
            run_ref[...] = run_ref[...] + jnp.sum(tied, axis=1, keepdims=True)
        for g in range(ATT_KV_HEADS):
            s_ref[g] = _dot(qg_ref[g], kt_ref[g * 128:(g + 1) * 128, pl.ds(c0, kb)])
            for r0 in range(0, rows, rc):
                rs = slice(r0, r0 + rc)
                b0 = r0 % tq
                s = s_ref[g, rs, :] + bias_ref[b0:b0 + rc, :]
                m_prev = m_ref[g, rs, :]
                m_next = jnp.maximum(m_prev, jnp.max(s, axis=1, keepdims=True))
                p = jnp.exp2(s - jnp.concatenate([m_next] * n_lane_tiles, axis=1))
                alpha = jnp.exp2(m_prev - m_next)
                l_ref[g, rs, :] = alpha * l_ref[g, rs, :] + jnp.sum(p, axis=1, keepdims=True)
                m_ref[g, rs, :] = m_next
                acc_ref[g, rs, :] = acc_ref[g, rs, :] * alpha
                p_ref[g, rs, :] = p.astype(BF16)
            acc_ref[g] += _dot(p_ref[g], v_ref[pl.ds(c0, kb), g * 128:(g + 1) * 128])
        return carry

    lax.fori_loop(0, nkb, attend_block, 0)

    for g in range(ATT_KV_HEADS):
        out = acc_ref[g] / l_ref[g]
        for r in range(rep):
            hd = g * rep + r
            o_ref[:, hd * 128:(hd + 1) * 128] = out[r * tq:(r + 1) * tq]


def _dsa(q, qi, wi, kt, v, kit, topk, tq, kb):
    b, s, _ = q.shape
    rep = ATT_HEADS // ATT_KV_HEADS
    qspec = lambda c: pl.BlockSpec((None, tq, c), lambda bi, i: (bi, i, 0))
    per_batch = lambda r, c: pl.BlockSpec((None, r, c), lambda bi, i: (bi, 0, 0),
                                          pipeline_mode=pl.Buffered(1))
    return pl.pallas_call(
        functools.partial(_dsa_kernel, tq=tq, kb=kb, topk=topk, seq=s),
        grid=(b, s // tq),
        in_specs=[qspec(1024), qspec(512), qspec(128),
                  per_batch(256, s), per_batch(s, 256), per_batch(128, s)],
        out_specs=qspec(1024),
        out_shape=jax.ShapeDtypeStruct((b, s, 1024), F32),
        scratch_shapes=[pltpu.VMEM((tq, s), I32),
                        pltpu.VMEM((ATT_KV_HEADS, rep * tq, 128), BF16),
                        pltpu.VMEM((tq, kb), F32),
                        pltpu.VMEM((ATT_KV_HEADS, rep * tq, kb), F32),
                        pltpu.VMEM((ATT_KV_HEADS, rep * tq, kb), BF16),
                        pltpu.VMEM((ATT_KV_HEADS, rep * tq, 128), F32),
                        pltpu.VMEM((ATT_KV_HEADS, rep * tq, 128), F32),
                        pltpu.VMEM((ATT_KV_HEADS, rep * tq, 128), F32),
                        pltpu.VMEM((tq, 1), F32),
                        pltpu.VMEM((kb, kb), BF16)],
        compiler_params=_cparams(("parallel", "arbitrary")),
        name="dsa_attention",
    )(q, qi, wi, kt, v, kit)


def _softplus(x):
    return jnp.maximum(x, 0.0) + jnp.log(1.0 + jnp.exp(-jnp.abs(x)))


def _gdn_kernel(qkv_ref, ab_ref, abt_ref, cw_ref, ac_ref, dc_ref, ar_ref, dr_ref, gn_ref,
                y_ref, xbuf_ref, act_ref, state_ref, *, tb):
    n_chunks = tb // CHUNK

    @pl.when(pl.program_id(1) == 0)
    def _():
        xbuf_ref[0:8, :] = jnp.zeros((8, xbuf_ref.shape[1]), F32)
        state_ref[...] = jnp.zeros(state_ref.shape, F32)

    xbuf_ref[8:8 + tb, :] = qkv_ref[...]
    for c in range(xbuf_ref.shape[1] // LANES):
        sl = slice(c * LANES, (c + 1) * LANES)
        y = xbuf_ref[5:5 + tb, sl] * cw_ref[0:1, sl]
        for j in range(1, CONV_K):
            y = y + xbuf_ref[5 + j:5 + j + tb, sl] * cw_ref[j:j + 1, sl]
        act_ref[:, sl] = y * (1.0 / (1.0 + jnp.exp(-y)))
    xbuf_ref[0:8, :] = xbuf_ref[tb:tb + 8, :]

    ab = ab_ref[...]
    g_col = -jnp.exp(ac_ref[...]) * _softplus(ab + dc_ref[...])
    beta_col = 1.0 / (1.0 + jnp.exp(-ab))
    g_row = -jnp.exp(ar_ref[...]) * _softplus(abt_ref[0:8, :] + dr_ref[...])

    r = lax.broadcasted_iota(I32, (tb, tb), 0)
    c = lax.broadcasted_iota(I32, (tb, tb), 1)
    chunk_shift = CHUNK.bit_length() - 1
    same = jnp.right_shift(r, chunk_shift) == jnp.right_shift(c, chunk_shift)
    lower = jnp.logical_and(same, r >= c)
    strict = jnp.logical_and(same, r > c)
    lower_m = jnp.where(lower, 1.0, 0.0).astype(BF16)
    upper_m = jnp.where(jnp.logical_and(same, r <= c), 1.0, 0.0).astype(BF16)
    same_m = jnp.where(same, 1.0, 0.0).astype(BF16)
    eye = jnp.where(r == c, 1.0, 0.0)
    n_levels = CHUNK.bit_length() - 1
    lvl_masks = []
    for lvl in range(n_levels):
        sub = jnp.logical_and(jnp.right_shift(r, lvl + 1) == jnp.right_shift(c, lvl + 1),
                              jnp.logical_and((jnp.right_shift(r, lvl) & 1) == 1,
                                              (jnp.right_shift(c, lvl) & 1) == 0))
        lvl_masks.append(jnp.where(sub, 1.0, 0.0).astype(BF16))

    gparts = _split3(g_col)
    gc_col = _dot(lower_m, gparts[0]) + (_dot(lower_m, gparts[1]) + _dot(lower_m, gparts[2]))
    gl_col = _dot(same_m, gparts[0]) + (_dot(same_m, gparts[1]) + _dot(same_m, gparts[2]))
    rparts = _split3(g_row)
    gc_row = _dot(rparts[0], upper_m) + (_dot(rparts[1], upper_m) + _dot(rparts[2], upper_m))

    gn = gn_ref[...]
    hg = 4
    for h0 in range(0, GDN_HEADS, hg):
        heads = range(h0, h0 + hg)
        q_l, k_l, kbeta_l, vbeta_l, decay_l, gc_l, gl_l, m16_l, d_l = ([] for _ in range(9))
        for h in heads:
            q = act_ref[:, h * 128:(h + 1) * 128]
            k = act_ref[:, 1024 + h * 128:1024 + (h + 1) * 128]
            v = act_ref[:, 2048 + h * 128:2048 + (h + 1) * 128]
            q = q * lax.rsqrt(jnp.sum(q * q, axis=-1, keepdims=True) + EPS) * (GDN_DK ** -0.5)
            k = k * lax.rsqrt(jnp.sum(k * k, axis=-1, keepdims=True) + EPS)
            gc = gc_col[:, h:h + 1]
            beta = beta_col[:, 8 + h:9 + h]
            diff = gc - gc_row[h:h + 1, :]
            decay = jnp.where(lower, jnp.exp(jnp.where(lower, diff, 0.0)), 0.0)
            kbeta = k * beta
            nm = jnp.where(strict, -(_dot_nt(kbeta.astype(BF16), k.astype(BF16)) * decay), 0.0)
            q_l.append(q); k_l.append(k); kbeta_l.append(kbeta); vbeta_l.append(v * beta)
            decay_l.append(decay); gc_l.append(gc); gl_l.append(gl_col[:, h:h + 1])
            m16_l.append(nm.astype(BF16)); d_l.append(eye)
        for lvl in range(n_levels):
            for i in range(hg):
                dh = d_l[i].astype(BF16)
                lh = m16_l[i] * lvl_masks[lvl]
                th = _dot(dh, lh).astype(BF16)
                d_l[i] = d_l[i] + _dot(th, dh)
        u_l, w_l, attn_l, qd_l, kd_l = [], [], [], [], []
        for i in range(hg):
            egc = jnp.exp(gc_l[i])
            rhs = jnp.concatenate([vbeta_l[i], kbeta_l[i] * egc], axis=1)
            sol = _dot(d_l[i].astype(BF16), rhs.astype(BF16))
            u_l.append(sol[:, :128])
            w_l.append(sol[:, 128:].astype(BF16))
            k16 = k_l[i].astype(BF16)
            attn_l.append(jnp.where(lower, _dot_nt(q_l[i].astype(BF16), k16) * decay_l[i], 0.0).astype(BF16))
            qd_l.append((q_l[i] * egc).astype(BF16))
            kd_l.append(k_l[i] * jnp.exp(gl_l[i] - gc_l[i]))
        st_l = [state_ref[h] for h in heads]
        outs = [[] for _ in range(hg)]
        for ci in range(n_chunks):
            rc = slice(ci * CHUNK, (ci + 1) * CHUNK)
            for i in range(hg):
                st16 = st_l[i].astype(BF16)
                v_new = u_l[i][rc] - _dot(w_l[i][rc], st16)
                v_new16 = v_new.astype(BF16)
                outs[i].append(_dot(qd_l[i][rc], st16) + _dot(attn_l[i][rc, rc], v_new16))
                kd_t = jnp.transpose(kd_l[i][rc]).astype(BF16)
                st_l[i] = (st_l[i] * jnp.exp(gl_l[i][ci * CHUNK:ci * CHUNK + 1])
                           + _dot(kd_t, v_new16))
        for i, h in enumerate(heads):
            state_ref[h] = st_l[i]
            y_ref[:, h * 128:(h + 1) * 128] = _rms(jnp.concatenate(outs[i], axis=0), gn)


def _gdn(qkv, ab, abt, conv_w, ac, dc, ar, dr, gn, tb):
    b, s, _ = qkv.shape
    blk = lambda c: pl.BlockSpec((None, tb, c), lambda bi, i: (bi, i, 0))
    return pl.pallas_call(
        functools.partial(_gdn_kernel, tb=tb),
        grid=(b, s // tb),
        in_specs=[blk(3072), blk(128), pl.BlockSpec((None, 16, tb), lambda bi, i: (bi, 0, i)),
                  _const_spec(conv_w.shape), _const_spec(ac.shape), _const_spec(dc.shape),
                  _const_spec(ar.shape), _const_spec(dr.shape), _const_spec(gn.shape)],
        out_specs=blk(1024),
        out_shape=jax.ShapeDtypeStruct((b, s, 1024), F32),
        scratch_shapes=[pltpu.VMEM((tb + 8, 3072), F32),
                        pltpu.VMEM((tb, 3072), F32),
                        pltpu.VMEM((GDN_HEADS, GDN_DK, GDN_DV), F32)],
        compiler_params=_cparams(("parallel", "arbitrary")),
        name="gated_delta",
    )(qkv, ab, abt, conv_w, ac, dc, ar, dr, gn)


def _sigmoid(x):
    return 1.0 / (1.0 + jnp.exp(-x))


def _merge_kernel(x_ref, ya_ref, yb_ref, gpre_ref, wg_ref, wo_ref, gpost_ref, o_ref):
    x = x_ref[...]
    h = _rms(x, gpre_ref[...]).astype(BF16)
    gates = _dot(h, wg_ref[...])
    y = _sigmoid(gates[:, :1024]) * ya_ref[...] + _sigmoid(gates[:, 1024:]) * yb_ref[...]
    z = _dot(y.astype(BF16), wo_ref[...])
    o_ref[...] = x + _rms(z, gpost_ref[...])


def _merge(x2, ya, yb, gpre, wg, wo, gpost, tm):
    n, d = x2.shape
    row = pl.BlockSpec((tm, d), lambda i: (i, 0))
    return pl.pallas_call(
        _merge_kernel,
        grid=(n // tm,),
        in_specs=[row, row, row, _const_spec((1, d)), _const_spec(wg.shape),
                  _const_spec(wo.shape), _const_spec((1, d))],
        out_specs=row,
        out_shape=jax.ShapeDtypeStruct((n, d), F32),
        compiler_params=_cparams(("parallel",)),
        name="merge_out",
    )(x2, ya, yb, gpre, wg, wo, gpost)


def _mem_kv_kernel(m_ref, g_ref, w_ref, kv_ref):
    h = _rms(m_ref[...], g_ref[...]).astype(BF16)
    kv_ref[...] = _dot(h, w_ref[...]).astype(BF16)


def _mem_kv(mem2, g, w):
    n, d = mem2.shape
    return pl.pallas_call(
        _mem_kv_kernel,
        out_shape=jax.ShapeDtypeStruct((n, w.shape[1]), BF16),
        compiler_params=pltpu.CompilerParams(vmem_limit_bytes=VMEM_LIMIT),
        name="mem_kv",
    )(mem2, g, w)


def _xattn_kernel(x_ref, kt_ref, v_ref, gpre_ref, wq_ref, wo_ref, gpost_ref, o_ref):
    x = x_ref[...]
    d = x.shape[1]
    hd = d // XATT_HEADS
    h = _rms(x, gpre_ref[...]).astype(BF16)
    q = (_dot(h, wq_ref[...]) * (hd ** -0.5)).astype(BF16)
    heads = []
    for i in range(XATT_HEADS):
        sl = slice(i * hd, (i + 1) * hd)
        s = _dot(q[:, sl], kt_ref[sl, :])
        p = jnp.exp(s - jnp.max(s, axis=1, keepdims=True))
        o = _dot(p.astype(BF16), v_ref[:, sl]) / jnp.sum(p, axis=1, keepdims=True)
        heads.append(o.astype(BF16))
    z = _dot(jnp.concatenate(heads, axis=1), wo_ref[...])
    o_ref[...] = x + _rms(z, gpost_ref[...])


def _xattn(x, kt, v, gpre, wq, wo, gpost, tm):
    b, s, d = x.shape
    n_mem = v.shape[1]
    row = pl.BlockSpec((None, tm, d), lambda bi, i: (bi, i, 0))
    return pl.pallas_call(
        _xattn_kernel,
        grid=(b, s // tm),
        in_specs=[row,
                  pl.BlockSpec((None, d, n_mem), lambda bi, i: (bi, 0, 0)),
                  pl.BlockSpec((None, n_mem, d), lambda bi, i: (bi, 0, 0)),
                  _const_spec((1, d)), _const_spec(wq.shape), _const_spec(wo.shape),
                  _const_spec((1, d))],
        out_specs=row,
        out_shape=jax.ShapeDtypeStruct((b, s, d), F32),
        compiler_params=_cparams(("parallel", "parallel")),
        name="mem_xattn",
    )(x, kt, v, gpre, wq, wo, gpost)


def _ffn_kernel(x_ref, gpre_ref, wgu_ref, wd_ref, gpost_ref, o_ref, *, ff, fc):
    x = x_ref[...]
    h = _rms(x, gpre_ref[...]).astype(BF16)
    z = jnp.zeros(x.shape, F32)
    for c0 in range(0, ff, fc):
        gate = _dot(h, wgu_ref[:, c0:c0 + fc])
        up = _dot(h, wgu_ref[:, ff + c0:ff + c0 + fc])
        act = (gate * _sigmoid(gate) * up).astype(BF16)
        z = z + _dot(act, wd_ref[c0:c0 + fc, :])
    o_ref[...] = x + _rms(z, gpost_ref[...])


def _ffn(x2, gpre, wgu, wd, gpost, tm):
    n, d = x2.shape
    ff = wd.shape[0]
    fc = ff // 2 if (ff // 2) % LANES == 0 else ff
    row = pl.BlockSpec((tm, d), lambda i: (i, 0))
    return pl.pallas_call(
        functools.partial(_ffn_kernel, ff=ff, fc=fc),
        grid=(n // tm,),
        in_specs=[row, _const_spec((1, d)), _const_spec(wgu.shape), _const_spec(wd.shape),
                  _const_spec((1, d))],
        out_specs=row,
        out_shape=jax.ShapeDtypeStruct((n, d), F32),
        compiler_params=_cparams(("parallel",)),
        name="swiglu",
    )(x2, gpre, wgu, wd, gpost)


def _rope_tables(positions):
    pos = positions.astype(F32).reshape(-1)[:, None]

    def tables(dim, reps):
        half = dim // 2
        inv_freq = ROPE_THETA ** (-jnp.arange(half, dtype=F32) * 2.0 / dim)
        ang = pos * inv_freq
        cos, sin = jnp.cos(ang), jnp.sin(ang)
        return (jnp.tile(jnp.concatenate([cos, cos], axis=1), (1, reps)),
                jnp.tile(jnp.concatenate([-sin, sin], axis=1), (1, reps)))

    return tables(ATT_HEAD_DIM, 1) + tables(IDX_DIM, 2)


def _pick_tile(n, pref):
    t = min(pref, n)
    while n % t:
        t //= 2
    return t


def kernel(x, mem, positions, norm_mix_pre, w_in, conv_w, a_log, dt_bias, gdn_norm, w_out,
           norm_mix_post, norm_x_pre, norm_mem, w_xq, w_xkv, w_xo, norm_x_post,
           norm_ffn_pre, w_gu, w_down, norm_ffn_post):
    b, s, d = x.shape
    n = b * s
    n_mem = mem.shape[1]
    depth = w_in.shape[0]
    topk = min(TOPK_MAX, s // 4)
    tm = _pick_tile(s, 512)
    tq = _pick_tile(s, 256)
    kb = _pick_tile(s, 512)
    tb = _pick_tile(s, 256)

    ca, sa, ci, si = (t.reshape(b, s, 128) for t in _rope_tables(positions))

    a_q, a_kv, i_q, i_k, i_w = 1024, 256, 256, 64, 4
    b_qkv = 3072
    o_k = a_q
    o_v = o_k + a_kv
    o_iq = o_v + a_kv
    o_ik = o_iq + i_q
    o_iw = o_ik + i_k
    o_b = o_iw + i_w
    o_ba = o_b + b_qkv
    o_bb = o_ba + GDN_HEADS
    o_ga = o_bb + GDN_HEADS
    row1 = lambda v: v.reshape(1, -1).astype(F32)

    x2 = x.reshape(n, d)
    for l in range(depth):
        w = w_in[l]
        w_iq = jnp.pad(w[:, o_iq:o_ik].reshape(d, IDX_HEADS, IDX_DIM), ((0, 0), (0, 0), (0, 128 - IDX_DIM)))
        w_att = jnp.concatenate([
            w[:, :o_iq], w_iq.reshape(d, IDX_HEADS * 128),
            jnp.pad(w[:, o_ik:o_iw], ((0, 0), (0, 128 - i_k))),
            jnp.pad(w[:, o_iw:o_b], ((0, 0), (0, 128 - i_w)))], axis=1).astype(BF16)
        w_gdn = jnp.concatenate([
            w[:, o_b:o_ba], jnp.pad(w[:, o_ba:o_ga], ((0, 0), (0, 128 - 2 * GDN_HEADS)))], axis=1).astype(BF16)
        w_gate = w[:, o_ga:].astype(BF16)

        g_pre = row1(norm_mix_pre[l])
        x3 = x2.reshape(b, s, d)
        q, kt, v, qi, kit, wi = _proj_att(x3, g_pre, w_att, ca, sa, ci, si, tm)
        ya = _dsa(q, qi, wi, kt, v, kit, topk, tq, kb)

        qkv, ab, abt = _proj_gdn(x3, g_pre, w_gdn, tm)
        pad_lanes = lambda vec: jnp.pad(vec.astype(F32), (0, 128 - GDN_HEADS)).reshape(1, 128)
        bcast_rows = lambda vec: jnp.broadcast_to(vec.astype(F32)[:, None], (GDN_HEADS, tb))
        yb = _gdn(qkv, ab, abt, conv_w[l].astype(F32),
                  pad_lanes(a_log[l]), pad_lanes(dt_bias[l]), bcast_rows(a_log[l]), bcast_rows(dt_bias[l]),
                  row1(gdn_norm[l]), tb)

        x2 = _merge(x2, ya.reshape(n, 1024), yb.reshape(n, 1024), g_pre, w_gate,
                    w_out[l].astype(BF16), row1(norm_mix_post[l]), tm)

        kv = _mem_kv(mem.reshape(b * n_mem, d), row1(norm_mem[l]), w_xkv[l].astype(BF16))
        kv = kv.reshape(b, n_mem, 2 * d)
        kt_mem = jnp.swapaxes(kv[:, :, :d], 1, 2)
        x2 = _xattn(x2.reshape(b, s, d), kt_mem, kv[:, :, d:], row1(norm_x_pre[l]),
                    w_xq[l].astype(BF16), w_xo[l].astype(BF16), row1(norm_x_post[l]), tm).reshape(n, d)

        x2 = _ffn(x2, row1(norm_ffn_pre[l]), w_gu[l].astype(BF16), w_down[l].astype(BF16),
                  row1(norm_ffn_post[l]), tm)
    return x2.reshape(b, s, d)
```

```python
import functools

import jax
import jax.numpy as jnp
import numpy as np
from jax import lax
from jax.experimental import pallas as pl
from jax.experimental.pallas import tpu as pltpu

EPS = 1e-6
ROPE_THETA = 10000.0
ATT_HEADS = 8
ATT_KV_HEADS = 2
ATT_HEAD_DIM = 128
IDX_HEADS = 4
IDX_DIM = 64
TOPK_MAX = 256
GDN_HEADS = 8
GDN_DK = 128
GDN_DV = 128
CONV_K = 4
CHUNK = 64
XATT_HEADS = 4

LANES = 128
VMEM_LIMIT = 56 * 1024 * 1024

F32 = jnp.float32
BF16 = jnp.bfloat16
I32 = jnp.int32

NEG_BIG = -1e30
LOG2E = 1.4426950408889634
INT_MIN = -2147483648
NEG_INF_KEY = -2139095041


def _cparams(sem):
    return pltpu.CompilerParams(dimension_semantics=sem, vmem_limit_bytes=VMEM_LIMIT)


def _const_spec(shape):
    nd = len(shape)
    return pl.BlockSpec(shape, lambda *_: (0,) * nd, pipeline_mode=pl.Buffered(1))


def _rms(x, g):
    return x * lax.rsqrt(jnp.mean(x * x, axis=-1, keepdims=True) + EPS) * g


def _dot(a, b):
    return jnp.dot(a, b, preferred_element_type=F32)


def _dot_nt(a, b):
    return lax.dot_general(a, b, (((1,), (1,)), ((), ())), preferred_element_type=F32)


def _split3(a):
    a1 = a.astype(BF16)
    r1 = a - a1.astype(F32)
    a2 = r1.astype(BF16)
    a3 = (r1 - a2.astype(F32)).astype(BF16)
    return a1, a2, a3


A_Q, A_KV, I_Q, I_K, I_W, B_QKV = 1024, 256, 256, 64, 4, 3072
O_IQ = A_Q + 2 * A_KV
O_IK = O_IQ + I_Q
O_IW = O_IK + I_K
O_B = O_IW + I_W
O_BA = O_B + B_QKV
O_GA = O_BA + 2 * GDN_HEADS
IN_COLS = O_GA + 2 * 1024


def _prep_w_in_kernel(w_ref, att_ref, gdn_ref, gate_ref):
    w = w_ref[...]
    zeros = lambda c: jnp.zeros((w.shape[0], c), F32)
    att = [w[:, :O_IQ]]
    for h in range(IDX_HEADS):
        att += [w[:, O_IQ + h * IDX_DIM:O_IQ + (h + 1) * IDX_DIM], zeros(128 - IDX_DIM)]
    att += [w[:, O_IK:O_IW], zeros(128 - I_K), w[:, O_IW:O_B], zeros(128 - I_W)]
    att_ref[...] = jnp.concatenate(att, axis=1).astype(BF16)
    gdn_ref[...] = jnp.concatenate([w[:, O_B:O_GA], zeros(128 - 2 * GDN_HEADS)], axis=1).astype(BF16)
    gate_ref[...] = w[:, O_GA:].astype(BF16)


def _prep_w_in(w_in, layer):
    _, d, cols = w_in.shape
    assert cols == IN_COLS
    tr = 128
    widths = (O_IQ + IDX_HEADS * 128 + 256, B_QKV + 128, 2048)
    return pl.pallas_call(
        _prep_w_in_kernel,
        grid=(d // tr,),
        in_specs=[pl.BlockSpec((None, tr, cols), lambda i: (layer, i, 0))],
        out_specs=[pl.BlockSpec((tr, c), lambda i: (i, 0)) for c in widths],
        out_shape=[jax.ShapeDtypeStruct((d, c), BF16) for c in widths],
        compiler_params=_cparams(("parallel",)),
        name="prep_w_in",
    )(w_in)


def _rope128(x, cos, sin_signed):
    return x * cos + pltpu.roll(x, 64, 1) * sin_signed


def _rope64(x, cos, sin_signed, first_half):
    partner = jnp.where(first_half, pltpu.roll(x, 96, 1), pltpu.roll(x, 32, 1))
    return x * cos + partner * sin_signed


def _proj_att_kernel(x_ref, g_ref, w_ref, ca_ref, sa_ref, ci_ref, si_ref,
                     q_ref, kt_ref, v_ref, qi_ref, kit_ref, wi_ref):
    h = _rms(x_ref[...], g_ref[...]).astype(BF16)
    p = _dot(h, w_ref[...])
    ca, sa = ca_ref[...], sa_ref[...]
    ci, si = ci_ref[...], si_ref[...]
    q_scale = ATT_HEAD_DIM ** -0.5 * LOG2E
    for hd in range(ATT_HEADS):
        sl = slice(hd * 128, (hd + 1) * 128)
        q_ref[:, sl] = (_rope128(p[:, sl], ca, sa) * q_scale).astype(BF16)
    for g in range(ATT_KV_HEADS):
        k_g = _rope128(p[:, 1024 + g * 128:1024 + (g + 1) * 128], ca, sa)
        kt_ref[g * 128:(g + 1) * 128, :] = jnp.transpose(k_g).astype(BF16)
    v_ref[...] = p[:, 1280:1536].astype(BF16)
    lane = lax.broadcasted_iota(I32, ci.shape, 1)
    first_half = (lane & 63) < 32
    idx_scale = IDX_DIM ** -0.5
    for hd in range(IDX_HEADS):
        off = 1536 + hd * 128
        qi_ref[:, hd * 128:(hd + 1) * 128] = (
            _rope64(p[:, off:off + 128], ci, si, first_half) * idx_scale).astype(BF16)
    kit_ref[...] = jnp.transpose(_rope64(p[:, 2048:2176], ci, si, first_half)).astype(BF16)
    wi_ref[...] = p[:, 2176:2304] * (IDX_HEADS ** -0.5)


def _proj_att(x, g, w, ca, sa, ci, si, tm):
    b, s, d = x.shape
    row = lambda c: pl.BlockSpec((None, tm, c), lambda bi, i: (bi, i, 0))
    col = lambda r: pl.BlockSpec((None, r, tm), lambda bi, i: (bi, 0, i))
    sds = jax.ShapeDtypeStruct
    return pl.pallas_call(
        _proj_att_kernel,
        grid=(b, s // tm),
        in_specs=[row(d), _const_spec((1, d)), _const_spec(w.shape),
                  row(128), row(128), row(128), row(128)],
        out_specs=[row(1024), col(256), row(256), row(512), col(128), row(128)],
        out_shape=[sds((b, s, 1024), BF16), sds((b, 256, s), BF16), sds((b, s, 256), BF16),
                   sds((b, s, 512), BF16), sds((b, 128, s), BF16), sds((b, s, 128), F32)],
        compiler_params=_cparams(("parallel", "parallel")),
        name="proj_att",
    )(x, g, w, ca, sa, ci, si)


def _proj_gdn_kernel(x_ref, g_ref, w_ref, qkv_ref, ab_ref, abt_ref):
    h = _rms(x_ref[...], g_ref[...]).astype(BF16)
    p = _dot(h, w_ref[...])
    qkv_ref[...] = p[:, :3072]
    ab = p[:, 3072:3200]
    ab_ref[...] = ab
    abt_ref[...] = jnp.transpose(ab)[:16, :]


def _proj_gdn(x, g, w, tm):
    b, s, d = x.shape
    row = lambda c: pl.BlockSpec((None, tm, c), lambda bi, i: (bi, i, 0))
    sds = jax.ShapeDtypeStruct
    return pl.pallas_call(
        _proj_gdn_kernel,
        grid=(b, s // tm),
        in_specs=[row(d), _const_spec((1, d)), _const_spec(w.shape)],
        out_specs=[row(3072), row(128), pl.BlockSpec((None, 16, tm), lambda bi, i: (bi, 0, i))],
        out_shape=[sds((b, s, 3072), F32), sds((b, s, 128), F32), sds((b, 16, s), F32)],
        compiler_params=_cparams(("parallel", "parallel")),
        name="proj_gdn",
    )(x, g, w)


def _key_of(v):
    b = pltpu.bitcast(v, I32)
    return jnp.where(b < 0, b ^ 0x7FFFFFFF, b)


def _val_of(k):
    return pltpu.bitcast(jnp.where(k < 0, k ^ 0x7FFFFFFF, k), F32)


def _dsa_kernel(q_ref, qi_ref, wi_ref, kt_ref, v_ref, kit_ref, o_ref,
                keys_ref, qg_ref, bias_ref, s_ref, p_ref, acc_ref, m_ref, l_ref, run_ref, tri_ref,
                *, tq, kb, topk, seq):
    t0 = pl.program_id(1) * tq
    nkb = (t0 + tq + kb - 1) // kb
    n_lane_tiles = kb // LANES
    row_ids = t0 + lax.broadcasted_iota(I32, (tq, kb), 0)
    col_iota = lax.broadcasted_iota(I32, (tq, kb), 1)
    kf = float(topk)

    qi = qi_ref[...]
    wi = wi_ref[...]
    w_heads = [wi[:, h:h + 1] for h in range(IDX_HEADS)]

    def score_block(j, carry):
        mx, mn = carry
        c0 = pl.multiple_of(j * kb, kb)
        kit = kit_ref[:, pl.ds(c0, kb)]
        sc = jnp.zeros((tq, kb), F32)
        for h in range(IDX_HEADS):
            lg = _dot(qi[:, h * 128:(h + 1) * 128], kit)
            sc = sc + w_heads[h] * jnp.maximum(lg, 0.0)
        causal = c0 + col_iota <= row_ids
        sc_hi = jnp.where(causal, sc, -jnp.inf)
        sc_lo = jnp.where(causal, sc, jnp.inf)
        for i in range(n_lane_tiles):
            mx = jnp.maximum(mx, sc_hi[:, i * LANES:(i + 1) * LANES])
            mn = jnp.minimum(mn, sc_lo[:, i * LANES:(i + 1) * LANES])
        keys_ref[:, pl.ds(c0, kb)] = _key_of(sc_hi)
        return mx, mn

    mx, mn = lax.fori_loop(0, nkb, score_block,
                           (jnp.full((tq, LANES), -jnp.inf, F32), jnp.full((tq, LANES), jnp.inf, F32)))
    row_max = jnp.max(mx, axis=1, keepdims=True)
    row_min = jnp.min(mn, axis=1, keepdims=True)

    n_row_tiles = tq // LANES if tq % LANES == 0 else 1
    rt = tq // n_row_tiles
    ones8 = jnp.ones((8, LANES), BF16)

    def as_col(row):
        return jnp.transpose(row)[:, :1]

    def count_ge(cands):
        nc = len(cands)
        tiles = range(n_row_tiles)
        cb = [[c if isinstance(c, int) else jnp.broadcast_to(c[t * rt:(t + 1) * rt], (rt, LANES))
               for t in tiles] for c in cands]

        def body(j, accs):
            c0 = pl.multiple_of(j * kb, kb)
            accs = list(accs)
            for t in tiles:
                for i in range(n_lane_tiles):
                    blk = keys_ref[t * rt:(t + 1) * rt, pl.ds(c0 + i * LANES, LANES)]
                    for k in range(nc):
                        accs[t * nc + k] = accs[t * nc + k] + jnp.where(blk >= cb[k][t], 1.0, 0.0)
            return tuple(accs)

        z = jnp.zeros((rt, LANES), F32)
        accs = lax.fori_loop(0, nkb, body, (z,) * (n_row_tiles * nc))
        return [jnp.concatenate([_dot_nt(ones8, accs[t * nc + k].astype(BF16)) for t in tiles], axis=1)
                for k in range(nc)]

    as_row = lambda col: jnp.transpose(jnp.broadcast_to(col, (tq, 8)))
    n_visible = (t0 + lax.broadcasted_iota(I32, (8, tq), 1) + 1).astype(F32)
    few = n_visible < kf
    f_pos, f_zero = count_ge([1, 0])
    above = f_pos >= kf
    below = f_zero < kf
    at_zero = jnp.logical_and(jnp.logical_not(above), jnp.logical_not(below))
    key_min = _key_of(as_row(row_min))
    key_max = _key_of(as_row(row_max))
    lo0 = jnp.where(few, NEG_INF_KEY + 1, jnp.where(above, 1, jnp.where(below, key_min, 0)))
    flo0 = jnp.where(few, n_visible, jnp.where(above, f_pos, jnp.where(below, n_visible, f_zero)))
    hi0 = jnp.where(above, jnp.minimum(key_max, 2147483646) + 1, jnp.where(below, 0, 1))
    fhi0 = jnp.where(above, 0.0, jnp.where(below, f_zero, f_pos))
    log_target = float(np.log(topk + 0.5))
    glog = lambda c: jnp.log(jnp.maximum(c, 0.5)) - log_target
    done0 = jnp.where(jnp.logical_or(jnp.logical_or(few, at_zero), flo0 == kf), 1.0, 0.0)
    n_interp = 24

    def search_cond(st):
        return jnp.logical_and(st[0] < n_interp + 34, st[1] > 0.0)

    def search_body(st):
        it, _, lo, hi, flo, fhi, glo, ghi, side, done = st
        v_lo, v_hi = _val_of(lo), _val_of(hi)
        v_model = (v_lo * ghi - v_hi * glo) / (ghi - glo)
        inside = flo - fhi
        v_even = v_lo + (v_hi - v_lo) * ((flo - kf + 0.5) / (inside + 1.0))
        c_interp = _key_of(jnp.where(inside <= 64.0, v_even, v_model))
        c_mid = jnp.right_shift(lo, 1) + jnp.right_shift(hi, 1) + (lo & hi & 1)
        cand = jnp.where(it >= n_interp, c_mid, c_interp)
        cand = jnp.minimum(jnp.maximum(cand, lo + 1), hi - 1)
        c = count_ge([as_col(cand)])[0]
        active = done < 0.5
        to_lo = jnp.logical_and(active, c >= kf)
        to_hi = jnp.logical_and(active, c < kf)
        gc = glog(c)
        ghi_n = jnp.where(to_hi, gc, jnp.where(jnp.logical_and(to_lo, side > 0.0), ghi * 0.5, ghi))
        glo_n = jnp.where(to_lo, gc, jnp.where(jnp.logical_and(to_hi, side < 0.0), glo * 0.5, glo))
        side = jnp.where(to_lo, 1.0, jnp.where(to_hi, -1.0, side))
        lo = jnp.where(to_lo, cand, lo)
        flo = jnp.where(to_lo, c, flo)
        hi = jnp.where(to_hi, cand, hi)
        fhi = jnp.where(to_hi, c, fhi)
        fin = jnp.logical_or(flo == kf, hi - 1 == lo)
        done = jnp.where(fin, 1.0, done)
        return it + 1, jnp.sum(1.0 - done), lo, hi, flo, fhi, glo_n, ghi_n, side, done

    st0 = (jnp.int32(0), jnp.sum(1.0 - done0), lo0, hi0, flo0, fhi0, glog(flo0), glog(fhi0),
           jnp.zeros((8, tq), F32), done0)
    st = lax.while_loop(search_cond, search_body, st0)
    thr = as_col(st[2])
    tie_rows = jnp.logical_and(st[4] > kf, jnp.logical_not(few))
    need = as_col(jnp.where(tie_rows, kf - st[5], float(2 * seq)))

    run_ref[...] = jnp.zeros(run_ref.shape, F32)
    rr = lax.broadcasted_iota(I32, (kb, kb), 0)
    cc = lax.broadcasted_iota(I32, (kb, kb), 1)
    tri_ref[...] = jnp.where(rr <= cc, 1.0, 0.0).astype(BF16)

    rep = ATT_HEADS // ATT_KV_HEADS
    rows = rep * tq
    rc = 32
    for g in range(ATT_KV_HEADS):
        for r in range(rep):
            hd = g * rep + r
            qg_ref[g, r * tq:(r + 1) * tq, :] = q_ref[:, hd * 128:(hd + 1) * 128]
    acc_ref[...] = jnp.zeros(acc_ref.shape, F32)
    l_ref[...] = jnp.zeros(l_ref.shape, F32)
    m_ref[...] = jnp.full(m_ref.shape, NEG_BIG, F32)

    def attend_block(j, carry):
        c0 = pl.multiple_of(j * kb, kb)
        keys = keys_ref[:, pl.ds(c0, kb)]
        tied = keys == thr
        prefix = _dot(jnp.where(tied, 1.0, 0.0).astype(BF16), tri_ref[...])
        tied_bias = jnp.where(tied, 0.0, NEG_BIG)
        left = need - run_ref[...]
        bias_ref[...] = jnp.where(keys > thr, 0.0, jnp.where(prefix <= left, tied_bias, NEG_BIG))
        run_ref[...] = run_ref[...] + prefix[:, kb - 1:kb]
        for g in range(ATT_KV_HEADS):
            for a0 in range(0, kb, 256):
                s_ref[g, :, a0:a0 + 256] = _dot(qg_ref[g], kt_ref[g * 128:(g + 1) * 128, pl.ds(c0 + a0, 256)])
        for g in range(ATT_KV_HEADS):
            for r0 in range(0, rows, rc):
                rs = slice(r0, r0 + rc)
                b0 = r0 % tq
                s = s_ref[g, rs, :] + bias_ref[b0:b0 + rc, :]
                m_prev = m_ref[g, rs, :]
                m_next = jnp.maximum(m_prev, jnp.max(s, axis=1, keepdims=True))
                p = jnp.exp2(s - jnp.concatenate([m_next] * n_lane_tiles, axis=1))
                alpha = jnp.exp2(m_prev - m_next)
                l_ref[g, rs, :] = alpha * l_ref[g, rs, :] + jnp.sum(p, axis=1, keepdims=True)
                m_ref[g, rs, :] = m_next
                acc_ref[g, rs, :] = acc_ref[g, rs, :] * alpha
                p_ref[g, rs, :] = p.astype(BF16)
            acc_ref[g] += _dot(p_ref[g], v_ref[pl.ds(c0, kb), g * 128:(g + 1) * 128])
        return carry

    lax.fori_loop(0, nkb, attend_block, 0)

    for g in range(ATT_KV_HEADS):
        out = acc_ref[g] / l_ref[g]
        for r in range(rep):
            hd = g * rep + r
            o_ref[:, hd * 128:(hd + 1) * 128] = out[r * tq:(r + 1) * tq]


def _dsa(q, qi, wi, kt, v, kit, topk, tq, kb):
    b, s, _ = q.shape
    assert s <= 256 * LANES, "per-lane partial counts must stay exact in bf16"
    rep = ATT_HEADS // ATT_KV_HEADS
    qspec = lambda c: pl.BlockSpec((None, tq, c), lambda bi, i: (bi, i, 0))
    per_batch = lambda r, c: pl.BlockSpec((None, r, c), lambda bi, i: (bi, 0, 0),
                                          pipeline_mode=pl.Buffered(1))
    return pl.pallas_call(
        functools.partial(_dsa_kernel, tq=tq, kb=kb, topk=topk, seq=s),
        grid=(b, s // tq),
        in_specs=[qspec(1024), qspec(512), qspec(128),
                  per_batch(256, s), per_batch(s, 256), per_batch(128, s)],
        out_specs=qspec(1024),
        out_shape=jax.ShapeDtypeStruct((b, s, 1024), F32),
        scratch_shapes=[pltpu.VMEM((tq, s), I32),
                        pltpu.VMEM((ATT_KV_HEADS, rep * tq, 128), BF16),
                        pltpu.VMEM((tq, kb), F32),
                        pltpu.VMEM((ATT_KV_HEADS, rep * tq, kb), F32),
                        pltpu.VMEM((ATT_KV_HEADS, rep * tq, kb), BF16),
                        pltpu.VMEM((ATT_KV_HEADS, rep * tq, 128), F32),
                        pltpu.VMEM((ATT_KV_HEADS, rep * tq, 128), F32),
                        pltpu.VMEM((ATT_KV_HEADS, rep * tq, 128), F32),
                        pltpu.VMEM((tq, 1), F32),
                        pltpu.VMEM((kb, kb), BF16)],
        compiler_params=_cparams(("parallel", "arbitrary")),
        name="dsa_attention",
    )(q, qi, wi, kt, v, kit)


def _softplus(x):
    return jnp.maximum(x, 0.0) + jnp.log(1.0 + jnp.exp(-jnp.abs(x)))


def _gdn_kernel(qkv_ref, ab_ref, abt_ref, cw_ref, ac_ref, dc_ref, ar_ref, dr_ref, gn_ref,
                y_ref, xbuf_ref, act_ref, state_ref, *, tb):
    n_chunks = tb // CHUNK

    @pl.when(pl.program_id(1) == 0)
    def _():
        xbuf_ref[0:8, :] = jnp.zeros((8, xbuf_ref.shape[1]), F32)
        state_ref[...] = jnp.zeros(state_ref.shape, F32)

    xbuf_ref[8:8 + tb, :] = qkv_ref[...]
    for c in range(xbuf_ref.shape[1] // LANES):
        sl = slice(c * LANES, (c + 1) * LANES)
        y = xbuf_ref[5:5 + tb, sl] * cw_ref[0:1, sl]
        for j in range(1, CONV_K):
            y = y + xbuf_ref[5 + j:5 + j + tb, sl] * cw_ref[j:j + 1, sl]
        act_ref[:, sl] = y * (1.0 / (1.0 + jnp.exp(-y)))
    xbuf_ref[0:8, :] = xbuf_ref[tb:tb + 8, :]

    ab = ab_ref[...]
    g_col = -jnp.exp(ac_ref[...]) * _softplus(ab + dc_ref[...])
    beta_col = 1.0 / (1.0 + jnp.exp(-ab))
    g_row = -jnp.exp(ar_ref[...]) * _softplus(abt_ref[0:8, :] + dr_ref[...])

    r = lax.broadcasted_iota(I32, (tb, tb), 0)
    c = lax.broadcasted_iota(I32, (tb, tb), 1)
    chunk_shift = CHUNK.bit_length() - 1
    same = jnp.right_shift(r, chunk_shift) == jnp.right_shift(c, chunk_shift)
    lower = jnp.logical_and(same, r >= c)
    strict = jnp.logical_and(same, r > c)
    lower_m = jnp.where(lower, 1.0, 0.0).astype(BF16)
    upper_m = jnp.where(jnp.logical_and(same, r <= c), 1.0, 0.0).astype(BF16)
    same_m = jnp.where(same, 1.0, 0.0).astype(BF16)
    eye = jnp.where(r == c, 1.0, 0.0)
    n_levels = CHUNK.bit_length() - 1
    lvl_masks = []
    for lvl in range(n_levels):
        sub = jnp.logical_and(jnp.right_shift(r, lvl + 1) == jnp.right_shift(c, lvl + 1),
                              jnp.logical_and((jnp.right_shift(r, lvl) & 1) == 1,
                                              (jnp.right_shift(c, lvl) & 1) == 0))
        lvl_masks.append(jnp.where(sub, 1.0, 0.0).astype(BF16))

    gparts = _split3(g_col)
    gc_col = _dot(lower_m, gparts[0]) + (_dot(lower_m, gparts[1]) + _dot(lower_m, gparts[2]))
    gl_col = _dot(same_m, gparts[0]) + (_dot(same_m, gparts[1]) + _dot(same_m, gparts[2]))
    rparts = _split3(g_row)
    gc_row = _dot(rparts[0], upper_m) + (_dot(rparts[1], upper_m) + _dot(rparts[2], upper_m))

    gn = gn_ref[...]
    hg = 4
    for h0 in range(0, GDN_HEADS, hg):
        heads = range(h0, h0 + hg)
        q_l, k_l, kbeta_l, vbeta_l, decay_l, gc_l, gl_l, m16_l, d_l = ([] for _ in range(9))
        for h in heads:
            q = act_ref[:, h * 128:(h + 1) * 128]
            k = act_ref[:, 1024 + h * 128:1024 + (h + 1) * 128]
            v = act_ref[:, 2048 + h * 128:2048 + (h + 1) * 128]
            q = q * lax.rsqrt(jnp.sum(q * q, axis=-1, keepdims=True) + EPS) * (GDN_DK ** -0.5)
            k = k * lax.rsqrt(jnp.sum(k * k, axis=-1, keepdims=True) + EPS)
            gc = gc_col[:, h:h + 1]
            beta = beta_col[:, 8 + h:9 + h]
            diff = gc - gc_row[h:h + 1, :]
            decay = jnp.where(lower, jnp.exp(jnp.where(lower, diff, 0.0)), 0.0)
            kbeta = k * beta
            nm = jnp.where(strict, -(_dot_nt(kbeta.astype(BF16), k.astype(BF16)) * decay), 0.0)
            q_l.append(q); k_l.append(k); kbeta_l.append(kbeta); vbeta_l.append(v * beta)
            decay_l.append(decay); gc_l.append(gc); gl_l.append(gl_col[:, h:h + 1])
            m16_l.append(nm.astype(BF16)); d_l.append(eye)
        for lvl in range(n_levels):
            for i in range(hg):
                dh = d_l[i].astype(BF16)
                lh = m16_l[i] * lvl_masks[lvl]
                th = _dot(dh, lh).astype(BF16)
                d_l[i] = d_l[i] + _dot(th, dh)
        u_l, w_l, attn_l, qd_l, kd_l = [], [], [], [], []
        for i in range(hg):
            egc = jnp.exp(gc_l[i])
            rhs = jnp.concatenate([vbeta_l[i], kbeta_l[i] * egc], axis=1)
            sol = _dot(d_l[i].astype(BF16), rhs.astype(BF16))
            u_l.append(sol[:, :128])
            w_l.append(sol[:, 128:].astype(BF16))
            k16 = k_l[i].astype(BF16)
            attn_l.append(jnp.where(lower, _dot_nt(q_l[i].astype(BF16), k16) * decay_l[i], 0.0).astype(BF16))
            qd_l.append((q_l[i] * egc).astype(BF16))
            kd_l.append(k_l[i] * jnp.exp(gl_l[i] - gc_l[i]))
        st_l = [state_ref[h] for h in heads]
        outs = [[] for _ in range(hg)]
        for ci in range(n_chunks):
            rc = slice(ci * CHUNK, (ci + 1) * CHUNK)
            for i in range(hg):
                st16 = st_l[i].astype(BF16)
                v_new = u_l[i][rc] - _dot(w_l[i][rc], st16)
                v_new16 = v_new.astype(BF16)
                outs[i].append(_dot(qd_l[i][rc], st16) + _dot(attn_l[i][rc, rc], v_new16))
                kd_t = jnp.transpose(kd_l[i][rc]).astype(BF16)
                st_l[i] = (st_l[i] * jnp.exp(gl_l[i][ci * CHUNK:ci * CHUNK + 1])
                           + _dot(kd_t, v_new16))
        for i, h in enumerate(heads):
            state_ref[h] = st_l[i]
            y_ref[:, h * 128:(h + 1) * 128] = _rms(jnp.concatenate(outs[i], axis=0), gn)


def _gdn(qkv, ab, abt, conv_w, ac, dc, ar, dr, gn, tb):
    b, s, _ = qkv.shape
    blk = lambda c: pl.BlockSpec((None, tb, c), lambda bi, i: (bi, i, 0))
    return pl.pallas_call(
        functools.partial(_gdn_kernel, tb=tb),
        grid=(b, s // tb),
        in_specs=[blk(3072), blk(128), pl.BlockSpec((None, 16, tb), lambda bi, i: (bi, 0, i)),
                  _const_spec(conv_w.shape), _const_spec(ac.shape), _const_spec(dc.shape),
                  _const_spec(ar.shape), _const_spec(dr.shape), _const_spec(gn.shape)],
        out_specs=blk(1024),
        out_shape=jax.ShapeDtypeStruct((b, s, 1024), F32),
        scratch_shapes=[pltpu.VMEM((tb + 8, 3072), F32),
                        pltpu.VMEM((tb, 3072), F32),
                        pltpu.VMEM((GDN_HEADS, GDN_DK, GDN_DV), F32)],
        compiler_params=_cparams(("parallel", "arbitrary")),
        name="gated_delta",
    )(qkv, ab, abt, conv_w, ac, dc, ar, dr, gn)


def _sigmoid(x):
    return 1.0 / (1.0 + jnp.exp(-x))


def _merge_kernel(x_ref, ya_ref, yb_ref, gpre_ref, wg_ref, wo_ref, gpost_ref, o_ref):
    x = x_ref[...]
    h = _rms(x, gpre_ref[...]).astype(BF16)
    gates = _dot(h, wg_ref[...])
    y = _sigmoid(gates[:, :1024]) * ya_ref[...] + _sigmoid(gates[:, 1024:]) * yb_ref[...]
    z = _dot(y.astype(BF16), wo_ref[...])
    o_ref[...] = x + _rms(z, gpost_ref[...])


def _merge(x2, ya, yb, gpre, wg, wo, gpost, tm):
    n, d = x2.shape
    row = pl.BlockSpec((tm, d), lambda i: (i, 0))
    return pl.pallas_call(
        _merge_kernel,
        grid=(n // tm,),
        in_specs=[row, row, row, _const_spec((1, d)), _const_spec(wg.shape),
                  _const_spec(wo.shape), _const_spec((1, d))],
        out_specs=row,
        out_shape=jax.ShapeDtypeStruct((n, d), F32),
        compiler_params=_cparams(("parallel",)),
        name="merge_out",
    )(x2, ya, yb, gpre, wg, wo, gpost)


def _mem_kv_kernel(m_ref, g_ref, w_ref, kv_ref):
    h = _rms(m_ref[...], g_ref[...]).astype(BF16)
    kv_ref[...] = _dot(h, w_ref[...]).astype(BF16)


def _mem_kv(mem2, g, w):
    n, d = mem2.shape
    return pl.pallas_call(
        _mem_kv_kernel,
        out_shape=jax.ShapeDtypeStruct((n, w.shape[1]), BF16),
        compiler_params=pltpu.CompilerParams(vmem_limit_bytes=VMEM_LIMIT),
        name="mem_kv",
    )(mem2, g, w)


def _xattn_kernel(x_ref, kt_ref, v_ref, gpre_ref, wq_ref, wo_ref, gpost_ref, o_ref):
    x = x_ref[...]
    d = x.shape[1]
    hd = d // XATT_HEADS
    h = _rms(x, gpre_ref[...]).astype(BF16)
    q = (_dot(h, wq_ref[...]) * (hd ** -0.5)).astype(BF16)
    heads = []
    for i in range(XATT_HEADS):
        sl = slice(i * hd, (i + 1) * hd)
        s = _dot(q[:, sl], kt_ref[sl, :])
        p = jnp.exp(s - jnp.max(s, axis=1, keepdims=True))
        o = _dot(p.astype(BF16), v_ref[:, sl]) / jnp.sum(p, axis=1, keepdims=True)
        heads.append(o.astype(BF16))
    z = _dot(jnp.concatenate(heads, axis=1), wo_ref[...])
    o_ref[...] = x + _rms(z, gpost_ref[...])


def _xattn(x, kt, v, gpre, wq, wo, gpost, tm):
    b, s, d = x.shape
    n_mem = v.shape[1]
    row = pl.BlockSpec((None, tm, d), lambda bi, i: (bi, i, 0))
    return pl.pallas_call(
        _xattn_kernel,
        grid=(b, s // tm),
        in_specs=[row,
                  pl.BlockSpec((None, d, n_mem), lambda bi, i: (bi, 0, 0)),
                  pl.BlockSpec((None, n_mem, d), lambda bi, i: (bi, 0, 0)),
                  _const_spec((1, d)), _const_spec(wq.shape), _const_spec(wo.shape),
                  _const_spec((1, d))],
        out_specs=row,
        out_shape=jax.ShapeDtypeStruct((b, s, d), F32),
        compiler_params=_cparams(("parallel", "parallel")),
        name="mem_xattn",
    )(x, kt, v, gpre, wq, wo, gpost)


def _ffn_kernel(x_ref, gpre_ref, wgu_ref, wd_ref, gpost_ref, o_ref, *, ff, fc):
    x = x_ref[...]
    h = _rms(x, gpre_ref[...]).astype(BF16)
    z = jnp.zeros(x.shape, F32)
    for c0 in range(0, ff, fc):
        gate = _dot(h, wgu_ref[:, c0:c0 + fc])
        up = _dot(h, wgu_ref[:, ff + c0:ff + c0 + fc])
        act = (gate * _sigmoid(gate) * up).astype(BF16)
        z = z + _dot(act, wd_ref[c0:c0 + fc, :])
    o_ref[...] = x + _rms(z, gpost_ref[...])


def _ffn(x2, gpre, wgu, wd, gpost, tm):
    n, d = x2.shape
    ff = wd.shape[0]
    fc = ff // 2 if (ff // 2) % LANES == 0 else ff
    row = pl.BlockSpec((tm, d), lambda i: (i, 0))
    return pl.pallas_call(
        functools.partial(_ffn_kernel, ff=ff, fc=fc),
        grid=(n // tm,),
        in_specs=[row, _const_spec((1, d)), _const_spec(wgu.shape), _const_spec(wd.shape),
                  _const_spec((1, d))],
        out_specs=row,
        out_shape=jax.ShapeDtypeStruct((n, d), F32),
        compiler_params=_cparams(("parallel",)),
        name="swiglu",
    )(x2, gpre, wgu, wd, gpost)


def _rope_tables(positions):
    pos = positions.astype(F32).reshape(-1)[:, None]

    def tables(dim, reps):
        half = dim // 2
        inv_freq = ROPE_THETA ** (-jnp.arange(half, dtype=F32) * 2.0 / dim)
        ang = pos * inv_freq
        cos, sin = jnp.cos(ang), jnp.sin(ang)
        return (jnp.tile(jnp.concatenate([cos, cos], axis=1), (1, reps)),
                jnp.tile(jnp.concatenate([-sin, sin], axis=1), (1, reps)))

    return tables(ATT_HEAD_DIM, 1) + tables(IDX_DIM, 2)


def _pick_tile(n, pref):
    t = min(pref, n)
    while n % t:
        t //= 2
    return t


def kernel(x, mem, positions, norm_mix_pre, w_in, conv_w, a_log, dt_bias, gdn_norm, w_out,
           norm_mix_post, norm_x_pre, norm_mem, w_xq, w_xkv, w_xo, norm_x_post,
           norm_ffn_pre, w_gu, w_down, norm_ffn_post):
    b, s, d = x.shape
    n = b * s
    n_mem = mem.shape[1]
    depth = w_in.shape[0]
    topk = min(TOPK_MAX, s // 4)
    tm = _pick_tile(s, 512)
    tq = _pick_tile(s, 256)
    kb = _pick_tile(s, 512)
    tb = _pick_tile(s, 256)

    ca, sa, ci, si = (t.reshape(b, s, 128) for t in _rope_tables(positions))

    row1 = lambda v: v.reshape(1, -1).astype(F32)

    x2 = x.reshape(n, d)
    for l in range(depth):
        w_att, w_gdn, w_gate = _prep_w_in(w_in, l)

        g_pre = row1(norm_mix_pre[l])
        x3 = x2.reshape(b, s, d)
        q, kt, v, qi, kit, wi = _proj_att(x3, g_pre, w_att, ca, sa, ci, si, tm)
        ya = _dsa(q, qi, wi, kt, v, kit, topk, tq, kb)

        qkv, ab, abt = _proj_gdn(x3, g_pre, w_gdn, tm)
        pad_lanes = lambda vec: jnp.pad(vec.astype(F32), (0, 128 - GDN_HEADS)).reshape(1, 128)
        bcast_rows = lambda vec: jnp.broadcast_to(vec.astype(F32)[:, None], (GDN_HEADS, tb))
        yb = _gdn(qkv, ab, abt, conv_w[l].astype(F32),
                  pad_lanes(a_log[l]), pad_lanes(dt_bias[l]), bcast_rows(a_log[l]), bcast_rows(dt_bias[l]),
                  row1(gdn_norm[l]), tb)

        x2 = _merge(x2, ya.reshape(n, 1024), yb.reshape(n, 1024), g_pre, w_gate,
                    w_out[l].astype(BF16), row1(norm_mix_post[l]), tm)

        kv = _mem_kv(mem.reshape(b * n_mem, d), row1(norm_mem[l]), w_xkv[l].astype(BF16))
        kv = kv.reshape(b, n_mem, 2 * d)
        kt_mem = jnp.swapaxes(kv[:, :, :d], 1, 2)
        x2 = _xattn(x2.reshape(b, s, d), kt_mem, kv[:, :, d:], row1(norm_x_pre[l]),
                    w_xq[l].astype(BF16), w_xo[l].astype(BF16), row1(norm_x_post[l]), tm).reshape(n, d)

        x2 = _ffn(x2, row1(norm_ffn_pre[l]), w_gu[l].astype(BF16), w_down[l].astype(BF16),
                  row1(norm_ffn_post[l]), tm)
    return x2.reshape(b, s, d)
```

```python
import functools

import jax
import jax.numpy as jnp
import numpy as np
from jax import lax
from jax.experimental import pallas as pl
from jax.experimental.pallas import tpu as pltpu

EPS = 1e-6
ROPE_THETA = 10000.0
ATT_HEADS = 8
ATT_KV_HEADS = 2
ATT_HEAD_DIM = 128
IDX_HEADS = 4
IDX_DIM = 64
TOPK_MAX = 256
GDN_HEADS = 8
GDN_DK = 128
GDN_DV = 128
CONV_K = 4
CHUNK = 64
XATT_HEADS = 4

LANES = 128
VMEM_LIMIT = 56 * 1024 * 1024

F32 = jnp.float32
BF16 = jnp.bfloat16
I32 = jnp.int32

NEG_BIG = -1e30
LOG2E = 1.4426950408889634
INT_MIN = -2147483648
NEG_INF_KEY = -2139095041


def _cparams(sem):
    return pltpu.CompilerParams(dimension_semantics=sem, vmem_limit_bytes=VMEM_LIMIT)


def _const_spec(shape):
    nd = len(shape)
    return pl.BlockSpec(shape, lambda *_: (0,) * nd, pipeline_mode=pl.Buffered(1))


def _rms(x, g):
    return x * lax.rsqrt(jnp.mean(x * x, axis=-1, keepdims=True) + EPS) * g


def _dot(a, b):
    return jnp.dot(a, b, preferred_element_type=F32)


def _dot_nt(a, b):
    return lax.dot_general(a, b, (((1,), (1,)), ((), ())), preferred_element_type=F32)


def _split3(a):
    a1 = a.astype(BF16)
    r1 = a - a1.astype(F32)
    a2 = r1.astype(BF16)
    a3 = (r1 - a2.astype(F32)).astype(BF16)
    return a1, a2, a3


A_Q, A_KV, I_Q, I_K, I_W, B_QKV = 1024, 256, 256, 64, 4, 3072
O_IQ = A_Q + 2 * A_KV
O_IK = O_IQ + I_Q
O_IW = O_IK + I_K
O_B = O_IW + I_W
O_BA = O_B + B_QKV
O_GA = O_BA + 2 * GDN_HEADS
IN_COLS = O_GA + 2 * 1024


def _prep_w_in_kernel(w_ref, att_ref, gdn_ref, gate_ref):
    w = w_ref[...]
    zeros = lambda c: jnp.zeros((w.shape[0], c), F32)
    att = [w[:, :O_IQ]]
    for h in range(IDX_HEADS):
        att += [w[:, O_IQ + h * IDX_DIM:O_IQ + (h + 1) * IDX_DIM], zeros(128 - IDX_DIM)]
    att += [w[:, O_IK:O_IW], zeros(128 - I_K), w[:, O_IW:O_B], zeros(128 - I_W)]
    att_ref[...] = jnp.concatenate(att, axis=1).astype(BF16)
    gdn_ref[...] = jnp.concatenate([w[:, O_B:O_GA], zeros(128 - 2 * GDN_HEADS)], axis=1).astype(BF16)
    gate_ref[...] = w[:, O_GA:].astype(BF16)


def _prep_w_in(w_in, layer):
    _, d, cols = w_in.shape
    assert cols == IN_COLS
    tr = 128
    widths = (O_IQ + IDX_HEADS * 128 + 256, B_QKV + 128, 2048)
    return pl.pallas_call(
        _prep_w_in_kernel,
        grid=(d // tr,),
        in_specs=[pl.BlockSpec((None, tr, cols), lambda i: (layer, i, 0))],
        out_specs=[pl.BlockSpec((tr, c), lambda i: (i, 0)) for c in widths],
        out_shape=[jax.ShapeDtypeStruct((d, c), BF16) for c in widths],
        compiler_params=_cparams(("parallel",)),
        name="prep_w_in",
    )(w_in)


def _rope128(x, cos, sin_signed):
    return x * cos + pltpu.roll(x, 64, 1) * sin_signed


def _rope64(x, cos, sin_signed, first_half):
    partner = jnp.where(first_half, pltpu.roll(x, 96, 1), pltpu.roll(x, 32, 1))
    return x * cos + partner * sin_signed


def _proj_att_kernel(x_ref, g_ref, w_ref, ca_ref, sa_ref, ci_ref, si_ref,
                     q_ref, kt_ref, v_ref, qi_ref, kit_ref, wi_ref):
    h = _rms(x_ref[...], g_ref[...]).astype(BF16)
    p = _dot(h, w_ref[...])
    ca, sa = ca_ref[...], sa_ref[...]
    ci, si = ci_ref[...], si_ref[...]
    q_scale = ATT_HEAD_DIM ** -0.5 * LOG2E
    for hd in range(ATT_HEADS):
        sl = slice(hd * 128, (hd + 1) * 128)
        q_ref[:, sl] = (_rope128(p[:, sl], ca, sa) * q_scale).astype(BF16)
    for g in range(ATT_KV_HEADS):
        k_g = _rope128(p[:, 1024 + g * 128:1024 + (g + 1) * 128], ca, sa)
        kt_ref[g * 128:(g + 1) * 128, :] = jnp.transpose(k_g).astype(BF16)
    v_ref[...] = p[:, 1280:1536].astype(BF16)
    lane = lax.broadcasted_iota(I32, ci.shape, 1)
    first_half = (lane & 63) < 32
    idx_scale = IDX_DIM ** -0.5
    for hd in range(IDX_HEADS):
        off = 1536 + hd * 128
        qi_ref[:, hd * 128:(hd + 1) * 128] = (
            _rope64(p[:, off:off + 128], ci, si, first_half) * idx_scale).astype(BF16)
    kit_ref[...] = jnp.transpose(_rope64(p[:, 2048:2176], ci, si, first_half)).astype(BF16)
    wi_ref[...] = p[:, 2176:2304] * (IDX_HEADS ** -0.5)


def _proj_att(x, g, w, ca, sa, ci, si, tm):
    b, s, d = x.shape
    row = lambda c: pl.BlockSpec((None, tm, c), lambda bi, i: (bi, i, 0))
    col = lambda r: pl.BlockSpec((None, r, tm), lambda bi, i: (bi, 0, i))
    sds = jax.ShapeDtypeStruct
    return pl.pallas_call(
        _proj_att_kernel,
        grid=(b, s // tm),
        in_specs=[row(d), _const_spec((1, d)), _const_spec(w.shape),
                  row(128), row(128), row(128), row(128)],
        out_specs=[row(1024), col(256), row(256), row(512), col(128), row(128)],
        out_shape=[sds((b, s, 1024), BF16), sds((b, 256, s), BF16), sds((b, s, 256), BF16),
                   sds((b, s, 512), BF16), sds((b, 128, s), BF16), sds((b, s, 128), F32)],
        compiler_params=_cparams(("parallel", "parallel")),
        name="proj_att",
    )(x, g, w, ca, sa, ci, si)


def _proj_gdn_kernel(x_ref, g_ref, w_ref, qkv_ref, ab_ref, abt_ref):
    h = _rms(x_ref[...], g_ref[...]).astype(BF16)
    p = _dot(h, w_ref[...])
    qkv_ref[...] = p[:, :3072]
    ab = p[:, 3072:3200]
    ab_ref[...] = ab
    abt_ref[...] = jnp.transpose(ab)[:16, :]


def _proj_gdn(x, g, w, tm):
    b, s, d = x.shape
    row = lambda c: pl.BlockSpec((None, tm, c), lambda bi, i: (bi, i, 0))
    sds = jax.ShapeDtypeStruct
    return pl.pallas_call(
        _proj_gdn_kernel,
        grid=(b, s // tm),
        in_specs=[row(d), _const_spec((1, d)), _const_spec(w.shape)],
        out_specs=[row(3072), row(128), pl.BlockSpec((None, 16, tm), lambda bi, i: (bi, 0, i))],
        out_shape=[sds((b, s, 3072), F32), sds((b, s, 128), F32), sds((b, 16, s), F32)],
        compiler_params=_cparams(("parallel", "parallel")),
        name="proj_gdn",
    )(x, g, w)


def _key_of(v):
    b = pltpu.bitcast(v, I32)
    return jnp.where(b < 0, b ^ 0x7FFFFFFF, b)


def _val_of(k):
    return pltpu.bitcast(jnp.where(k < 0, k ^ 0x7FFFFFFF, k), F32)


def _dsa_kernel(q_ref, qi_ref, wi_ref, kt_ref, v_ref, kit_ref, o_ref,
                keys_ref, qg_ref, bias_ref, s_ref, p_ref, acc_ref, m_ref, l_ref, run_ref, tri_ref,
                *, tq, kb, topk, seq):
    t0 = pl.program_id(1) * tq
    nkb = (t0 + tq + kb - 1) // kb
    n_lane_tiles = kb // LANES
    row_ids = t0 + lax.broadcasted_iota(I32, (tq, kb), 0)
    col_iota = lax.broadcasted_iota(I32, (tq, kb), 1)
    kf = float(topk)

    qi = qi_ref[...]
    wi = wi_ref[...]
    w_heads = [wi[:, h:h + 1] for h in range(IDX_HEADS)]

    def score_block(j, carry):
        mx, mn = carry
        c0 = pl.multiple_of(j * kb, kb)
        kit = kit_ref[:, pl.ds(c0, kb)]
        sc = jnp.zeros((tq, kb), F32)
        for h in range(IDX_HEADS):
            lg = _dot(qi[:, h * 128:(h + 1) * 128], kit)
            sc = sc + w_heads[h] * jnp.maximum(lg, 0.0)
        sc = jnp.where(sc == 0.0, 0.0, sc)
        causal = c0 + col_iota <= row_ids
        sc_hi = jnp.where(causal, sc, -jnp.inf)
        sc_lo = jnp.where(causal, sc, jnp.inf)
        for i in range(n_lane_tiles):
            mx = jnp.maximum(mx, sc_hi[:, i * LANES:(i + 1) * LANES])
            mn = jnp.minimum(mn, sc_lo[:, i * LANES:(i + 1) * LANES])
        keys_ref[:, pl.ds(c0, kb)] = _key_of(sc_hi)
        return mx, mn

    mx, mn = lax.fori_loop(0, nkb, score_block,
                           (jnp.full((tq, LANES), -jnp.inf, F32), jnp.full((tq, LANES), jnp.inf, F32)))
    row_max = jnp.max(mx, axis=1, keepdims=True)
    row_min = jnp.min(mn, axis=1, keepdims=True)

    n_row_tiles = tq // LANES if tq % LANES == 0 else 1
    rt = tq // n_row_tiles
    ones8 = jnp.ones((8, LANES), BF16)

    def as_col(row):
        return jnp.transpose(row)[:, :1]

    def count_ge(cands):
        nc = len(cands)
        tiles = range(n_row_tiles)
        cb = [[c if isinstance(c, int) else jnp.broadcast_to(c[t * rt:(t + 1) * rt], (rt, LANES))
               for t in tiles] for c in cands]

        def tile_counts(t):
            def body(j, accs):
                c0 = pl.multiple_of(j * kb, kb)
                accs = list(accs)
                blk = keys_ref[t * rt:(t + 1) * rt, pl.ds(c0, kb)]
                for i in range(n_lane_tiles):
                    for k in range(nc):
                        accs[k] = accs[k] + jnp.where(blk[:, i * LANES:(i + 1) * LANES] >= cb[k][t], 1.0, 0.0)
                return tuple(accs)

            z = jnp.zeros((rt, LANES), F32)
            return lax.fori_loop(0, nkb, body, (z,) * nc)

        accs = [tile_counts(t) for t in tiles]
        return [jnp.concatenate([_dot_nt(ones8, accs[t][k].astype(BF16)) for t in tiles], axis=1)
                for k in range(nc)]

    as_row = lambda col: jnp.transpose(jnp.broadcast_to(col, (tq, 8)))
    n_visible = (t0 + lax.broadcasted_iota(I32, (8, tq), 1) + 1).astype(F32)
    few = n_visible < kf
    f_pos, f_zero = count_ge([1, 0])
    above = f_pos >= kf
    below = f_zero < kf
    at_zero = jnp.logical_and(jnp.logical_not(above), jnp.logical_not(below))
    key_min = _key_of(as_row(row_min))
    key_max = _key_of(as_row(row_max))
    lo0 = jnp.where(few, NEG_INF_KEY + 1, jnp.where(above, 1, jnp.where(below, key_min, 0)))
    flo0 = jnp.where(few, n_visible, jnp.where(above, f_pos, jnp.where(below, n_visible, f_zero)))
    hi0 = jnp.where(above, jnp.minimum(key_max, 2147483646) + 1, jnp.where(below, 0, 1))
    fhi0 = jnp.where(above, 0.0, jnp.where(below, f_zero, f_pos))
    log_target = float(np.log(topk + 0.5))
    glog = lambda c: jnp.log(jnp.maximum(c, 0.5)) - log_target
    done0 = jnp.where(jnp.logical_or(jnp.logical_or(few, at_zero), flo0 == kf), 1.0, 0.0)
    n_interp = 24

    def search_cond(st):
        return jnp.logical_and(st[0] < n_interp + 34, st[1] > 0.0)

    def search_body(st):
        it, _, lo, hi, flo, fhi, glo, ghi, side, done = st
        v_lo, v_hi = _val_of(lo), _val_of(hi)
        v_model = (v_lo * ghi - v_hi * glo) / (ghi - glo)
        inside = flo - fhi
        v_even = v_lo + (v_hi - v_lo) * ((flo - kf + 0.5) / (inside + 1.0))
        c_interp = _key_of(jnp.where(inside <= 64.0, v_even, v_model))
        c_mid = jnp.right_shift(lo, 1) + jnp.right_shift(hi, 1) + (lo & hi & 1)
        cand = jnp.where(it >= n_interp, c_mid, c_interp)
        cand = jnp.minimum(jnp.maximum(cand, lo + 1), hi - 1)
        c = count_ge([as_col(cand)])[0]
        active = done < 0.5
        to_lo = jnp.logical_and(active, c >= kf)
        to_hi = jnp.logical_and(active, c < kf)
        gc = glog(c)
        ghi_n = jnp.where(to_hi, gc, jnp.where(jnp.logical_and(to_lo, side > 0.0), ghi * 0.5, ghi))
        glo_n = jnp.where(to_lo, gc, jnp.where(jnp.logical_and(to_hi, side < 0.0), glo * 0.5, glo))
        side = jnp.where(to_lo, 1.0, jnp.where(to_hi, -1.0, side))
        lo = jnp.where(to_lo, cand, lo)
        flo = jnp.where(to_lo, c, flo)
        hi = jnp.where(to_hi, cand, hi)
        fhi = jnp.where(to_hi, c, fhi)
        fin = jnp.logical_or(flo == kf, hi - 1 == lo)
        done = jnp.where(fin, 1.0, done)
        return it + 1, jnp.sum(1.0 - done), lo, hi, flo, fhi, glo_n, ghi_n, side, done

    st0 = (jnp.int32(0), jnp.sum(1.0 - done0), lo0, hi0, flo0, fhi0, glog(flo0), glog(fhi0),
           jnp.zeros((8, tq), F32), done0)
    st = lax.while_loop(search_cond, search_body, st0)
    thr = as_col(st[2])
    tie_rows = jnp.logical_and(st[4] > kf, jnp.logical_not(few))
    need = as_col(jnp.where(tie_rows, kf - st[5], float(2 * seq)))

    run_ref[...] = jnp.zeros(run_ref.shape, F32)
    rr = lax.broadcasted_iota(I32, (kb, kb), 0)
    cc = lax.broadcasted_iota(I32, (kb, kb), 1)
    tri_ref[...] = jnp.where(rr <= cc, 1.0, 0.0).astype(BF16)

    rep = ATT_HEADS // ATT_KV_HEADS
    rows = rep * tq
    rc = 32
    for g in range(ATT_KV_HEADS):
        for r in range(rep):
            hd = g * rep + r
            qg_ref[g, r * tq:(r + 1) * tq, :] = q_ref[:, hd * 128:(hd + 1) * 128]
    acc_ref[...] = jnp.zeros(acc_ref.shape, F32)
    l_ref[...] = jnp.zeros(l_ref.shape, F32)
    m_ref[...] = jnp.full(m_ref.shape, NEG_BIG, F32)

    def attend_block(j, carry):
        c0 = pl.multiple_of(j * kb, kb)
        keys = keys_ref[:, pl.ds(c0, kb)]
        tied = keys == thr
        prefix = _dot(jnp.where(tied, 1.0, 0.0).astype(BF16), tri_ref[...])
        tied_bias = jnp.where(tied, 0.0, NEG_BIG)
        left = need - run_ref[...]
        bias_ref[...] = jnp.where(keys > thr, 0.0, jnp.where(prefix <= left, tied_bias, NEG_BIG))
        run_ref[...] = run_ref[...] + prefix[:, kb - 1:kb]
        for g in range(ATT_KV_HEADS):
            for a0 in range(0, kb, 256):
                s_ref[g, :, a0:a0 + 256] = _dot(qg_ref[g], kt_ref[g * 128:(g + 1) * 128, pl.ds(c0 + a0, 256)])
        for g in range(ATT_KV_HEADS):
            for r0 in range(0, rows, rc):
                rs = slice(r0, r0 + rc)
                b0 = r0 % tq
                s = s_ref[g, rs, :] + bias_ref[b0:b0 + rc, :]
                m_prev = m_ref[g, rs, :]
                m_next = jnp.maximum(m_prev, jnp.max(s, axis=1, keepdims=True))
                p = jnp.exp2(s - jnp.concatenate([m_next] * n_lane_tiles, axis=1))
                alpha = jnp.exp2(m_prev - m_next)
                l_ref[g, rs, :] = alpha * l_ref[g, rs, :] + jnp.sum(p, axis=1, keepdims=True)
                m_ref[g, rs, :] = m_next
                acc_ref[g, rs, :] = acc_ref[g, rs, :] * alpha
                p_ref[g, rs, :] = p.astype(BF16)
            acc_ref[g] += _dot(p_ref[g], v_ref[pl.ds(c0, kb), g * 128:(g + 1) * 128])
        return carry

    lax.fori_loop(0, nkb, attend_block, 0)

    for g in range(ATT_KV_HEADS):
        out = acc_ref[g] / l_ref[g]
        for r in range(rep):
            hd = g * rep + r
            o_ref[:, hd * 128:(hd + 1) * 128] = out[r * tq:(r + 1) * tq]


def _dsa(q, qi, wi, kt, v, kit, topk, tq, kb):
    b, s, _ = q.shape
    assert s <= 256 * LANES, "per-lane partial counts must stay exact in bf16"
    rep = ATT_HEADS // ATT_KV_HEADS
    qspec = lambda c: pl.BlockSpec((None, tq, c), lambda bi, i: (bi, i, 0))
    per_batch = lambda r, c: pl.BlockSpec((None, r, c), lambda bi, i: (bi, 0, 0),
                                          pipeline_mode=pl.Buffered(1))
    return pl.pallas_call(
        functools.partial(_dsa_kernel, tq=tq, kb=kb, topk=topk, seq=s),
        grid=(b, s // tq),
        in_specs=[qspec(1024), qspec(512), qspec(128),
                  per_batch(256, s), per_batch(s, 256), per_batch(128, s)],
        out_specs=qspec(1024),
        out_shape=jax.ShapeDtypeStruct((b, s, 1024), F32),
        scratch_shapes=[pltpu.VMEM((tq, s), I32),
                        pltpu.VMEM((ATT_KV_HEADS, rep * tq, 128), BF16),
                        pltpu.VMEM((tq, kb), F32),
                        pltpu.VMEM((ATT_KV_HEADS, rep * tq, kb), F32),
                        pltpu.VMEM((ATT_KV_HEADS, rep * tq, kb), BF16),
                        pltpu.VMEM((ATT_KV_HEADS, rep * tq, 128), F32),
                        pltpu.VMEM((ATT_KV_HEADS, rep * tq, 128), F32),
                        pltpu.VMEM((ATT_KV_HEADS, rep * tq, 128), F32),
                        pltpu.VMEM((tq, 1), F32),
                        pltpu.VMEM((kb, kb), BF16)],
        compiler_params=_cparams(("parallel", "arbitrary")),
        name="dsa_attention",
    )(q, qi, wi, kt, v, kit)


def _softplus(x):
    return jnp.maximum(x, 0.0) + jnp.log(1.0 + jnp.exp(-jnp.abs(x)))


def _gdn_kernel(qkv_ref, ab_ref, abt_ref, cw_ref, ac_ref, dc_ref, ar_ref, dr_ref, gn_ref,
                y_ref, xbuf_ref, act_ref, state_ref, *, tb):
    n_chunks = tb // CHUNK

    @pl.when(pl.program_id(1) == 0)
    def _():
        xbuf_ref[0:8, :] = jnp.zeros((8, xbuf_ref.shape[1]), F32)
        state_ref[...] = jnp.zeros(state_ref.shape, F32)

    xbuf_ref[8:8 + tb, :] = qkv_ref[...]
    for c in range(xbuf_ref.shape[1] // LANES):
        sl = slice(c * LANES, (c + 1) * LANES)
        y = xbuf_ref[5:5 + tb, sl] * cw_ref[0:1, sl]
        for j in range(1, CONV_K):
            y = y + xbuf_ref[5 + j:5 + j + tb, sl] * cw_ref[j:j + 1, sl]
        act_ref[:, sl] = y * (1.0 / (1.0 + jnp.exp(-y)))
    xbuf_ref[0:8, :] = xbuf_ref[tb:tb + 8, :]

    ab = ab_ref[...]
    g_col = -jnp.exp(ac_ref[...]) * _softplus(ab + dc_ref[...])
    beta_col = 1.0 / (1.0 + jnp.exp(-ab))
    g_row = -jnp.exp(ar_ref[...]) * _softplus(abt_ref[0:8, :] + dr_ref[...])

    r = lax.broadcasted_iota(I32, (tb, tb), 0)
    c = lax.broadcasted_iota(I32, (tb, tb), 1)
    chunk_shift = CHUNK.bit_length() - 1
    same = jnp.right_shift(r, chunk_shift) == jnp.right_shift(c, chunk_shift)
    lower = jnp.logical_and(same, r >= c)
    strict = jnp.logical_and(same, r > c)
    lower_m = jnp.where(lower, 1.0, 0.0).astype(BF16)
    upper_m = jnp.where(jnp.logical_and(same, r <= c), 1.0, 0.0).astype(BF16)
    same_m = jnp.where(same, 1.0, 0.0).astype(BF16)
    eye = jnp.where(r == c, 1.0, 0.0)
    n_levels = CHUNK.bit_length() - 1
    lvl_subs = []
    for lvl in range(n_levels):
        lvl_subs.append(jnp.logical_and(jnp.right_shift(r, lvl + 1) == jnp.right_shift(c, lvl + 1),
                                        jnp.logical_and((jnp.right_shift(r, lvl) & 1) == 1,
                                                        (jnp.right_shift(c, lvl) & 1) == 0)))
    lvl_masks = [jnp.where(sub, 1.0, 0.0).astype(BF16) for sub in lvl_subs]

    gparts = _split3(g_col)
    gc_col = _dot(lower_m, gparts[0]) + (_dot(lower_m, gparts[1]) + _dot(lower_m, gparts[2]))
    gl_col = _dot(same_m, gparts[0]) + (_dot(same_m, gparts[1]) + _dot(same_m, gparts[2]))
    rparts = _split3(g_row)
    gc_row = _dot(rparts[0], upper_m) + (_dot(rparts[1], upper_m) + _dot(rparts[2], upper_m))

    gn = gn_ref[...]
    hg = 4
    for h0 in range(0, GDN_HEADS, hg):
        heads = range(h0, h0 + hg)
        q_l, k_l, kbeta_l, vbeta_l, decay_l, gc_l, gl_l, m16_l, d_l = ([] for _ in range(9))
        for h in heads:
            q = act_ref[:, h * 128:(h + 1) * 128]
            k = act_ref[:, 1024 + h * 128:1024 + (h + 1) * 128]
            v = act_ref[:, 2048 + h * 128:2048 + (h + 1) * 128]
            q = q * lax.rsqrt(jnp.sum(q * q, axis=-1, keepdims=True) + EPS) * (GDN_DK ** -0.5)
            k = k * lax.rsqrt(jnp.sum(k * k, axis=-1, keepdims=True) + EPS)
            gc = gc_col[:, h:h + 1]
            beta = beta_col[:, 8 + h:9 + h]
            diff = gc - gc_row[h:h + 1, :]
            decay = jnp.where(lower, jnp.exp(jnp.where(lower, diff, 0.0)), 0.0)
            kbeta = k * beta
            nm = jnp.where(strict, -(_dot_nt(kbeta.astype(BF16), k.astype(BF16)) * decay), 0.0)
            q_l.append(q); k_l.append(k); kbeta_l.append(kbeta); vbeta_l.append(v * beta)
            decay_l.append(decay); gc_l.append(gc); gl_l.append(gl_col[:, h:h + 1])
            m16_l.append(nm.astype(BF16)); d_l.append(eye + jnp.where(lvl_subs[0], nm, 0.0))
        for lvl in range(1, n_levels):
            for i in range(hg):
                dh = d_l[i].astype(BF16)
                lh = m16_l[i] * lvl_masks[lvl]
                th = _dot(dh, lh).astype(BF16)
                d_l[i] = d_l[i] + _dot(th, dh)
        u_l, w_l, attn_l, qd_l, kd_l = [], [], [], [], []
        for i in range(hg):
            egc = jnp.exp(gc_l[i])
            rhs = jnp.concatenate([vbeta_l[i], kbeta_l[i] * egc], axis=1)
            sol = _dot(d_l[i].astype(BF16), rhs.astype(BF16))
            u_l.append(sol[:, :128])
            w_l.append(sol[:, 128:].astype(BF16))
            k16 = k_l[i].astype(BF16)
            attn_l.append(jnp.where(lower, _dot_nt(q_l[i].astype(BF16), k16) * decay_l[i], 0.0).astype(BF16))
            qd_l.append((q_l[i] * egc).astype(BF16))
            kd_l.append(k_l[i] * jnp.exp(gl_l[i] - gc_l[i]))
        st_l = [state_ref[h] for h in heads]
        outs = [[] for _ in range(hg)]
        for ci in range(n_chunks):
            rc = slice(ci * CHUNK, (ci + 1) * CHUNK)
            for i in range(hg):
                st16 = st_l[i].astype(BF16)
                v_new = u_l[i][rc] - _dot(w_l[i][rc], st16)
                v_new16 = v_new.astype(BF16)
                outs[i].append(_dot(qd_l[i][rc], st16) + _dot(attn_l[i][rc, rc], v_new16))
                kd_t = jnp.transpose(kd_l[i][rc]).astype(BF16)
                st_l[i] = (st_l[i] * jnp.exp(gl_l[i][ci * CHUNK:ci * CHUNK + 1])
                           + _dot(kd_t, v_new16))
        for i, h in enumerate(heads):
            state_ref[h] = st_l[i]
            y_ref[:, h * 128:(h + 1) * 128] = _rms(jnp.concatenate(outs[i], axis=0), gn)


def _gdn(qkv, ab, abt, conv_w, ac, dc, ar, dr, gn, tb):
    b, s, _ = qkv.shape
    blk = lambda c: pl.BlockSpec((None, tb, c), lambda bi, i: (bi, i, 0))
    return pl.pallas_call(
        functools.partial(_gdn_kernel, tb=tb),
        grid=(b, s // tb),
        in_specs=[blk(3072), blk(128), pl.BlockSpec((None, 16, tb), lambda bi, i: (bi, 0, i)),
                  _const_spec(conv_w.shape), _const_spec(ac.shape), _const_spec(dc.shape),
                  _const_spec(ar.shape), _const_spec(dr.shape), _const_spec(gn.shape)],
        out_specs=blk(1024),
        out_shape=jax.ShapeDtypeStruct((b, s, 1024), F32),
        scratch_shapes=[pltpu.VMEM((tb + 8, 3072), F32),
                        pltpu.VMEM((tb, 3072), F32),
                        pltpu.VMEM((GDN_HEADS, GDN_DK, GDN_DV), F32)],
        compiler_params=_cparams(("parallel", "arbitrary")),
        name="gated_delta",
    )(qkv, ab, abt, conv_w, ac, dc, ar, dr, gn)


def _sigmoid(x):
    return 1.0 / (1.0 + jnp.exp(-x))


def _merge_kernel(x_ref, ya_ref, yb_ref, gpre_ref, wg_ref, wo_ref, gpost_ref, o_ref):
    x = x_ref[...]
    h = _rms(x, gpre_ref[...]).astype(BF16)
    gates = _dot(h, wg_ref[...])
    y = _sigmoid(gates[:, :1024]) * ya_ref[...] + _sigmoid(gates[:, 1024:]) * yb_ref[...]
    z = _dot(y.astype(BF16), wo_ref[...])
    o_ref[...] = x + _rms(z, gpost_ref[...])


def _merge(x2, ya, yb, gpre, wg, wo, gpost, tm):
    n, d = x2.shape
    row = pl.BlockSpec((tm, d), lambda i: (i, 0))
    return pl.pallas_call(
        _merge_kernel,
        grid=(n // tm,),
        in_specs=[row, row, row, _const_spec((1, d)), _const_spec(wg.shape),
                  _const_spec(wo.shape), _const_spec((1, d))],
        out_specs=row,
        out_shape=jax.ShapeDtypeStruct((n, d), F32),
        compiler_params=_cparams(("parallel",)),
        name="merge_out",
    )(x2, ya, yb, gpre, wg, wo, gpost)


def _mem_kv_kernel(m_ref, g_ref, w_ref, kv_ref):
    h = _rms(m_ref[...], g_ref[...]).astype(BF16)
    kv_ref[...] = _dot(h, w_ref[...]).astype(BF16)


def _mem_kv(mem2, g, w):
    n, d = mem2.shape
    return pl.pallas_call(
        _mem_kv_kernel,
        out_shape=jax.ShapeDtypeStruct((n, w.shape[1]), BF16),
        compiler_params=pltpu.CompilerParams(vmem_limit_bytes=VMEM_LIMIT),
        name="mem_kv",
    )(mem2, g, w)


def _xattn_kernel(x_ref, kt_ref, v_ref, gpre_ref, wq_ref, wo_ref, gpost_ref, o_ref):
    x = x_ref[...]
    d = x.shape[1]
    hd = d // XATT_HEADS
    h = _rms(x, gpre_ref[...]).astype(BF16)
    q = (_dot(h, wq_ref[...]) * (hd ** -0.5)).astype(BF16)
    heads = []
    for i in range(XATT_HEADS):
        sl = slice(i * hd, (i + 1) * hd)
        s = _dot(q[:, sl], kt_ref[sl, :])
        p = jnp.exp(s - jnp.max(s, axis=1, keepdims=True))
        o = _dot(p.astype(BF16), v_ref[:, sl]) / jnp.sum(p, axis=1, keepdims=True)
        heads.append(o.astype(BF16))
    z = _dot(jnp.concatenate(heads, axis=1), wo_ref[...])
    o_ref[...] = x + _rms(z, gpost_ref[...])


def _xattn(x, kt, v, gpre, wq, wo, gpost, tm):
    b, s, d = x.shape
    n_mem = v.shape[1]
    row = pl.BlockSpec((None, tm, d), lambda bi, i: (bi, i, 0))
    return pl.pallas_call(
        _xattn_kernel,
        grid=(b, s // tm),
        in_specs=[row,
                  pl.BlockSpec((None, d, n_mem), lambda bi, i: (bi, 0, 0)),
                  pl.BlockSpec((None, n_mem, d), lambda bi, i: (bi, 0, 0)),
                  _const_spec((1, d)), _const_spec(wq.shape), _const_spec(wo.shape),
                  _const_spec((1, d))],
        out_specs=row,
        out_shape=jax.ShapeDtypeStruct((b, s, d), F32),
        compiler_params=_cparams(("parallel", "parallel")),
        name="mem_xattn",
    )(x, kt, v, gpre, wq, wo, gpost)


def _ffn_kernel(x_ref, gpre_ref, wgu_ref, wd_ref, gpost_ref, o_ref, *, ff, fc):
    x = x_ref[...]
    h = _rms(x, gpre_ref[...]).astype(BF16)
    z = jnp.zeros(x.shape, F32)
    for c0 in range(0, ff, fc):
        gate = _dot(h, wgu_ref[:, c0:c0 + fc])
        up = _dot(h, wgu_ref[:, ff + c0:ff + c0 + fc])
        act = (gate * _sigmoid(gate) * up).astype(BF16)
        z = z + _dot(act, wd_ref[c0:c0 + fc, :])
    o_ref[...] = x + _rms(z, gpost_ref[...])


def _ffn(x2, gpre, wgu, wd, gpost, tm):
    n, d = x2.shape
    ff = wd.shape[0]
    fc = ff // 2 if (ff // 2) % LANES == 0 else ff
    row = pl.BlockSpec((tm, d), lambda i: (i, 0))
    return pl.pallas_call(
        functools.partial(_ffn_kernel, ff=ff, fc=fc),
        grid=(n // tm,),
        in_specs=[row, _const_spec((1, d)), _const_spec(wgu.shape), _const_spec(wd.shape),
                  _const_spec((1, d))],
        out_specs=row,
        out_shape=jax.ShapeDtypeStruct((n, d), F32),
        compiler_params=_cparams(("parallel",)),
        name="swiglu",
    )(x2, gpre, wgu, wd, gpost)


def _rope_tables(positions):
    pos = positions.astype(F32).reshape(-1)[:, None]

    def tables(dim, reps):
        half = dim // 2
        inv_freq = ROPE_THETA ** (-jnp.arange(half, dtype=F32) * 2.0 / dim)
        ang = pos * inv_freq
        cos, sin = jnp.cos(ang), jnp.sin(ang)
        return (jnp.tile(jnp.concatenate([cos, cos], axis=1), (1, reps)),
                jnp.tile(jnp.concatenate([-sin, sin], axis=1), (1, reps)))

    return tables(ATT_HEAD_DIM, 1) + tables(IDX_DIM, 2)


def _pick_tile(n, pref):
    t = min(pref, n)
    while n % t:
        t //= 2
    return t


def kernel(x, mem, positions, norm_mix_pre, w_in, conv_w, a_log, dt_bias, gdn_norm, w_out,
           norm_mix_post, norm_x_pre, norm_mem, w_xq, w_xkv, w_xo, norm_x_post,
           norm_ffn_pre, w_gu, w_down, norm_ffn_post):
    b, s, d = x.shape
    n = b * s
    n_mem = mem.shape[1]
    depth = w_in.shape[0]
    topk = min(TOPK_MAX, s // 4)
    tm = _pick_tile(s, 512)
    tq = _pick_tile(s, 256)
    kb = _pick_tile(s, 512)
    tb = _pick_tile(s, 256)

    ca, sa, ci, si = (t.reshape(b, s, 128) for t in _rope_tables(positions))

    row1 = lambda v: v.reshape(1, -1).astype(F32)

    x2 = x.reshape(n, d)
    for l in range(depth):
        w_att, w_gdn, w_gate = _prep_w_in(w_in, l)

        g_pre = row1(norm_mix_pre[l])
        x3 = x2.reshape(b, s, d)
        q, kt, v, qi, kit, wi = _proj_att(x3, g_pre, w_att, ca, sa, ci, si, tm)
        ya = _dsa(q, qi, wi, kt, v, kit, topk, tq, kb)

        qkv, ab, abt = _proj_gdn(x3, g_pre, w_gdn, tm)
        pad_lanes = lambda vec: jnp.pad(vec.astype(F32), (0, 128 - GDN_HEADS)).reshape(1, 128)
        bcast_rows = lambda vec: jnp.broadcast_to(vec.astype(F32)[:, None], (GDN_HEADS, tb))
        yb = _gdn(qkv, ab, abt, conv_w[l].astype(F32),
                  pad_lanes(a_log[l]), pad_lanes(dt_bias[l]), bcast_rows(a_log[l]), bcast_rows(dt_bias[l]),
                  row1(gdn_norm[l]), tb)

        x2 = _merge(x2, ya.reshape(n, 1024), yb.reshape(n, 1024), g_pre, w_gate,
                    w_out[l].astype(BF16), row1(norm_mix_post[l]), tm)

        kv = _mem_kv(mem.reshape(b * n_mem, d), row1(norm_mem[l]), w_xkv[l].astype(BF16))
        kv = kv.reshape(b, n_mem, 2 * d)
        kt_mem = jnp.swapaxes(kv[:, :, :d], 1, 2)
        x2 = _xattn(x2.reshape(b, s, d), kt_mem, kv[:, :, d:], row1(norm_x_pre[l]),
                    w_xq[l].astype(BF16), w_xo[l].astype(BF16), row1(norm_x_post[l]), tm).reshape(n, d)

        x2 = _ffn(x2, row1(norm_ffn_pre[l]), w_gu[l].astype(BF16), w_down[l].astype(BF16),
                  row1(norm_ffn_post[l]), tm)
    return x2.reshape(b, s, d)
```

```python
import functools

import jax
import jax.numpy as jnp
import numpy as np
from jax import lax
from jax.experimental import pallas as pl
from jax.experimental.pallas import tpu as pltpu

EPS = 1e-6
ROPE_THETA = 10000.0
ATT_HEADS = 8
ATT_KV_HEADS = 2
ATT_HEAD_DIM = 128
IDX_HEADS = 4
IDX_DIM = 64
TOPK_MAX = 256
GDN_HEADS = 8
GDN_DK = 128
GDN_DV = 128
CONV_K = 4
CHUNK = 64
XATT_HEADS = 4

LANES = 128
VMEM_LIMIT = 56 * 1024 * 1024

F32 = jnp.float32
BF16 = jnp.bfloat16
I32 = jnp.int32

NEG_BIG = -1e30
LOG2E = 1.4426950408889634
INT_MIN = -2147483648
INT_MAX = 2147483647
NEG_INF_KEY = -2139095041


def _cparams(sem):
    return pltpu.CompilerParams(dimension_semantics=sem, vmem_limit_bytes=VMEM_LIMIT)


def _const_spec(shape):
    nd = len(shape)
    return pl.BlockSpec(shape, lambda *_: (0,) * nd, pipeline_mode=pl.Buffered(1))


def _rms(x, g):
    return x * lax.rsqrt(jnp.mean(x * x, axis=-1, keepdims=True) + EPS) * g


def _dot(a, b):
    return jnp.dot(a, b, preferred_element_type=F32)


def _dot_nt(a, b):
    return lax.dot_general(a, b, (((1,), (1,)), ((), ())), preferred_element_type=F32)


def _split3(a):
    a1 = a.astype(BF16)
    r1 = a - a1.astype(F32)
    a2 = r1.astype(BF16)
    a3 = (r1 - a2.astype(F32)).astype(BF16)
    return a1, a2, a3


A_Q, A_KV, I_Q, I_K, I_W, B_QKV = 1024, 256, 256, 64, 4, 3072
O_IQ = A_Q + 2 * A_KV
O_IK = O_IQ + I_Q
O_IW = O_IK + I_K
O_B = O_IW + I_W
O_BA = O_B + B_QKV
O_GA = O_BA + 2 * GDN_HEADS
IN_COLS = O_GA + 2 * 1024


def _prep_w_in_kernel(w_ref, att_ref, gdn_ref, gate_ref):
    w = w_ref[...]
    zeros = lambda c: jnp.zeros((w.shape[0], c), F32)
    att = [w[:, :O_IQ]]
    for h in range(IDX_HEADS):
        att += [w[:, O_IQ + h * IDX_DIM:O_IQ + (h + 1) * IDX_DIM], zeros(128 - IDX_DIM)]
    att += [w[:, O_IK:O_IW], zeros(128 - I_K), w[:, O_IW:O_B], zeros(128 - I_W)]
    att_ref[...] = jnp.concatenate(att, axis=1).astype(BF16)
    gdn_ref[...] = jnp.concatenate([w[:, O_B:O_GA], zeros(128 - 2 * GDN_HEADS)], axis=1).astype(BF16)
    gate_ref[...] = w[:, O_GA:].astype(BF16)


def _prep_w_in(w_in, layer):
    _, d, cols = w_in.shape
    assert cols == IN_COLS
    tr = 128
    widths = (O_IQ + IDX_HEADS * 128 + 256, B_QKV + 128, 2048)
    return pl.pallas_call(
        _prep_w_in_kernel,
        grid=(d // tr,),
        in_specs=[pl.BlockSpec((None, tr, cols), lambda i: (layer, i, 0))],
        out_specs=[pl.BlockSpec((tr, c), lambda i: (i, 0)) for c in widths],
        out_shape=[jax.ShapeDtypeStruct((d, c), BF16) for c in widths],
        compiler_params=_cparams(("parallel",)),
        name="prep_w_in",
    )(w_in)


def _rope128(x, cos, sin_signed):
    return x * cos + pltpu.roll(x, 64, 1) * sin_signed


def _rope64(x, cos, sin_signed, first_half):
    partner = jnp.where(first_half, pltpu.roll(x, 96, 1), pltpu.roll(x, 32, 1))
    return x * cos + partner * sin_signed


def _proj_att_kernel(x_ref, g_ref, w_ref, ca_ref, sa_ref, ci_ref, si_ref,
                     q_ref, kt_ref, v_ref, qi_ref, kit_ref, wi_ref):
    h = _rms(x_ref[...], g_ref[...]).astype(BF16)
    p = _dot(h, w_ref[...])
    ca, sa = ca_ref[...], sa_ref[...]
    ci, si = ci_ref[...], si_ref[...]
    q_scale = ATT_HEAD_DIM ** -0.5 * LOG2E
    for hd in range(ATT_HEADS):
        sl = slice(hd * 128, (hd + 1) * 128)
        q_ref[:, sl] = (_rope128(p[:, sl], ca, sa) * q_scale).astype(BF16)
    for g in range(ATT_KV_HEADS):
        k_g = _rope128(p[:, 1024 + g * 128:1024 + (g + 1) * 128], ca, sa)
        kt_ref[g * 128:(g + 1) * 128, :] = jnp.transpose(k_g).astype(BF16)
    v_ref[...] = p[:, 1280:1536].astype(BF16)
    lane = lax.broadcasted_iota(I32, ci.shape, 1)
    first_half = (lane & 63) < 32
    idx_scale = IDX_DIM ** -0.5
    for hd in range(IDX_HEADS):
        off = 1536 + hd * 128
        qi_ref[:, hd * 128:(hd + 1) * 128] = (
            _rope64(p[:, off:off + 128], ci, si, first_half) * idx_scale).astype(BF16)
    kit_ref[...] = jnp.transpose(_rope64(p[:, 2048:2176], ci, si, first_half)).astype(BF16)
    wi_ref[...] = p[:, 2176:2304] * (IDX_HEADS ** -0.5)


def _proj_att(x, g, w, ca, sa, ci, si, tm):
    b, s, d = x.shape
    row = lambda c: pl.BlockSpec((None, tm, c), lambda bi, i: (bi, i, 0))
    col = lambda r: pl.BlockSpec((None, r, tm), lambda bi, i: (bi, 0, i))
    sds = jax.ShapeDtypeStruct
    return pl.pallas_call(
        _proj_att_kernel,
        grid=(b, s // tm),
        in_specs=[row(d), _const_spec((1, d)), _const_spec(w.shape),
                  row(128), row(128), row(128), row(128)],
        out_specs=[row(1024), col(256), row(256), row(512), col(128), row(128)],
        out_shape=[sds((b, s, 1024), BF16), sds((b, 256, s), BF16), sds((b, s, 256), BF16),
                   sds((b, s, 512), BF16), sds((b, 128, s), BF16), sds((b, s, 128), F32)],
        compiler_params=_cparams(("parallel", "parallel")),
        name="proj_att",
    )(x, g, w, ca, sa, ci, si)


def _proj_gdn_kernel(x_ref, g_ref, w_ref, qkv_ref, ab_ref, abt_ref):
    h = _rms(x_ref[...], g_ref[...]).astype(BF16)
    p = _dot(h, w_ref[...])
    qkv_ref[...] = p[:, :3072]
    ab = p[:, 3072:3200]
    ab_ref[...] = ab
    abt_ref[...] = jnp.transpose(ab)[:16, :]


def _proj_gdn(x, g, w, tm):
    b, s, d = x.shape
    row = lambda c: pl.BlockSpec((None, tm, c), lambda bi, i: (bi, i, 0))
    sds = jax.ShapeDtypeStruct
    return pl.pallas_call(
        _proj_gdn_kernel,
        grid=(b, s // tm),
        in_specs=[row(d), _const_spec((1, d)), _const_spec(w.shape)],
        out_specs=[row(3072), row(128), pl.BlockSpec((None, 16, tm), lambda bi, i: (bi, 0, i))],
        out_shape=[sds((b, s, 3072), F32), sds((b, s, 128), F32), sds((b, 16, s), F32)],
        compiler_params=_cparams(("parallel", "parallel")),
        name="proj_gdn",
    )(x, g, w)


def _key_of(v):
    b = pltpu.bitcast(v, I32)
    return jnp.where(b < 0, b ^ 0x7FFFFFFF, b)


def _val_of(k):
    return pltpu.bitcast(jnp.where(k < 0, k ^ 0x7FFFFFFF, k), F32)


def _dsa_kernel(q_ref, qi_ref, wi_ref, kt_ref, v_ref, kit_ref, o_ref,
                keys_ref, qis_ref, qg_ref, bias_ref, s_ref, p_ref, acc_ref, m_ref, l_ref, run_ref, tri_ref,
                *, tq, kb, topk, seq):
    t0 = pl.program_id(1) * tq
    nkb = (t0 + tq + kb - 1) // kb
    n_lane_tiles = kb // LANES
    row_ids = t0 + lax.broadcasted_iota(I32, (tq, kb), 0)
    col_iota = lax.broadcasted_iota(I32, (tq, kb), 1)
    kf = float(topk)

    for h in range(IDX_HEADS):
        qis_ref[h * tq:(h + 1) * tq, :] = qi_ref[:, h * 128:(h + 1) * 128]
    wi = wi_ref[...]
    w_heads = [wi[:, h:h + 1] for h in range(IDX_HEADS)]

    def score_block(j, carry, masked):
        mx, mn = carry
        c0 = pl.multiple_of(j * kb, kb)
        halves = []
        for a0 in range(0, kb, 256):
            lg = _dot(qis_ref[...], kit_ref[:, pl.ds(c0 + a0, 256)])
            part = jnp.zeros((tq, 256), F32)
            for h in range(IDX_HEADS):
                part = part + w_heads[h] * jnp.maximum(lg[h * tq:(h + 1) * tq], 0.0)
            halves.append(part)
        sc = jnp.concatenate(halves, axis=1)
        sc = jnp.where(sc == 0.0, 0.0, sc)
        for i in range(n_lane_tiles):
            mx = jnp.maximum(mx, sc[:, i * LANES:(i + 1) * LANES])
            mn = jnp.minimum(mn, sc[:, i * LANES:(i + 1) * LANES])
        if masked:
            sc = jnp.where(c0 + col_iota <= row_ids, sc, -jnp.inf)
        keys_ref[:, pl.ds(c0, kb)] = _key_of(sc)
        return mx, mn

    n_full = t0 // kb
    ext0 = (jnp.full((tq, LANES), -jnp.inf, F32), jnp.full((tq, LANES), jnp.inf, F32))
    ext1 = lax.fori_loop(0, n_full, functools.partial(score_block, masked=False), ext0)
    mx, mn = lax.fori_loop(n_full, nkb, functools.partial(score_block, masked=True), ext1)
    row_max = jnp.max(mx, axis=1, keepdims=True)
    row_min = jnp.min(mn, axis=1, keepdims=True)

    n_row_tiles = tq // LANES if tq % LANES == 0 else 1
    rt = tq // n_row_tiles
    ones8 = jnp.ones((8, LANES), BF16)

    def as_col(row):
        return jnp.transpose(row)[:, :1]

    def count_ge(cands):
        nc = len(cands)
        tiles = range(n_row_tiles)
        cb = [[c if isinstance(c, int) else jnp.broadcast_to(c[t * rt:(t + 1) * rt], (rt, LANES))
               for t in tiles] for c in cands]

        def tile_counts(t):
            def body(j, accs):
                c0 = pl.multiple_of(j * kb, kb)
                accs = list(accs)
                blk = keys_ref[t * rt:(t + 1) * rt, pl.ds(c0, kb)]
                for i in range(n_lane_tiles):
                    for k in range(nc):
                        accs[k] = accs[k] + jnp.where(blk[:, i * LANES:(i + 1) * LANES] >= cb[k][t], 1.0, 0.0)
                return tuple(accs)

            z = jnp.zeros((rt, LANES), F32)
            return lax.fori_loop(0, nkb, body, (z,) * nc)

        accs = [tile_counts(t) for t in tiles]
        return [jnp.concatenate([_dot_nt(ones8, accs[t][k].astype(BF16)) for t in tiles], axis=1)
                for k in range(nc)]

    as_row = lambda col: jnp.transpose(jnp.broadcast_to(col, (tq, 8)))
    n_visible = (t0 + lax.broadcasted_iota(I32, (8, tq), 1) + 1).astype(F32)
    few = n_visible < kf
    f_pos, f_zero = count_ge([1, 0])
    above = f_pos >= kf
    below = f_zero < kf
    at_zero = jnp.logical_and(jnp.logical_not(above), jnp.logical_not(below))
    key_min = _key_of(as_row(row_min))
    key_max = _key_of(as_row(row_max))
    lo0 = jnp.where(few, NEG_INF_KEY + 1, jnp.where(above, 1, jnp.where(below, key_min, 0)))
    flo0 = jnp.where(few, n_visible, jnp.where(above, f_pos, jnp.where(below, n_visible, f_zero)))
    hi0 = jnp.where(above, jnp.minimum(key_max, 2147483646) + 1, jnp.where(below, 0, 1))
    fhi0 = jnp.where(above, 0.0, jnp.where(below, f_zero, f_pos))
    log_target = float(np.log(topk + 0.5))
    glog = lambda c: jnp.log(jnp.maximum(c, 0.5)) - log_target
    done0 = jnp.where(jnp.logical_or(jnp.logical_or(few, at_zero), flo0 == kf), 1.0, 0.0)
    n_interp = 24

    def peel_min(base, flip):
        tiles = range(n_row_tiles)
        bb = [jnp.broadcast_to(base[t * rt:(t + 1) * rt], (rt, LANES)) for t in tiles]
        fb = [jnp.broadcast_to(flip[t * rt:(t + 1) * rt], (rt, LANES)) for t in tiles]

        def tile_min(t):
            def body(j, acc):
                c0 = pl.multiple_of(j * kb, kb)
                blk = keys_ref[t * rt:(t + 1) * rt, pl.ds(c0, kb)]
                for i in range(n_lane_tiles):
                    x = blk[:, i * LANES:(i + 1) * LANES] ^ fb[t]
                    acc = jnp.minimum(acc, jnp.where(x >= bb[t], x, INT_MAX))
                return acc

            acc = lax.fori_loop(0, nkb, body, jnp.full((rt, LANES), INT_MAX, I32))
            return jnp.min(acc, axis=1, keepdims=True)

        col = jnp.concatenate([tile_min(t) for t in tiles], axis=0)
        return jnp.transpose(jnp.broadcast_to(col, (tq, 8)))

    def probe(st, forced=None, use_forced=None):
        it, _, _, lo, hi, flo, fhi, glo, ghi, side, done = st
        v_lo, v_hi = _val_of(lo), _val_of(hi)
        v_model = (v_lo * ghi - v_hi * glo) / (ghi - glo)
        inside = flo - fhi
        v_even = v_lo + (v_hi - v_lo) * ((flo - kf + 0.5) / (inside + 1.0))
        c_interp = _key_of(jnp.where(inside <= 64.0, v_even, v_model))
        c_mid = jnp.right_shift(lo, 1) + jnp.right_shift(hi, 1) + (lo & hi & 1)
        cand = jnp.where(it >= n_interp, c_mid, c_interp)
        if forced is not None:
            cand = jnp.where(use_forced, forced, cand)
        cand = jnp.minimum(jnp.maximum(cand, lo + 1), hi - 1)
        c = count_ge([as_col(cand)])[0]
        active = done < 0.5
        to_lo = jnp.logical_and(active, c >= kf)
        to_hi = jnp.logical_and(active, c < kf)
        gc = glog(c)
        ghi_n = jnp.where(to_hi, gc, jnp.where(jnp.logical_and(to_lo, side > 0.0), ghi * 0.5, ghi))
        glo_n = jnp.where(to_lo, gc, jnp.where(jnp.logical_and(to_hi, side < 0.0), glo * 0.5, glo))
        side = jnp.where(to_lo, 1.0, jnp.where(to_hi, -1.0, side))
        lo = jnp.where(to_lo, cand, lo)
        flo = jnp.where(to_lo, c, flo)
        hi = jnp.where(to_hi, cand, hi)
        fhi = jnp.where(to_hi, c, fhi)
        fin = jnp.logical_or(flo == kf, hi - 1 == lo)
        done = jnp.where(fin, 1.0, done)
        near = jnp.logical_or(flo - kf <= 1.0, kf - fhi <= 1.0)
        n_far = jnp.sum(jnp.where(near, 0.0, 1.0 - done))
        return it + 1, n_far, jnp.sum(1.0 - done), lo, hi, flo, fhi, glo_n, ghi_n, side, done

    n_pass_cap = n_interp + 34
    near0 = jnp.logical_or(flo0 - kf <= 1.0, kf - fhi0 <= 1.0)
    st = (jnp.int32(0), jnp.sum(jnp.where(near0, 0.0, 1.0 - done0)), jnp.sum(1.0 - done0),
          lo0, hi0, flo0, fhi0, glog(flo0), glog(fhi0), jnp.zeros((8, tq), F32), done0)
    st = lax.while_loop(lambda st: jnp.logical_and(st[0] < n_pass_cap, st[1] > 0.0), probe, st)
    lo, hi, flo, fhi, done = st[3], st[4], st[5], st[6], st[10]
    active = done < 0.5
    drop_one = jnp.logical_and(active, flo - kf <= 1.0)
    add_one = jnp.logical_and(jnp.logical_and(active, jnp.logical_not(drop_one)), kf - fhi <= 1.0)
    flip = jnp.where(add_one, -1, 0)
    edge = peel_min(as_col(jnp.where(add_one, -hi, lo)), as_col(flip))
    forced = jnp.where(add_one, edge ^ flip, edge + 1)
    st = probe(st, forced, jnp.logical_or(drop_one, add_one))
    st = lax.while_loop(lambda st: jnp.logical_and(st[0] < n_pass_cap + 2, st[2] > 0.0), probe, st)
    thr = as_col(st[3])
    tie_rows = jnp.logical_and(st[5] > kf, jnp.logical_not(few))
    need = as_col(jnp.where(tie_rows, kf - st[6], float(2 * seq)))

    run_ref[...] = jnp.zeros(run_ref.shape, F32)
    rr = lax.broadcasted_iota(I32, (kb, kb), 0)
    cc = lax.broadcasted_iota(I32, (kb, kb), 1)
    tri_ref[...] = jnp.where(rr <= cc, 1.0, 0.0).astype(BF16)

    rep = ATT_HEADS // ATT_KV_HEADS
    rows = rep * tq
    rc = 32
    for g in range(ATT_KV_HEADS):
        for r in range(rep):
            hd = g * rep + r
            qg_ref[g, r * tq:(r + 1) * tq, :] = q_ref[:, hd * 128:(hd + 1) * 128]
    acc_ref[...] = jnp.zeros(acc_ref.shape, F32)
    l_ref[...] = jnp.zeros(l_ref.shape, F32)
    m_ref[...] = jnp.full(m_ref.shape, NEG_BIG, F32)

    def attend_block(j, carry):
        c0 = pl.multiple_of(j * kb, kb)
        keys = keys_ref[:, pl.ds(c0, kb)]
        tied = keys == thr
        prefix = _dot(jnp.where(tied, 1.0, 0.0).astype(BF16), tri_ref[...])
        tied_bias = jnp.where(tied, 0.0, NEG_BIG)
        left = need - run_ref[...]
        bias_ref[...] = jnp.where(keys > thr, 0.0, jnp.where(prefix <= left, tied_bias, NEG_BIG))
        run_ref[...] = run_ref[...] + prefix[:, kb - 1:kb]
        for g in range(ATT_KV_HEADS):
            for a0 in range(0, kb, 256):
                s_ref[g, :, a0:a0 + 256] = _dot(qg_ref[g], kt_ref[g * 128:(g + 1) * 128, pl.ds(c0 + a0, 256)])
        for g in range(ATT_KV_HEADS):
            for r0 in range(0, rows, rc):
                rs = slice(r0, r0 + rc)
                b0 = r0 % tq
                s = s_ref[g, rs, :] + bias_ref[b0:b0 + rc, :]
                m_prev = m_ref[g, rs, :]
                m_next = jnp.maximum(m_prev, jnp.max(s, axis=1, keepdims=True))
                p = jnp.exp2(s - jnp.concatenate([m_next] * n_lane_tiles, axis=1))
                alpha = jnp.exp2(m_prev - m_next)
                l_ref[g, rs, :] = alpha * l_ref[g, rs, :] + jnp.sum(p, axis=1, keepdims=True)
                m_ref[g, rs, :] = m_next
                acc_ref[g, rs, :] = acc_ref[g, rs, :] * alpha
                p_ref[g, rs, :] = p.astype(BF16)
            acc_ref[g] += _dot(p_ref[g], v_ref[pl.ds(c0, kb), g * 128:(g + 1) * 128])
        return carry

    lax.fori_loop(0, nkb, attend_block, 0)

    for g in range(ATT_KV_HEADS):
        out = acc_ref[g] / l_ref[g]
        for r in range(rep):
            hd = g * rep + r
            o_ref[:, hd * 128:(hd + 1) * 128] = out[r * tq:(r + 1) * tq]


def _dsa(q, qi, wi, kt, v, kit, topk, tq, kb):
    b, s, _ = q.shape
    assert s <= 256 * LANES, "per-lane partial counts must stay exact in bf16"
    rep = ATT_HEADS // ATT_KV_HEADS
    qspec = lambda c: pl.BlockSpec((None, tq, c), lambda bi, i: (bi, i, 0))
    per_batch = lambda r, c: pl.BlockSpec((None, r, c), lambda bi, i: (bi, 0, 0),
                                          pipeline_mode=pl.Buffered(1))
    return pl.pallas_call(
        functools.partial(_dsa_kernel, tq=tq, kb=kb, topk=topk, seq=s),
        grid=(b, s // tq),
        in_specs=[qspec(1024), qspec(512), qspec(128),
                  per_batch(256, s), per_batch(s, 256), per_batch(128, s)],
        out_specs=qspec(1024),
        out_shape=jax.ShapeDtypeStruct((b, s, 1024), F32),
        scratch_shapes=[pltpu.VMEM((tq, s), I32),
                        pltpu.VMEM((IDX_HEADS * tq, 128), BF16),
                        pltpu.VMEM((ATT_KV_HEADS, rep * tq, 128), BF16),
                        pltpu.VMEM((tq, kb), F32),
                        pltpu.VMEM((ATT_KV_HEADS, rep * tq, kb), F32),
                        pltpu.VMEM((ATT_KV_HEADS, rep * tq, kb), BF16),
                        pltpu.VMEM((ATT_KV_HEADS, rep * tq, 128), F32),
                        pltpu.VMEM((ATT_KV_HEADS, rep * tq, 128), F32),
                        pltpu.VMEM((ATT_KV_HEADS, rep * tq, 128), F32),
                        pltpu.VMEM((tq, 1), F32),
                        pltpu.VMEM((kb, kb), BF16)],
        compiler_params=_cparams(("parallel", "arbitrary")),
        name="dsa_attention",
    )(q, qi, wi, kt, v, kit)


def _softplus(x):
    return jnp.maximum(x, 0.0) + jnp.log(1.0 + jnp.exp(-jnp.abs(x)))


def _gdn_kernel(qkv_ref, ab_ref, abt_ref, cw_ref, ac_ref, dc_ref, ar_ref, dr_ref, gn_ref,
                y_ref, xbuf_ref, act_ref, state_ref, *, tb):
    n_chunks = tb // CHUNK

    @pl.when(pl.program_id(1) == 0)
    def _():
        xbuf_ref[0:8, :] = jnp.zeros((8, xbuf_ref.shape[1]), F32)
        state_ref[...] = jnp.zeros(state_ref.shape, F32)

    xbuf_ref[8:8 + tb, :] = qkv_ref[...]
    for c in range(xbuf_ref.shape[1] // LANES):
        sl = slice(c * LANES, (c + 1) * LANES)
        y = xbuf_ref[5:5 + tb, sl] * cw_ref[0:1, sl]
        for j in range(1, CONV_K):
            y = y + xbuf_ref[5 + j:5 + j + tb, sl] * cw_ref[j:j + 1, sl]
        act_ref[:, sl] = y * (1.0 / (1.0 + jnp.exp(-y)))
    xbuf_ref[0:8, :] = xbuf_ref[tb:tb + 8, :]

    ab = ab_ref[...]
    g_col = -jnp.exp(ac_ref[...]) * _softplus(ab + dc_ref[...])
    beta_col = 1.0 / (1.0 + jnp.exp(-ab))
    g_row = -jnp.exp(ar_ref[...]) * _softplus(abt_ref[0:8, :] + dr_ref[...])

    r = lax.broadcasted_iota(I32, (tb, tb), 0)
    c = lax.broadcasted_iota(I32, (tb, tb), 1)
    chunk_shift = CHUNK.bit_length() - 1
    same = jnp.right_shift(r, chunk_shift) == jnp.right_shift(c, chunk_shift)
    lower = jnp.logical_and(same, r >= c)
    strict = jnp.logical_and(same, r > c)
    lower_m = jnp.where(lower, 1.0, 0.0).astype(BF16)
    upper_m = jnp.where(jnp.logical_and(same, r <= c), 1.0, 0.0).astype(BF16)
    same_m = jnp.where(same, 1.0, 0.0).astype(BF16)
    eye = jnp.where(r == c, 1.0, 0.0)
    n_levels = CHUNK.bit_length() - 1
    lvl_subs = []
    for lvl in range(n_levels):
        lvl_subs.append(jnp.logical_and(jnp.right_shift(r, lvl + 1) == jnp.right_shift(c, lvl + 1),
                                        jnp.logical_and((jnp.right_shift(r, lvl) & 1) == 1,
                                                        (jnp.right_shift(c, lvl) & 1) == 0)))
    lvl_masks = [jnp.where(sub, 1.0, 0.0).astype(BF16) for sub in lvl_subs]

    gparts = _split3(g_col)
    gc_col = _dot(lower_m, gparts[0]) + (_dot(lower_m, gparts[1]) + _dot(lower_m, gparts[2]))
    gl_col = _dot(same_m, gparts[0]) + (_dot(same_m, gparts[1]) + _dot(same_m, gparts[2]))
    rparts = _split3(g_row)
    gc_row = _dot(rparts[0], upper_m) + (_dot(rparts[1], upper_m) + _dot(rparts[2], upper_m))

    gn = gn_ref[...]
    hg = 4
    for h0 in range(0, GDN_HEADS, hg):
        heads = range(h0, h0 + hg)
        q_l, k_l, kbeta_l, vbeta_l, decay_l, gc_l, gl_l, m16_l, d_l = ([] for _ in range(9))
        for h in heads:
            q = act_ref[:, h * 128:(h + 1) * 128]
            k = act_ref[:, 1024 + h * 128:1024 + (h + 1) * 128]
            v = act_ref[:, 2048 + h * 128:2048 + (h + 1) * 128]
            q = q * lax.rsqrt(jnp.sum(q * q, axis=-1, keepdims=True) + EPS) * (GDN_DK ** -0.5)
            k = k * lax.rsqrt(jnp.sum(k * k, axis=-1, keepdims=True) + EPS)
            gc = gc_col[:, h:h + 1]
            beta = beta_col[:, 8 + h:9 + h]
            diff = gc - gc_row[h:h + 1, :]
            decay = jnp.where(lower, jnp.exp(jnp.where(lower, diff, 0.0)), 0.0)
            kbeta = k * beta
            nm = jnp.where(strict, -(_dot_nt(kbeta.astype(BF16), k.astype(BF16)) * decay), 0.0)
            q_l.append(q); k_l.append(k); kbeta_l.append(kbeta); vbeta_l.append(v * beta)
            decay_l.append(decay); gc_l.append(gc); gl_l.append(gl_col[:, h:h + 1])
            m16_l.append(nm.astype(BF16)); d_l.append(eye + jnp.where(lvl_subs[0], nm, 0.0))
        for lvl in range(1, n_levels):
            for i in range(hg):
                dh = d_l[i].astype(BF16)
                lh = m16_l[i] * lvl_masks[lvl]
                th = _dot(dh, lh).astype(BF16)
                d_l[i] = d_l[i] + _dot(th, dh)
        u_l, w_l, attn_l, qd_l, kd_l = [], [], [], [], []
        for i in range(hg):
            egc = jnp.exp(gc_l[i])
            rhs = jnp.concatenate([vbeta_l[i], kbeta_l[i] * egc], axis=1)
            sol = _dot(d_l[i].astype(BF16), rhs.astype(BF16))
            u_l.append(sol[:, :128])
            w_l.append(sol[:, 128:].astype(BF16))
            k16 = k_l[i].astype(BF16)
            attn_l.append(jnp.where(lower, _dot_nt(q_l[i].astype(BF16), k16) * decay_l[i], 0.0).astype(BF16))
            qd_l.append((q_l[i] * egc).astype(BF16))
            kd_l.append(k_l[i] * jnp.exp(gl_l[i] - gc_l[i]))
        st_l = [state_ref[h] for h in heads]
        outs = [[] for _ in range(hg)]
        for ci in range(n_chunks):
            rc = slice(ci * CHUNK, (ci + 1) * CHUNK)
            for i in range(hg):
                st16 = st_l[i].astype(BF16)
                v_new = u_l[i][rc] - _dot(w_l[i][rc], st16)
                v_new16 = v_new.astype(BF16)
                outs[i].append(_dot(qd_l[i][rc], st16) + _dot(attn_l[i][rc, rc], v_new16))
                kd_t = jnp.transpose(kd_l[i][rc]).astype(BF16)
                st_l[i] = (st_l[i] * jnp.exp(gl_l[i][ci * CHUNK:ci * CHUNK + 1])
                           + _dot(kd_t, v_new16))
        for i, h in enumerate(heads):
            state_ref[h] = st_l[i]
            y_ref[:, h * 128:(h + 1) * 128] = _rms(jnp.concatenate(outs[i], axis=0), gn)


def _gdn(qkv, ab, abt, conv_w, ac, dc, ar, dr, gn, tb):
    b, s, _ = qkv.shape
    blk = lambda c: pl.BlockSpec((None, tb, c), lambda bi, i: (bi, i, 0))
    return pl.pallas_call(
        functools.partial(_gdn_kernel, tb=tb),
        grid=(b, s // tb),
        in_specs=[blk(3072), blk(128), pl.BlockSpec((None, 16, tb), lambda bi, i: (bi, 0, i)),
                  _const_spec(conv_w.shape), _const_spec(ac.shape), _const_spec(dc.shape),
                  _const_spec(ar.shape), _const_spec(dr.shape), _const_spec(gn.shape)],
        out_specs=blk(1024),
        out_shape=jax.ShapeDtypeStruct((b, s, 1024), F32),
        scratch_shapes=[pltpu.VMEM((tb + 8, 3072), F32),
                        pltpu.VMEM((tb, 3072), F32),
                        pltpu.VMEM((GDN_HEADS, GDN_DK, GDN_DV), F32)],
        compiler_params=_cparams(("parallel", "arbitrary")),
        name="gated_delta",
    )(qkv, ab, abt, conv_w, ac, dc, ar, dr, gn)


def _sigmoid(x):
    return 1.0 / (1.0 + jnp.exp(-x))


def _merge_kernel(x_ref, ya_ref, yb_ref, gpre_ref, wg_ref, wo_ref, gpost_ref, o_ref):
    x = x_ref[...]
    h = _rms(x, gpre_ref[...]).astype(BF16)
    gates = _dot(h, wg_ref[...])
    y = _sigmoid(gates[:, :1024]) * ya_ref[...] + _sigmoid(gates[:, 1024:]) * yb_ref[...]
    z = _dot(y.astype(BF16), wo_ref[...])
    o_ref[...] = x + _rms(z, gpost_ref[...])


def _merge(x2, ya, yb, gpre, wg, wo, gpost, tm):
    n, d = x2.shape
    row = pl.BlockSpec((tm, d), lambda i: (i, 0))
    return pl.pallas_call(
        _merge_kernel,
        grid=(n // tm,),
        in_specs=[row, row, row, _const_spec((1, d)), _const_spec(wg.shape),
                  _const_spec(wo.shape), _const_spec((1, d))],
        out_specs=row,
        out_shape=jax.ShapeDtypeStruct((n, d), F32),
        compiler_params=_cparams(("parallel",)),
        name="merge_out",
    )(x2, ya, yb, gpre, wg, wo, gpost)


def _mem_kv_kernel(m_ref, g_ref, w_ref, kv_ref):
    h = _rms(m_ref[...], g_ref[...]).astype(BF16)
    kv_ref[...] = _dot(h, w_ref[...]).astype(BF16)


def _mem_kv(mem2, g, w):
    n, d = mem2.shape
    return pl.pallas_call(
        _mem_kv_kernel,
        out_shape=jax.ShapeDtypeStruct((n, w.shape[1]), BF16),
        compiler_params=pltpu.CompilerParams(vmem_limit_bytes=VMEM_LIMIT),
        name="mem_kv",
    )(mem2, g, w)


def _xattn_kernel(x_ref, kt_ref, v_ref, gpre_ref, wq_ref, wo_ref, gpost_ref, o_ref):
    x = x_ref[...]
    d = x.shape[1]
    hd = d // XATT_HEADS
    h = _rms(x, gpre_ref[...]).astype(BF16)
    q = (_dot(h, wq_ref[...]) * (hd ** -0.5)).astype(BF16)
    heads = []
    for i in range(XATT_HEADS):
        sl = slice(i * hd, (i + 1) * hd)
        s = _dot(q[:, sl], kt_ref[sl, :])
        p = jnp.exp(s - jnp.max(s, axis=1, keepdims=True))
        o = _dot(p.astype(BF16), v_ref[:, sl]) / jnp.sum(p, axis=1, keepdims=True)
        heads.append(o.astype(BF16))
    z = _dot(jnp.concatenate(heads, axis=1), wo_ref[...])
    o_ref[...] = x + _rms(z, gpost_ref[...])


def _xattn(x, kt, v, gpre, wq, wo, gpost, tm):
    b, s, d = x.shape
    n_mem = v.shape[1]
    row = pl.BlockSpec((None, tm, d), lambda bi, i: (bi, i, 0))
    return pl.pallas_call(
        _xattn_kernel,
        grid=(b, s // tm),
        in_specs=[row,
                  pl.BlockSpec((None, d, n_mem), lambda bi, i: (bi, 0, 0)),
                  pl.BlockSpec((None, n_mem, d), lambda bi, i: (bi, 0, 0)),
                  _const_spec((1, d)), _const_spec(wq.shape), _const_spec(wo.shape),
                  _const_spec((1, d))],
        out_specs=row,
        out_shape=jax.ShapeDtypeStruct((b, s, d), F32),
        compiler_params=_cparams(("parallel", "parallel")),
        name="mem_xattn",
    )(x, kt, v, gpre, wq, wo, gpost)


def _ffn_kernel(x_ref, gpre_ref, wgu_ref, wd_ref, gpost_ref, o_ref, *, ff, fc):
    x = x_ref[...]
    h = _rms(x, gpre_ref[...]).astype(BF16)
    z = jnp.zeros(x.shape, F32)
    for c0 in range(0, ff, fc):
        gate = _dot(h, wgu_ref[:, c0:c0 + fc])
        up = _dot(h, wgu_ref[:, ff + c0:ff + c0 + fc])
        act = (gate * _sigmoid(gate) * up).astype(BF16)
        z = z + _dot(act, wd_ref[c0:c0 + fc, :])
    o_ref[...] = x + _rms(z, gpost_ref[...])


def _ffn(x2, gpre, wgu, wd, gpost, tm):
    n, d = x2.shape
    ff = wd.shape[0]
    fc = ff // 2 if (ff // 2) % LANES == 0 else ff
    row = pl.BlockSpec((tm, d), lambda i: (i, 0))
    return pl.pallas_call(
        functools.partial(_ffn_kernel, ff=ff, fc=fc),
        grid=(n // tm,),
        in_specs=[row, _const_spec((1, d)), _const_spec(wgu.shape), _const_spec(wd.shape),
                  _const_spec((1, d))],
        out_specs=row,
        out_shape=jax.ShapeDtypeStruct((n, d), F32),
        compiler_params=_cparams(("parallel",)),
        name="swiglu",
    )(x2, gpre, wgu, wd, gpost)


def _rope_tables(positions):
    pos = positions.astype(F32).reshape(-1)[:, None]

    def tables(dim, reps):
        half = dim // 2
        inv_freq = ROPE_THETA ** (-jnp.arange(half, dtype=F32) * 2.0 / dim)
        ang = pos * inv_freq
        cos, sin = jnp.cos(ang), jnp.sin(ang)
        return (jnp.tile(jnp.concatenate([cos, cos], axis=1), (1, reps)),
                jnp.tile(jnp.concatenate([-sin, sin], axis=1), (1, reps)))

    return tables(ATT_HEAD_DIM, 1) + tables(IDX_DIM, 2)


def _pick_tile(n, pref):
    t = min(pref, n)
    while n % t:
        t //= 2
    return t


def kernel(x, mem, positions, norm_mix_pre, w_in, conv_w, a_log, dt_bias, gdn_norm, w_out,
           norm_mix_post, norm_x_pre, norm_mem, w_xq, w_xkv, w_xo, norm_x_post,
           norm_ffn_pre, w_gu, w_down, norm_ffn_post):
    b, s, d = x.shape
    n = b * s
    n_mem = mem.shape[1]
    depth = w_in.shape[0]
    topk = min(TOPK_MAX, s // 4)
    tm = _pick_tile(s, 512)
    tq = _pick_tile(s, 256)
    kb = _pick_tile(s, 512)
    tb = _pick_tile(s, 256)

    ca, sa, ci, si = (t.reshape(b, s, 128) for t in _rope_tables(positions))

    row1 = lambda v: v.reshape(1, -1).astype(F32)

    x2 = x.reshape(n, d)
    for l in range(depth):
        w_att, w_gdn, w_gate = _prep_w_in(w_in, l)

        g_pre = row1(norm_mix_pre[l])
        x3 = x2.reshape(b, s, d)
        q, kt, v, qi, kit, wi = _proj_att(x3, g_pre, w_att, ca, sa, ci, si, tm)
        ya = _dsa(q, qi, wi, kt, v, kit, topk, tq, kb)

        qkv, ab, abt = _proj_gdn(x3, g_pre, w_gdn, tm)
        pad_lanes = lambda vec: jnp.pad(vec.astype(F32), (0, 128 - GDN_HEADS)).reshape(1, 128)
        bcast_rows = lambda vec: jnp.broadcast_to(vec.astype(F32)[:, None], (GDN_HEADS, tb))
        yb = _gdn(qkv, ab, abt, conv_w[l].astype(F32),
                  pad_lanes(a_log[l]), pad_lanes(dt_bias[l]), bcast_rows(a_log[l]), bcast_rows(dt_bias[l]),
                  row1(gdn_norm[l]), tb)

        x2 = _merge(x2, ya.reshape(n, 1024), yb.reshape(n, 1024), g_pre, w_gate,
                    w_out[l].astype(BF16), row1(norm_mix_post[l]), tm)

        kv = _mem_kv(mem.reshape(b * n_mem, d), row1(norm_mem[l]), w_xkv[l].astype(BF16))
        kv = kv.reshape(b, n_mem, 2 * d)
        kt_mem = jnp.swapaxes(kv[:, :, :d], 1, 2)
        x2 = _xattn(x2.reshape(b, s, d), kt_mem, kv[:, :, d:], row1(norm_x_pre[l]),
                    w_xq[l].astype(BF16), w_xo[l].astype(BF16), row1(norm_x_post[l]), tm).reshape(n, d)

        x2 = _ffn(x2, row1(norm_ffn_pre[l]), w_gu[l].astype(BF16), w_down[l].astype(BF16),
                  row1(norm_ffn_post[l]), tm)
    return x2.reshape(b, s, d)
```

```python
import functools

import jax
import jax.numpy as jnp
import numpy as np
from jax import lax
from jax.experimental import pallas as pl
from jax.experimental.pallas import tpu as pltpu

EPS = 1e-6
ROPE_THETA = 10000.0
ATT_HEADS = 8
ATT_KV_HEADS = 2
ATT_HEAD_DIM = 128
IDX_HEADS = 4
IDX_DIM = 64
TOPK_MAX = 256
GDN_HEADS = 8
GDN_DK = 128
GDN_DV = 128
CONV_K = 4
CHUNK = 64
XATT_HEADS = 4

LANES = 128
VMEM_LIMIT = 56 * 1024 * 1024

F32 = jnp.float32
BF16 = jnp.bfloat16
I32 = jnp.int32

NEG_BIG = -1e30
LOG2E = 1.4426950408889634
INT_MIN = -2147483648
INT_MAX = 2147483647
NEG_INF_KEY = -2139095041


def _cparams(sem):
    return pltpu.CompilerParams(dimension_semantics=sem, vmem_limit_bytes=VMEM_LIMIT)


def _const_spec(shape):
    nd = len(shape)
    return pl.BlockSpec(shape, lambda *_: (0,) * nd, pipeline_mode=pl.Buffered(1))


def _rms(x, g):
    return x * lax.rsqrt(jnp.mean(x * x, axis=-1, keepdims=True) + EPS) * g


def _dot(a, b):
    return jnp.dot(a, b, preferred_element_type=F32)


def _dot_nt(a, b):
    return lax.dot_general(a, b, (((1,), (1,)), ((), ())), preferred_element_type=F32)


def _split3(a):
    a1 = a.astype(BF16)
    r1 = a - a1.astype(F32)
    a2 = r1.astype(BF16)
    a3 = (r1 - a2.astype(F32)).astype(BF16)
    return a1, a2, a3


A_Q, A_KV, I_Q, I_K, I_W, B_QKV = 1024, 256, 256, 64, 4, 3072
O_IQ = A_Q + 2 * A_KV
O_IK = O_IQ + I_Q
O_IW = O_IK + I_K
O_B = O_IW + I_W
O_BA = O_B + B_QKV
O_GA = O_BA + 2 * GDN_HEADS
IN_COLS = O_GA + 2 * 1024


def _prep_w_in_kernel(w_ref, att_ref, gdn_ref, gate_ref):
    w = w_ref[...]
    zeros = lambda c: jnp.zeros((w.shape[0], c), F32)
    att = [w[:, :O_IQ]]
    for h in range(IDX_HEADS):
        att += [w[:, O_IQ + h * IDX_DIM:O_IQ + (h + 1) * IDX_DIM], zeros(128 - IDX_DIM)]
    att += [w[:, O_IK:O_IW], zeros(128 - I_K), w[:, O_IW:O_B], zeros(128 - I_W)]
    att_ref[...] = jnp.concatenate(att, axis=1).astype(BF16)
    gdn_ref[...] = jnp.concatenate([w[:, O_B:O_GA], zeros(128 - 2 * GDN_HEADS)], axis=1).astype(BF16)
    gate_ref[...] = w[:, O_GA:].astype(BF16)


def _prep_w_in(w_in, layer):
    _, d, cols = w_in.shape
    assert cols == IN_COLS
    tr = 128
    widths = (O_IQ + IDX_HEADS * 128 + 256, B_QKV + 128, 2048)
    return pl.pallas_call(
        _prep_w_in_kernel,
        grid=(d // tr,),
        in_specs=[pl.BlockSpec((None, tr, cols), lambda i: (layer, i, 0))],
        out_specs=[pl.BlockSpec((tr, c), lambda i: (i, 0)) for c in widths],
        out_shape=[jax.ShapeDtypeStruct((d, c), BF16) for c in widths],
        compiler_params=_cparams(("parallel",)),
        name="prep_w_in",
    )(w_in)


def _rope128(x, cos, sin_signed):
    return x * cos + pltpu.roll(x, 64, 1) * sin_signed


def _rope64(x, cos, sin_signed, first_half):
    partner = jnp.where(first_half, pltpu.roll(x, 96, 1), pltpu.roll(x, 32, 1))
    return x * cos + partner * sin_signed


def _proj_att_kernel(x_ref, g_ref, w_ref, ca_ref, sa_ref, ci_ref, si_ref,
                     q_ref, kt_ref, v_ref, qi_ref, kit_ref, wi_ref):
    h = _rms(x_ref[...], g_ref[...]).astype(BF16)
    p = _dot(h, w_ref[...])
    ca, sa = ca_ref[...], sa_ref[...]
    ci, si = ci_ref[...], si_ref[...]
    q_scale = ATT_HEAD_DIM ** -0.5 * LOG2E
    for hd in range(ATT_HEADS):
        sl = slice(hd * 128, (hd + 1) * 128)
        q_ref[:, sl] = (_rope128(p[:, sl], ca, sa) * q_scale).astype(BF16)
    for g in range(ATT_KV_HEADS):
        k_g = _rope128(p[:, 1024 + g * 128:1024 + (g + 1) * 128], ca, sa)
        kt_ref[g * 128:(g + 1) * 128, :] = jnp.transpose(k_g).astype(BF16)
    v_ref[...] = p[:, 1280:1536].astype(BF16)
    lane = lax.broadcasted_iota(I32, ci.shape, 1)
    first_half = (lane & 63) < 32
    idx_scale = IDX_DIM ** -0.5
    for hd in range(IDX_HEADS):
        off = 1536 + hd * 128
        qi_ref[:, hd * 128:(hd + 1) * 128] = (
            _rope64(p[:, off:off + 128], ci, si, first_half) * idx_scale).astype(BF16)
    kit_ref[...] = jnp.transpose(_rope64(p[:, 2048:2176], ci, si, first_half)).astype(BF16)
    wi_ref[...] = p[:, 2176:2304] * (IDX_HEADS ** -0.5)


def _proj_att(x, g, w, ca, sa, ci, si, tm):
    b, s, d = x.shape
    row = lambda c: pl.BlockSpec((None, tm, c), lambda bi, i: (bi, i, 0))
    col = lambda r: pl.BlockSpec((None, r, tm), lambda bi, i: (bi, 0, i))
    sds = jax.ShapeDtypeStruct
    return pl.pallas_call(
        _proj_att_kernel,
        grid=(b, s // tm),
        in_specs=[row(d), _const_spec((1, d)), _const_spec(w.shape),
                  row(128), row(128), row(128), row(128)],
        out_specs=[row(1024), col(256), row(256), row(512), col(128), row(128)],
        out_shape=[sds((b, s, 1024), BF16), sds((b, 256, s), BF16), sds((b, s, 256), BF16),
                   sds((b, s, 512), BF16), sds((b, 128, s), BF16), sds((b, s, 128), F32)],
        compiler_params=_cparams(("parallel", "parallel")),
        name="proj_att",
    )(x, g, w, ca, sa, ci, si)


def _proj_gdn_kernel(x_ref, g_ref, w_ref, qkv_ref, ab_ref, abt_ref):
    h = _rms(x_ref[...], g_ref[...]).astype(BF16)
    p = _dot(h, w_ref[...])
    qkv_ref[...] = p[:, :3072]
    ab = p[:, 3072:3200]
    ab_ref[...] = ab
    abt_ref[...] = jnp.transpose(ab)[:16, :]


def _proj_gdn(x, g, w, tm):
    b, s, d = x.shape
    row = lambda c: pl.BlockSpec((None, tm, c), lambda bi, i: (bi, i, 0))
    sds = jax.ShapeDtypeStruct
    return pl.pallas_call(
        _proj_gdn_kernel,
        grid=(b, s // tm),
        in_specs=[row(d), _const_spec((1, d)), _const_spec(w.shape)],
        out_specs=[row(3072), row(128), pl.BlockSpec((None, 16, tm), lambda bi, i: (bi, 0, i))],
        out_shape=[sds((b, s, 3072), F32), sds((b, s, 128), F32), sds((b, 16, s), F32)],
        compiler_params=_cparams(("parallel", "parallel")),
        name="proj_gdn",
    )(x, g, w)


def _key_of(v):
    b = pltpu.bitcast(v, I32)
    return jnp.where(b < 0, b ^ 0x7FFFFFFF, b)


def _val_of(k):
    return pltpu.bitcast(jnp.where(k < 0, k ^ 0x7FFFFFFF, k), F32)


def _dsa_kernel(q_ref, qi_ref, wi_ref, kt_ref, v_ref, kit_ref, o_ref,
                keys_ref, qis_ref, qg_ref, bias_ref, s_ref, p_ref, acc_ref, m_ref, l_ref, run_ref, tri_ref,
                *, tq, kb, topk, seq):
    t0 = pl.program_id(1) * tq
    nkb = (t0 + tq + kb - 1) // kb
    n_lane_tiles = kb // LANES
    row_ids = t0 + lax.broadcasted_iota(I32, (tq, kb), 0)
    col_iota = lax.broadcasted_iota(I32, (tq, kb), 1)
    kf = float(topk)

    for h in range(IDX_HEADS):
        qis_ref[h * tq:(h + 1) * tq, :] = qi_ref[:, h * 128:(h + 1) * 128]
    wi = wi_ref[...]
    w_heads = [wi[:, h:h + 1] for h in range(IDX_HEADS)]

    def score_block(j, carry, masked):
        mx, mn = carry
        c0 = pl.multiple_of(j * kb, kb)
        halves = []
        for a0 in range(0, kb, 256):
            lg = _dot(qis_ref[...], kit_ref[:, pl.ds(c0 + a0, 256)])
            part = jnp.zeros((tq, 256), F32)
            for h in range(IDX_HEADS):
                part = part + w_heads[h] * jnp.maximum(lg[h * tq:(h + 1) * tq], 0.0)
            halves.append(part)
        sc = jnp.concatenate(halves, axis=1)
        sc = jnp.where(sc == 0.0, 0.0, sc)
        for i in range(n_lane_tiles):
            mx = jnp.maximum(mx, sc[:, i * LANES:(i + 1) * LANES])
            mn = jnp.minimum(mn, sc[:, i * LANES:(i + 1) * LANES])
        if masked:
            sc = jnp.where(c0 + col_iota <= row_ids, sc, -jnp.inf)
        keys_ref[:, pl.ds(c0, kb)] = _key_of(sc)
        return mx, mn

    n_full = t0 // kb
    ext0 = (jnp.full((tq, LANES), -jnp.inf, F32), jnp.full((tq, LANES), jnp.inf, F32))
    ext1 = lax.fori_loop(0, n_full, functools.partial(score_block, masked=False), ext0)
    mx, mn = lax.fori_loop(n_full, nkb, functools.partial(score_block, masked=True), ext1)
    row_max = jnp.max(mx, axis=1, keepdims=True)
    row_min = jnp.min(mn, axis=1, keepdims=True)

    n_row_tiles = tq // LANES if tq % LANES == 0 else 1
    rt = tq // n_row_tiles
    ones8 = jnp.ones((8, LANES), BF16)

    def as_col(row):
        return jnp.transpose(row)[:, :1]

    def count_ge(cands):
        nc = len(cands)
        tiles = range(n_row_tiles)
        cb = [[c if isinstance(c, int) else jnp.broadcast_to(c[t * rt:(t + 1) * rt], (rt, LANES))
               for t in tiles] for c in cands]

        def tile_counts(t):
            def body(j, accs):
                c0 = pl.multiple_of(j * kb, kb)
                accs = list(accs)
                blk = keys_ref[t * rt:(t + 1) * rt, pl.ds(c0, kb)]
                for i in range(n_lane_tiles):
                    for k in range(nc):
                        accs[k] = accs[k] + jnp.where(blk[:, i * LANES:(i + 1) * LANES] >= cb[k][t], 1.0, 0.0)
                return tuple(accs)

            z = jnp.zeros((rt, LANES), F32)
            return lax.fori_loop(0, nkb, body, (z,) * nc)

        accs = [tile_counts(t) for t in tiles]
        return [jnp.concatenate([_dot_nt(ones8, accs[t][k].astype(BF16)) for t in tiles], axis=1)
                for k in range(nc)]

    as_row = lambda col: jnp.transpose(jnp.broadcast_to(col, (tq, 8)))
    n_visible = (t0 + lax.broadcasted_iota(I32, (8, tq), 1) + 1).astype(F32)
    few = n_visible < kf
    f_pos, f_zero = count_ge([1, 0])
    above = f_pos >= kf
    below = f_zero < kf
    at_zero = jnp.logical_and(jnp.logical_not(above), jnp.logical_not(below))
    key_min = _key_of(as_row(row_min))
    key_max = _key_of(as_row(row_max))
    lo0 = jnp.where(few, NEG_INF_KEY + 1, jnp.where(above, 1, jnp.where(below, key_min, 0)))
    flo0 = jnp.where(few, n_visible, jnp.where(above, f_pos, jnp.where(below, n_visible, f_zero)))
    hi0 = jnp.where(above, jnp.minimum(key_max, 2147483646) + 1, jnp.where(below, 0, 1))
    fhi0 = jnp.where(above, 0.0, jnp.where(below, f_zero, f_pos))
    log_target = float(np.log(topk + 0.5))
    glog = lambda c: jnp.log(jnp.maximum(c, 0.5)) - log_target
    done0 = jnp.where(jnp.logical_or(jnp.logical_or(few, at_zero), flo0 == kf), 1.0, 0.0)
    n_interp = 24

    def peel_min(base, flip):
        tiles = range(n_row_tiles)
        bb = [jnp.broadcast_to(base[t * rt:(t + 1) * rt], (rt, LANES)) for t in tiles]
        fb = [jnp.broadcast_to(flip[t * rt:(t + 1) * rt], (rt, LANES)) for t in tiles]

        def tile_min(t):
            def body(j, acc):
                c0 = pl.multiple_of(j * kb, kb)
                blk = keys_ref[t * rt:(t + 1) * rt, pl.ds(c0, kb)]
                for i in range(n_lane_tiles):
                    x = blk[:, i * LANES:(i + 1) * LANES] ^ fb[t]
                    acc = jnp.minimum(acc, jnp.where(x >= bb[t], x, INT_MAX))
                return acc

            acc = lax.fori_loop(0, nkb, body, jnp.full((rt, LANES), INT_MAX, I32))
            return jnp.min(acc, axis=1, keepdims=True)

        col = jnp.concatenate([tile_min(t) for t in tiles], axis=0)
        return jnp.transpose(jnp.broadcast_to(col, (tq, 8)))

    def probe(st, forced=None, use_forced=None):
        it, _, _, lo, hi, flo, fhi, glo, ghi, side, done = st
        v_lo, v_hi = _val_of(lo), _val_of(hi)
        v_model = (v_lo * ghi - v_hi * glo) / (ghi - glo)
        inside = flo - fhi
        v_even = v_lo + (v_hi - v_lo) * ((flo - kf + 0.5) / (inside + 1.0))
        c_interp = _key_of(jnp.where(inside <= 64.0, v_even, v_model))
        c_mid = jnp.right_shift(lo, 1) + jnp.right_shift(hi, 1) + (lo & hi & 1)
        cand = jnp.where(it >= n_interp, c_mid, c_interp)
        if forced is not None:
            cand = jnp.where(use_forced, forced, cand)
        cand = jnp.minimum(jnp.maximum(cand, lo + 1), hi - 1)
        c = count_ge([as_col(cand)])[0]
        active = done < 0.5
        to_lo = jnp.logical_and(active, c >= kf)
        to_hi = jnp.logical_and(active, c < kf)
        gc = glog(c)
        ghi_n = jnp.where(to_hi, gc, jnp.where(jnp.logical_and(to_lo, side > 0.0), ghi * 0.5, ghi))
        glo_n = jnp.where(to_lo, gc, jnp.where(jnp.logical_and(to_hi, side < 0.0), glo * 0.5, glo))
        side = jnp.where(to_lo, 1.0, jnp.where(to_hi, -1.0, side))
        lo = jnp.where(to_lo, cand, lo)
        flo = jnp.where(to_lo, c, flo)
        hi = jnp.where(to_hi, cand, hi)
        fhi = jnp.where(to_hi, c, fhi)
        fin = jnp.logical_or(flo == kf, hi - 1 == lo)
        done = jnp.where(fin, 1.0, done)
        near = jnp.logical_or(flo - kf <= 1.0, kf - fhi <= 1.0)
        n_far = jnp.sum(jnp.where(near, 0.0, 1.0 - done))
        return it + 1, n_far, jnp.sum(1.0 - done), lo, hi, flo, fhi, glo_n, ghi_n, side, done

    n_pass_cap = n_interp + 34
    near0 = jnp.logical_or(flo0 - kf <= 1.0, kf - fhi0 <= 1.0)
    st = (jnp.int32(0), jnp.sum(jnp.where(near0, 0.0, 1.0 - done0)), jnp.sum(1.0 - done0),
          lo0, hi0, flo0, fhi0, glog(flo0), glog(fhi0), jnp.zeros((8, tq), F32), done0)
    st = lax.while_loop(lambda st: jnp.logical_and(st[0] < n_pass_cap, st[1] > 0.0), probe, st)
    lo, hi, flo, fhi, done = st[3], st[4], st[5], st[6], st[10]
    active = done < 0.5
    drop_one = jnp.logical_and(active, flo - kf <= 1.0)
    add_one = jnp.logical_and(jnp.logical_and(active, jnp.logical_not(drop_one)), kf - fhi <= 1.0)
    flip = jnp.where(add_one, -1, 0)
    edge = peel_min(as_col(jnp.where(add_one, -hi, lo)), as_col(flip))
    forced = jnp.where(add_one, edge ^ flip, edge + 1)
    st = probe(st, forced, jnp.logical_or(drop_one, add_one))
    st = lax.while_loop(lambda st: jnp.logical_and(st[0] < n_pass_cap + 2, st[2] > 0.0), probe, st)
    thr = as_col(st[3])
    tie_rows = jnp.logical_and(st[5] > kf, jnp.logical_not(few))
    need = as_col(jnp.where(tie_rows, kf - st[6], float(2 * seq)))

    run_ref[...] = jnp.zeros(run_ref.shape, F32)
    rr = lax.broadcasted_iota(I32, (LANES, 2 * LANES), 0)
    cc = lax.broadcasted_iota(I32, (LANES, 2 * LANES), 1)
    tri_ref[...] = jnp.where(jnp.logical_or(rr <= cc, cc >= LANES), 1.0, 0.0).astype(BF16)

    rep = ATT_HEADS // ATT_KV_HEADS
    rows = rep * tq
    rc = 32
    for g in range(ATT_KV_HEADS):
        for r in range(rep):
            hd = g * rep + r
            qg_ref[g, r * tq:(r + 1) * tq, :] = q_ref[:, hd * 128:(hd + 1) * 128]
    acc_ref[...] = jnp.zeros(acc_ref.shape, F32)
    l_ref[...] = jnp.zeros(l_ref.shape, F32)
    m_ref[...] = jnp.full(m_ref.shape, NEG_BIG, F32)

    def attend_block(j, carry):
        c0 = pl.multiple_of(j * kb, kb)
        left = jnp.broadcast_to(need - run_ref[...], (tq, LANES))
        for i in range(n_lane_tiles):
            keys = keys_ref[:, pl.ds(c0 + i * LANES, LANES)]
            tied = keys == thr
            cnt = _dot(jnp.where(tied, 1.0, 0.0).astype(BF16), tri_ref[...])
            ok = jnp.logical_and(tied, cnt[:, :LANES] <= left)
            bias_ref[:, i * LANES:(i + 1) * LANES] = jnp.where(keys > thr, 0.0, jnp.where(ok, 0.0, NEG_BIG))
            left = left - cnt[:, LANES:]
        run_ref[...] = need - left[:, :1]
        for g in range(ATT_KV_HEADS):
            for a0 in range(0, kb, 256):
                s_ref[g, :, a0:a0 + 256] = _dot(qg_ref[g], kt_ref[g * 128:(g + 1) * 128, pl.ds(c0 + a0, 256)])
        for g in range(ATT_KV_HEADS):
            for r0 in range(0, rows, rc):
                rs = slice(r0, r0 + rc)
                b0 = r0 % tq
                s = s_ref[g, rs, :] + bias_ref[b0:b0 + rc, :]
                m_prev = m_ref[g, rs, :]
                m_next = jnp.maximum(m_prev, jnp.max(s, axis=1, keepdims=True))
                p = jnp.exp2(s - jnp.concatenate([m_next] * n_lane_tiles, axis=1))
                alpha = jnp.exp2(m_prev - m_next)
                l_ref[g, rs, :] = alpha * l_ref[g, rs, :] + jnp.sum(p, axis=1, keepdims=True)
                m_ref[g, rs, :] = m_next
                acc_ref[g, rs, :] = acc_ref[g, rs, :] * alpha
                p_ref[g, rs, :] = p.astype(BF16)
            acc_ref[g] += _dot(p_ref[g], v_ref[pl.ds(c0, kb), g * 128:(g + 1) * 128])
        return carry

    lax.fori_loop(0, nkb, attend_block, 0)

    for g in range(ATT_KV_HEADS):
        out = acc_ref[g] / l_ref[g]
        for r in range(rep):
            hd = g * rep + r
            o_ref[:, hd * 128:(hd + 1) * 128] = out[r * tq:(r + 1) * tq]


def _dsa(q, qi, wi, kt, v, kit, topk, tq, kb):
    b, s, _ = q.shape
    assert s <= 256 * LANES, "per-lane partial counts must stay exact in bf16"
    rep = ATT_HEADS // ATT_KV_HEADS
    qspec = lambda c: pl.BlockSpec((None, tq, c), lambda bi, i: (bi, i, 0))
    per_batch = lambda r, c: pl.BlockSpec((None, r, c), lambda bi, i: (bi, 0, 0),
                                          pipeline_mode=pl.Buffered(1))
    return pl.pallas_call(
        functools.partial(_dsa_kernel, tq=tq, kb=kb, topk=topk, seq=s),
        grid=(b, s // tq),
        in_specs=[qspec(1024), qspec(512), qspec(128),
                  per_batch(256, s), per_batch(s, 256), per_batch(128, s)],
        out_specs=qspec(1024),
        out_shape=jax.ShapeDtypeStruct((b, s, 1024), F32),
        scratch_shapes=[pltpu.VMEM((tq, s), I32),
                        pltpu.VMEM((IDX_HEADS * tq, 128), BF16),
                        pltpu.VMEM((ATT_KV_HEADS, rep * tq, 128), BF16),
                        pltpu.VMEM((tq, kb), F32),
                        pltpu.VMEM((ATT_KV_HEADS, rep * tq, kb), F32),
                        pltpu.VMEM((ATT_KV_HEADS, rep * tq, kb), BF16),
                        pltpu.VMEM((ATT_KV_HEADS, rep * tq, 128), F32),
                        pltpu.VMEM((ATT_KV_HEADS, rep * tq, 128), F32),
                        pltpu.VMEM((ATT_KV_HEADS, rep * tq, 128), F32),
                        pltpu.VMEM((tq, 1), F32),
                        pltpu.VMEM((LANES, 2 * LANES), BF16)],
        compiler_params=_cparams(("parallel", "arbitrary")),
        name="dsa_attention",
    )(q, qi, wi, kt, v, kit)


def _softplus(x):
    return jnp.maximum(x, 0.0) + jnp.log(1.0 + jnp.exp(-jnp.abs(x)))


def _gdn_kernel(qkv_ref, ab_ref, abt_ref, cw_ref, ac_ref, dc_ref, ar_ref, dr_ref, gn_ref,
                y_ref, xbuf_ref, act_ref, state_ref, *, tb):
    n_chunks = tb // CHUNK

    @pl.when(pl.program_id(1) == 0)
    def _():
        xbuf_ref[0:8, :] = jnp.zeros((8, xbuf_ref.shape[1]), F32)
        state_ref[...] = jnp.zeros(state_ref.shape, F32)

    xbuf_ref[8:8 + tb, :] = qkv_ref[...]
    for c in range(xbuf_ref.shape[1] // LANES):
        sl = slice(c * LANES, (c + 1) * LANES)
        y = xbuf_ref[5:5 + tb, sl] * cw_ref[0:1, sl]
        for j in range(1, CONV_K):
            y = y + xbuf_ref[5 + j:5 + j + tb, sl] * cw_ref[j:j + 1, sl]
        act_ref[:, sl] = y * (1.0 / (1.0 + jnp.exp(-y)))
    xbuf_ref[0:8, :] = xbuf_ref[tb:tb + 8, :]

    ab = ab_ref[...]
    g_col = -jnp.exp(ac_ref[...]) * _softplus(ab + dc_ref[...])
    beta_col = 1.0 / (1.0 + jnp.exp(-ab))
    g_row = -jnp.exp(ar_ref[...]) * _softplus(abt_ref[0:8, :] + dr_ref[...])

    r = lax.broadcasted_iota(I32, (tb, tb), 0)
    c = lax.broadcasted_iota(I32, (tb, tb), 1)
    chunk_shift = CHUNK.bit_length() - 1
    same = jnp.right_shift(r, chunk_shift) == jnp.right_shift(c, chunk_shift)
    lower = jnp.logical_and(same, r >= c)
    strict = jnp.logical_and(same, r > c)
    lower_m = jnp.where(lower, 1.0, 0.0).astype(BF16)
    upper_m = jnp.where(jnp.logical_and(same, r <= c), 1.0, 0.0).astype(BF16)
    same_m = jnp.where(same, 1.0, 0.0).astype(BF16)
    eye = jnp.where(r == c, 1.0, 0.0)
    n_levels = CHUNK.bit_length() - 1
    lvl_subs = []
    for lvl in range(n_levels):
        lvl_subs.append(jnp.logical_and(jnp.right_shift(r, lvl + 1) == jnp.right_shift(c, lvl + 1),
                                        jnp.logical_and((jnp.right_shift(r, lvl) & 1) == 1,
                                                        (jnp.right_shift(c, lvl) & 1) == 0)))
    lvl_masks = [jnp.where(sub, 1.0, 0.0).astype(BF16) for sub in lvl_subs]

    gparts = _split3(g_col)
    gc_col = _dot(lower_m, gparts[0]) + (_dot(lower_m, gparts[1]) + _dot(lower_m, gparts[2]))
    gl_col = _dot(same_m, gparts[0]) + (_dot(same_m, gparts[1]) + _dot(same_m, gparts[2]))
    rparts = _split3(g_row)
    gc_row = _dot(rparts[0], upper_m) + (_dot(rparts[1], upper_m) + _dot(rparts[2], upper_m))

    gn = gn_ref[...]
    hg = 4
    for h0 in range(0, GDN_HEADS, hg):
        heads = range(h0, h0 + hg)
        q_l, k_l, kbeta_l, vbeta_l, decay_l, gc_l, gl_l, m16_l, d_l = ([] for _ in range(9))
        for h in heads:
            q = act_ref[:, h * 128:(h + 1) * 128]
            k = act_ref[:, 1024 + h * 128:1024 + (h + 1) * 128]
            v = act_ref[:, 2048 + h * 128:2048 + (h + 1) * 128]
            q = q * lax.rsqrt(jnp.sum(q * q, axis=-1, keepdims=True) + EPS) * (GDN_DK ** -0.5)
            k = k * lax.rsqrt(jnp.sum(k * k, axis=-1, keepdims=True) + EPS)
            gc = gc_col[:, h:h + 1]
            beta = beta_col[:, 8 + h:9 + h]
            diff = gc - gc_row[h:h + 1, :]
            decay = jnp.where(lower, jnp.exp(jnp.where(lower, diff, 0.0)), 0.0)
            kbeta = k * beta
            nm = jnp.where(strict, -(_dot_nt(kbeta.astype(BF16), k.astype(BF16)) * decay), 0.0)
            q_l.append(q); k_l.append(k); kbeta_l.append(kbeta); vbeta_l.append(v * beta)
            decay_l.append(decay); gc_l.append(gc); gl_l.append(gl_col[:, h:h + 1])
            m16_l.append(nm.astype(BF16)); d_l.append(eye + jnp.where(lvl_subs[0], nm, 0.0))
        for lvl in range(1, n_levels):
            for i in range(hg):
                dh = d_l[i].astype(BF16)
                lh = m16_l[i] * lvl_masks[lvl]
                th = _dot(dh, lh).astype(BF16)
                d_l[i] = d_l[i] + _dot(th, dh)
        u_l, w_l, attn_l, qd_l, kd_l = [], [], [], [], []
        for i in range(hg):
            egc = jnp.exp(gc_l[i])
            rhs = jnp.concatenate([vbeta_l[i], kbeta_l[i] * egc], axis=1)
            sol = _dot(d_l[i].astype(BF16), rhs.astype(BF16))
            u_l.append(sol[:, :128])
            w_l.append(sol[:, 128:].astype(BF16))
            k16 = k_l[i].astype(BF16)
            attn_l.append(jnp.where(lower, _dot_nt(q_l[i].astype(BF16), k16) * decay_l[i], 0.0).astype(BF16))
            qd_l.append((q_l[i] * egc).astype(BF16))
            kd_l.append(k_l[i] * jnp.exp(gl_l[i] - gc_l[i]))
        st_l = [state_ref[h] for h in heads]
        outs = [[] for _ in range(hg)]
        for ci in range(n_chunks):
            rc = slice(ci * CHUNK, (ci + 1) * CHUNK)
            for i in range(hg):
                st16 = st_l[i].astype(BF16)
                v_new = u_l[i][rc] - _dot(w_l[i][rc], st16)
                v_new16 = v_new.astype(BF16)
                outs[i].append(_dot(qd_l[i][rc], st16) + _dot(attn_l[i][rc, rc], v_new16))
                kd_t = jnp.transpose(kd_l[i][rc]).astype(BF16)
                st_l[i] = (st_l[i] * jnp.exp(gl_l[i][ci * CHUNK:ci * CHUNK + 1])
                           + _dot(kd_t, v_new16))
        for i, h in enumerate(heads):
            state_ref[h] = st_l[i]
            y_ref[:, h * 128:(h + 1) * 128] = _rms(jnp.concatenate(outs[i], axis=0), gn)


def _gdn(qkv, ab, abt, conv_w, ac, dc, ar, dr, gn, tb):
    b, s, _ = qkv.shape
    blk = lambda c: pl.BlockSpec((None, tb, c), lambda bi, i: (bi, i, 0))
    return pl.pallas_call(
        functools.partial(_gdn_kernel, tb=tb),
        grid=(b, s // tb),
        in_specs=[blk(3072), blk(128), pl.BlockSpec((None, 16, tb), lambda bi, i: (bi, 0, i)),
                  _const_spec(conv_w.shape), _const_spec(ac.shape), _const_spec(dc.shape),
                  _const_spec(ar.shape), _const_spec(dr.shape), _const_spec(gn.shape)],
        out_specs=blk(1024),
        out_shape=jax.ShapeDtypeStruct((b, s, 1024), F32),
        scratch_shapes=[pltpu.VMEM((tb + 8, 3072), F32),
                        pltpu.VMEM((tb, 3072), F32),
                        pltpu.VMEM((GDN_HEADS, GDN_DK, GDN_DV), F32)],
        compiler_params=_cparams(("parallel", "arbitrary")),
        name="gated_delta",
    )(qkv, ab, abt, conv_w, ac, dc, ar, dr, gn)


def _sigmoid(x):
    return 1.0 / (1.0 + jnp.exp(-x))


def _merge_kernel(x_ref, ya_ref, yb_ref, gpre_ref, wg_ref, wo_ref, gpost_ref, o_ref):
    x = x_ref[...]
    h = _rms(x, gpre_ref[...]).astype(BF16)
    gates = _dot(h, wg_ref[...])
    y = _sigmoid(gates[:, :1024]) * ya_ref[...] + _sigmoid(gates[:, 1024:]) * yb_ref[...]
    z = _dot(y.astype(BF16), wo_ref[...])
    o_ref[...] = x + _rms(z, gpost_ref[...])


def _merge(x2, ya, yb, gpre, wg, wo, gpost, tm):
    n, d = x2.shape
    row = pl.BlockSpec((tm, d), lambda i: (i, 0))
    return pl.pallas_call(
        _merge_kernel,
        grid=(n // tm,),
        in_specs=[row, row, row, _const_spec((1, d)), _const_spec(wg.shape),
                  _const_spec(wo.shape), _const_spec((1, d))],
        out_specs=row,
        out_shape=jax.ShapeDtypeStruct((n, d), F32),
        compiler_params=_cparams(("parallel",)),
        name="merge_out",
    )(x2, ya, yb, gpre, wg, wo, gpost)


def _mem_kv_kernel(m_ref, g_ref, w_ref, kv_ref):
    h = _rms(m_ref[...], g_ref[...]).astype(BF16)
    kv_ref[...] = _dot(h, w_ref[...]).astype(BF16)


def _mem_kv(mem2, g, w):
    n, d = mem2.shape
    return pl.pallas_call(
        _mem_kv_kernel,
        out_shape=jax.ShapeDtypeStruct((n, w.shape[1]), BF16),
        compiler_params=pltpu.CompilerParams(vmem_limit_bytes=VMEM_LIMIT),
        name="mem_kv",
    )(mem2, g, w)


def _xattn_kernel(x_ref, kt_ref, v_ref, gpre_ref, wq_ref, wo_ref, gpost_ref, o_ref):
    x = x_ref[...]
    d = x.shape[1]
    hd = d // XATT_HEADS
    h = _rms(x, gpre_ref[...]).astype(BF16)
    q = (_dot(h, wq_ref[...]) * (hd ** -0.5)).astype(BF16)
    heads = []
    for i in range(XATT_HEADS):
        sl = slice(i * hd, (i + 1) * hd)
        s = _dot(q[:, sl], kt_ref[sl, :])
        p = jnp.exp(s - jnp.max(s, axis=1, keepdims=True))
        o = _dot(p.astype(BF16), v_ref[:, sl]) / jnp.sum(p, axis=1, keepdims=True)
        heads.append(o.astype(BF16))
    z = _dot(jnp.concatenate(heads, axis=1), wo_ref[...])
    o_ref[...] = x + _rms(z, gpost_ref[...])


def _xattn(x, kt, v, gpre, wq, wo, gpost, tm):
    b, s, d = x.shape
    n_mem = v.shape[1]
    row = pl.BlockSpec((None, tm, d), lambda bi, i: (bi, i, 0))
    return pl.pallas_call(
        _xattn_kernel,
        grid=(b, s // tm),
        in_specs=[row,
                  pl.BlockSpec((None, d, n_mem), lambda bi, i: (bi, 0, 0)),
                  pl.BlockSpec((None, n_mem, d), lambda bi, i: (bi, 0, 0)),
                  _const_spec((1, d)), _const_spec(wq.shape), _const_spec(wo.shape),
                  _const_spec((1, d))],
        out_specs=row,
        out_shape=jax.ShapeDtypeStruct((b, s, d), F32),
        compiler_params=_cparams(("parallel", "parallel")),
        name="mem_xattn",
    )(x, kt, v, gpre, wq, wo, gpost)


def _ffn_kernel(x_ref, gpre_ref, wgu_ref, wd_ref, gpost_ref, o_ref, *, ff, fc):
    x = x_ref[...]
    h = _rms(x, gpre_ref[...]).astype(BF16)
    z = jnp.zeros(x.shape, F32)
    for c0 in range(0, ff, fc):
        gate = _dot(h, wgu_ref[:, c0:c0 + fc])
        up = _dot(h, wgu_ref[:, ff + c0:ff + c0 + fc])
        act = (gate * _sigmoid(gate) * up).astype(BF16)
        z = z + _dot(act, wd_ref[c0:c0 + fc, :])
    o_ref[...] = x + _rms(z, gpost_ref[...])


def _ffn(x2, gpre, wgu, wd, gpost, tm):
    n, d = x2.shape
    ff = wd.shape[0]
    fc = ff // 2 if (ff // 2) % LANES == 0 else ff
    row = pl.BlockSpec((tm, d), lambda i: (i, 0))
    return pl.pallas_call(
        functools.partial(_ffn_kernel, ff=ff, fc=fc),
        grid=(n // tm,),
        in_specs=[row, _const_spec((1, d)), _const_spec(wgu.shape), _const_spec(wd.shape),
                  _const_spec((1, d))],
        out_specs=row,
        out_shape=jax.ShapeDtypeStruct((n, d), F32),
        compiler_params=_cparams(("parallel",)),
        name="swiglu",
    )(x2, gpre, wgu, wd, gpost)


def _rope_tables(positions):
    pos = positions.astype(F32).reshape(-1)[:, None]

    def tables(dim, reps):
        half = dim // 2
        inv_freq = ROPE_THETA ** (-jnp.arange(half, dtype=F32) * 2.0 / dim)
        ang = pos * inv_freq
        cos, sin = jnp.cos(ang), jnp.sin(ang)
        return (jnp.tile(jnp.concatenate([cos, cos], axis=1), (1, reps)),
                jnp.tile(jnp.concatenate([-sin, sin], axis=1), (1, reps)))

    return tables(ATT_HEAD_DIM, 1) + tables(IDX_DIM, 2)


def _pick_tile(n, pref):
    t = min(pref, n)
    while n % t:
        t //= 2
    return t


def kernel(x, mem, positions, norm_mix_pre, w_in, conv_w, a_log, dt_bias, gdn_norm, w_out,
           norm_mix_post, norm_x_pre, norm_mem, w_xq, w_xkv, w_xo, norm_x_post,
           norm_ffn_pre, w_gu, w_down, norm_ffn_post):
    b, s, d = x.shape
    n = b * s
    n_mem = mem.shape[1]
    depth = w_in.shape[0]
    topk = min(TOPK_MAX, s // 4)
    tm = _pick_tile(s, 512)
    tq = _pick_tile(s, 256)
    kb = _pick_tile(s, 512)
    tb = _pick_tile(s, 256)

    ca, sa, ci, si = (t.reshape(b, s, 128) for t in _rope_tables(positions))

    row1 = lambda v: v.reshape(1, -1).astype(F32)

    x2 = x.reshape(n, d)
    for l in range(depth):
        w_att, w_gdn, w_gate = _prep_w_in(w_in, l)

        g_pre = row1(norm_mix_pre[l])
        x3 = x2.reshape(b, s, d)
        q, kt, v, qi, kit, wi = _proj_att(x3, g_pre, w_att, ca, sa, ci, si, tm)
        ya = _dsa(q, qi, wi, kt, v, kit, topk, tq, kb)

        qkv, ab, abt = _proj_gdn(x3, g_pre, w_gdn, tm)
        pad_lanes = lambda vec: jnp.pad(vec.astype(F32), (0, 128 - GDN_HEADS)).reshape(1, 128)
        bcast_rows = lambda vec: jnp.broadcast_to(vec.astype(F32)[:, None], (GDN_HEADS, tb))
        yb = _gdn(qkv, ab, abt, conv_w[l].astype(F32),
                  pad_lanes(a_log[l]), pad_lanes(dt_bias[l]), bcast_rows(a_log[l]), bcast_rows(dt_bias[l]),
                  row1(gdn_norm[l]), tb)

        x2 = _merge(x2, ya.reshape(n, 1024), yb.reshape(n, 1024), g_pre, w_gate,
                    w_out[l].astype(BF16), row1(norm_mix_post[l]), tm)

        kv = _mem_kv(mem.reshape(b * n_mem, d), row1(norm_mem[l]), w_xkv[l].astype(BF16))
        kv = kv.reshape(b, n_mem, 2 * d)
        kt_mem = jnp.swapaxes(kv[:, :, :d], 1, 2)
        x2 = _xattn(x2.reshape(b, s, d), kt_mem, kv[:, :, d:], row1(norm_x_pre[l]),
                    w_xq[l].astype(BF16), w_xo[l].astype(BF16), row1(norm_x_post[l]), tm).reshape(n, d)

        x2 = _ffn(x2, row1(norm_ffn_pre[l]), w_gu[l].astype(BF16), w_down[l].astype(BF16),
                  row1(norm_ffn_post[l]), tm)
    return x2.reshape(b, s, d)
```

```python
import functools

import jax
import jax.numpy as jnp
import numpy as np
from jax import lax
from jax.experimental import pallas as pl
from jax.experimental.pallas import tpu as pltpu

EPS = 1e-6
ROPE_THETA = 10000.0
ATT_HEADS = 8
ATT_KV_HEADS = 2
ATT_HEAD_DIM = 128
IDX_HEADS = 4
IDX_DIM = 64
TOPK_MAX = 256
GDN_HEADS = 8
GDN_DK = 128
GDN_DV = 128
CONV_K = 4
CHUNK = 64
XATT_HEADS = 4

LANES = 128
VMEM_LIMIT = 56 * 1024 * 1024

F32 = jnp.float32
BF16 = jnp.bfloat16
I32 = jnp.int32

NEG_BIG = -1e30
LOG2E = 1.4426950408889634
INT_MIN = -2147483648
INT_MAX = 2147483647
NEG_INF_KEY = -2139095041


def _cparams(sem):
    return pltpu.CompilerParams(dimension_semantics=sem, vmem_limit_bytes=VMEM_LIMIT)


def _const_spec(shape):
    nd = len(shape)
    return pl.BlockSpec(shape, lambda *_: (0,) * nd, pipeline_mode=pl.Buffered(1))


def _rms(x, g):
    return x * lax.rsqrt(jnp.mean(x * x, axis=-1, keepdims=True) + EPS) * g


def _dot(a, b):
    return jnp.dot(a, b, preferred_element_type=F32)


def _dot_nt(a, b):
    return lax.dot_general(a, b, (((1,), (1,)), ((), ())), preferred_element_type=F32)


def _split3(a):
    a1 = a.astype(BF16)
    r1 = a - a1.astype(F32)
    a2 = r1.astype(BF16)
    a3 = (r1 - a2.astype(F32)).astype(BF16)
    return a1, a2, a3


A_Q, A_KV, I_Q, I_K, I_W, B_QKV = 1024, 256, 256, 64, 4, 3072
O_IQ = A_Q + 2 * A_KV
O_IK = O_IQ + I_Q
O_IW = O_IK + I_K
O_B = O_IW + I_W
O_BA = O_B + B_QKV
O_GA = O_BA + 2 * GDN_HEADS
IN_COLS = O_GA + 2 * 1024


def _prep_w_in_kernel(w_ref, att_ref, gdn_ref, gate_ref):
    w = w_ref[...]
    zeros = lambda c: jnp.zeros((w.shape[0], c), F32)
    att = [w[:, :O_IQ]]
    for h in range(IDX_HEADS):
        att += [w[:, O_IQ + h * IDX_DIM:O_IQ + (h + 1) * IDX_DIM], zeros(128 - IDX_DIM)]
    att += [w[:, O_IK:O_IW], zeros(128 - I_K), w[:, O_IW:O_B], zeros(128 - I_W)]
    att_ref[...] = jnp.concatenate(att, axis=1).astype(BF16)
    gdn_ref[...] = jnp.concatenate([w[:, O_B:O_GA], zeros(128 - 2 * GDN_HEADS)], axis=1).astype(BF16)
    gate_ref[...] = w[:, O_GA:].astype(BF16)


def _prep_w_in(w_in, layer):
    _, d, cols = w_in.shape
    assert cols == IN_COLS
    tr = 128
    widths = (O_IQ + IDX_HEADS * 128 + 256, B_QKV + 128, 2048)
    return pl.pallas_call(
        _prep_w_in_kernel,
        grid=(d // tr,),
        in_specs=[pl.BlockSpec((None, tr, cols), lambda i: (layer, i, 0))],
        out_specs=[pl.BlockSpec((tr, c), lambda i: (i, 0)) for c in widths],
        out_shape=[jax.ShapeDtypeStruct((d, c), BF16) for c in widths],
        compiler_params=_cparams(("parallel",)),
        name="prep_w_in",
    )(w_in)


def _rope128(x, cos, sin_signed):
    return x * cos + pltpu.roll(x, 64, 1) * sin_signed


def _rope64(x, cos, sin_signed, first_half):
    partner = jnp.where(first_half, pltpu.roll(x, 96, 1), pltpu.roll(x, 32, 1))
    return x * cos + partner * sin_signed


def _proj_att_kernel(x_ref, g_ref, w_ref, ca_ref, sa_ref, ci_ref, si_ref,
                     q_ref, kt_ref, v_ref, qi_ref, kit_ref, wi_ref):
    h = _rms(x_ref[...], g_ref[...]).astype(BF16)
    p = _dot(h, w_ref[...])
    ca, sa = ca_ref[...], sa_ref[...]
    ci, si = ci_ref[...], si_ref[...]
    q_scale = ATT_HEAD_DIM ** -0.5 * LOG2E
    for hd in range(ATT_HEADS):
        sl = slice(hd * 128, (hd + 1) * 128)
        q_ref[:, sl] = (_rope128(p[:, sl], ca, sa) * q_scale).astype(BF16)
    for g in range(ATT_KV_HEADS):
        k_g = _rope128(p[:, 1024 + g * 128:1024 + (g + 1) * 128], ca, sa)
        kt_ref[g * 128:(g + 1) * 128, :] = jnp.transpose(k_g).astype(BF16)
    v_ref[...] = p[:, 1280:1536].astype(BF16)
    lane = lax.broadcasted_iota(I32, ci.shape, 1)
    first_half = (lane & 63) < 32
    idx_scale = IDX_DIM ** -0.5
    for hd in range(IDX_HEADS):
        off = 1536 + hd * 128
        qi_ref[:, hd * 128:(hd + 1) * 128] = (
            _rope64(p[:, off:off + 128], ci, si, first_half) * idx_scale).astype(BF16)
    kit_ref[...] = jnp.transpose(_rope64(p[:, 2048:2176], ci, si, first_half)).astype(BF16)
    wi_ref[...] = p[:, 2176:2304] * (IDX_HEADS ** -0.5)


def _proj_att(x, g, w, ca, sa, ci, si, tm):
    b, s, d = x.shape
    row = lambda c: pl.BlockSpec((None, tm, c), lambda bi, i: (bi, i, 0))
    col = lambda r: pl.BlockSpec((None, r, tm), lambda bi, i: (bi, 0, i))
    sds = jax.ShapeDtypeStruct
    return pl.pallas_call(
        _proj_att_kernel,
        grid=(b, s // tm),
        in_specs=[row(d), _const_spec((1, d)), _const_spec(w.shape),
                  row(128), row(128), row(128), row(128)],
        out_specs=[row(1024), col(256), row(256), row(512), col(128), row(128)],
        out_shape=[sds((b, s, 1024), BF16), sds((b, 256, s), BF16), sds((b, s, 256), BF16),
                   sds((b, s, 512), BF16), sds((b, 128, s), BF16), sds((b, s, 128), F32)],
        compiler_params=_cparams(("parallel", "parallel")),
        name="proj_att",
    )(x, g, w, ca, sa, ci, si)


def _proj_gdn_kernel(x_ref, g_ref, w_ref, qkv_ref, ab_ref, abt_ref):
    h = _rms(x_ref[...], g_ref[...]).astype(BF16)
    p = _dot(h, w_ref[...])
    qkv_ref[...] = p[:, :3072]
    ab = p[:, 3072:3200]
    ab_ref[...] = ab
    abt_ref[...] = jnp.transpose(ab)[:16, :]


def _proj_gdn(x, g, w, tm):
    b, s, d = x.shape
    row = lambda c: pl.BlockSpec((None, tm, c), lambda bi, i: (bi, i, 0))
    sds = jax.ShapeDtypeStruct
    return pl.pallas_call(
        _proj_gdn_kernel,
        grid=(b, s // tm),
        in_specs=[row(d), _const_spec((1, d)), _const_spec(w.shape)],
        out_specs=[row(3072), row(128), pl.BlockSpec((None, 16, tm), lambda bi, i: (bi, 0, i))],
        out_shape=[sds((b, s, 3072), F32), sds((b, s, 128), F32), sds((b, 16, s), F32)],
        compiler_params=_cparams(("parallel", "parallel")),
        name="proj_gdn",
    )(x, g, w)


def _key_of(v):
    b = pltpu.bitcast(v, I32)
    return jnp.where(b < 0, b ^ 0x7FFFFFFF, b)


def _val_of(k):
    return pltpu.bitcast(jnp.where(k < 0, k ^ 0x7FFFFFFF, k), F32)


def _dsa_kernel(q_ref, qi_ref, wi_ref, kt_ref, v_ref, kit_ref, o_ref,
                keys_ref, qis_ref, qg_ref, bias_ref, s_ref, p_ref, acc_ref, m_ref, l_ref, run_ref, tri_ref,
                *, tq, kb, topk, seq):
    t0 = pl.program_id(1) * tq
    nkb = (t0 + tq + kb - 1) // kb
    n_lane_tiles = kb // LANES
    row_ids = t0 + lax.broadcasted_iota(I32, (tq, kb), 0)
    col_iota = lax.broadcasted_iota(I32, (tq, kb), 1)
    kf = float(topk)

    for h in range(IDX_HEADS):
        qis_ref[h * tq:(h + 1) * tq, :] = qi_ref[:, h * 128:(h + 1) * 128]
    wi = wi_ref[...]
    w_heads = [wi[:, h:h + 1] for h in range(IDX_HEADS)]

    def score_block(j, carry, masked):
        mx, mn = carry
        c0 = pl.multiple_of(j * kb, kb)
        halves = []
        for a0 in range(0, kb, 256):
            lg = _dot(qis_ref[...], kit_ref[:, pl.ds(c0 + a0, 256)])
            part = jnp.zeros((tq, 256), F32)
            for h in range(IDX_HEADS):
                part = part + w_heads[h] * jnp.maximum(lg[h * tq:(h + 1) * tq], 0.0)
            halves.append(part)
        sc = jnp.concatenate(halves, axis=1)
        sc = jnp.where(sc == 0.0, 0.0, sc)
        for i in range(n_lane_tiles):
            mx = jnp.maximum(mx, sc[:, i * LANES:(i + 1) * LANES])
            mn = jnp.minimum(mn, sc[:, i * LANES:(i + 1) * LANES])
        if masked:
            sc = jnp.where(c0 + col_iota <= row_ids, sc, -jnp.inf)
        keys_ref[:, pl.ds(c0, kb)] = _key_of(sc)
        return mx, mn

    n_full = t0 // kb
    ext0 = (jnp.full((tq, LANES), -jnp.inf, F32), jnp.full((tq, LANES), jnp.inf, F32))
    ext1 = lax.fori_loop(0, n_full, functools.partial(score_block, masked=False), ext0)
    mx, mn = lax.fori_loop(n_full, nkb, functools.partial(score_block, masked=True), ext1)
    row_max = jnp.max(mx, axis=1, keepdims=True)
    row_min = jnp.min(mn, axis=1, keepdims=True)

    n_row_tiles = tq // LANES if tq % LANES == 0 else 1
    rt = tq // n_row_tiles

    def as_col(row):
        return jnp.transpose(row)[:, :1]

    def count_ge(cands):
        nc = len(cands)
        tiles = range(n_row_tiles)
        cb = [[c if isinstance(c, int) else
               jnp.transpose(jnp.broadcast_to(c[:1, t * rt:(t + 1) * rt], (LANES, rt)))
               for t in tiles] for c in cands]

        def tile_counts(t):
            def body(j, accs):
                c0 = pl.multiple_of(j * kb, kb)
                accs = list(accs)
                blk = keys_ref[t * rt:(t + 1) * rt, pl.ds(c0, kb)]
                for i in range(n_lane_tiles):
                    for k in range(nc):
                        accs[k] = accs[k] + jnp.where(blk[:, i * LANES:(i + 1) * LANES] >= cb[k][t], 1.0, 0.0)
                return tuple(accs)

            z = jnp.zeros((rt, LANES), F32)
            return lax.fori_loop(0, nkb, body, (z,) * nc)

        accs = [tile_counts(t) for t in tiles]
        row_sums = lambda a: jnp.broadcast_to(jnp.sum(jnp.transpose(a), axis=0, keepdims=True), (8, rt))
        return [jnp.concatenate([row_sums(accs[t][k]) for t in tiles], axis=1) for k in range(nc)]

    as_row = lambda col: jnp.transpose(jnp.broadcast_to(col, (tq, 8)))
    n_visible = (t0 + lax.broadcasted_iota(I32, (8, tq), 1) + 1).astype(F32)
    few = n_visible < kf
    f_pos, f_zero = count_ge([1, 0])
    above = f_pos >= kf
    below = f_zero < kf
    at_zero = jnp.logical_and(jnp.logical_not(above), jnp.logical_not(below))
    key_min = _key_of(as_row(row_min))
    key_max = _key_of(as_row(row_max))
    lo0 = jnp.where(few, NEG_INF_KEY + 1, jnp.where(above, 1, jnp.where(below, key_min, 0)))
    flo0 = jnp.where(few, n_visible, jnp.where(above, f_pos, jnp.where(below, n_visible, f_zero)))
    hi0 = jnp.where(above, jnp.minimum(key_max, 2147483646) + 1, jnp.where(below, 0, 1))
    fhi0 = jnp.where(above, 0.0, jnp.where(below, f_zero, f_pos))
    log_target = float(np.log(topk + 0.5))
    glog = lambda c: jnp.log(jnp.maximum(c, 0.5)) - log_target
    done0 = jnp.where(jnp.logical_or(jnp.logical_or(few, at_zero), flo0 == kf), 1.0, 0.0)
    n_interp = 24

    def peel_min(base, flip):
        tiles = range(n_row_tiles)
        bb = [jnp.broadcast_to(base[t * rt:(t + 1) * rt], (rt, LANES)) for t in tiles]
        fb = [jnp.broadcast_to(flip[t * rt:(t + 1) * rt], (rt, LANES)) for t in tiles]

        def tile_min(t):
            def body(j, acc):
                c0 = pl.multiple_of(j * kb, kb)
                blk = keys_ref[t * rt:(t + 1) * rt, pl.ds(c0, kb)]
                for i in range(n_lane_tiles):
                    x = blk[:, i * LANES:(i + 1) * LANES] ^ fb[t]
                    acc = jnp.minimum(acc, jnp.where(x >= bb[t], x, INT_MAX))
                return acc

            acc = lax.fori_loop(0, nkb, body, jnp.full((rt, LANES), INT_MAX, I32))
            return jnp.min(acc, axis=1, keepdims=True)

        col = jnp.concatenate([tile_min(t) for t in tiles], axis=0)
        return jnp.transpose(jnp.broadcast_to(col, (tq, 8)))

    def probe(st, forced=None, use_forced=None):
        it, _, _, lo, hi, flo, fhi, glo, ghi, side, done = st
        v_lo, v_hi = _val_of(lo), _val_of(hi)
        v_model = (v_lo * ghi - v_hi * glo) / (ghi - glo)
        inside = flo - fhi
        v_even = v_lo + (v_hi - v_lo) * ((flo - kf + 0.5) / (inside + 1.0))
        c_interp = _key_of(jnp.where(inside <= 64.0, v_even, v_model))
        c_mid = jnp.right_shift(lo, 1) + jnp.right_shift(hi, 1) + (lo & hi & 1)
        cand = jnp.where(it >= n_interp, c_mid, c_interp)
        if forced is not None:
            cand = jnp.where(use_forced, forced, cand)
        cand = jnp.minimum(jnp.maximum(cand, lo + 1), hi - 1)
        c = count_ge([cand])[0]
        active = done < 0.5
        to_lo = jnp.logical_and(active, c >= kf)
        to_hi = jnp.logical_and(active, c < kf)
        gc = glog(c)
        ghi_n = jnp.where(to_hi, gc, jnp.where(jnp.logical_and(to_lo, side > 0.0), ghi * 0.5, ghi))
        glo_n = jnp.where(to_lo, gc, jnp.where(jnp.logical_and(to_hi, side < 0.0), glo * 0.5, glo))
        side = jnp.where(to_lo, 1.0, jnp.where(to_hi, -1.0, side))
        lo = jnp.where(to_lo, cand, lo)
        flo = jnp.where(to_lo, c, flo)
        hi = jnp.where(to_hi, cand, hi)
        fhi = jnp.where(to_hi, c, fhi)
        fin = jnp.logical_or(flo == kf, hi - 1 == lo)
        done = jnp.where(fin, 1.0, done)
        near = jnp.logical_or(flo - kf <= 1.0, kf - fhi <= 1.0)
        n_far = jnp.sum(jnp.where(near, 0.0, 1.0 - done))
        return it + 1, n_far, jnp.sum(1.0 - done), lo, hi, flo, fhi, glo_n, ghi_n, side, done

    n_pass_cap = n_interp + 34
    near0 = jnp.logical_or(flo0 - kf <= 1.0, kf - fhi0 <= 1.0)
    st = (jnp.int32(0), jnp.sum(jnp.where(near0, 0.0, 1.0 - done0)), jnp.sum(1.0 - done0),
          lo0, hi0, flo0, fhi0, glog(flo0), glog(fhi0), jnp.zeros((8, tq), F32), done0)
    st = lax.while_loop(lambda st: jnp.logical_and(st[0] < n_pass_cap, st[1] > 0.0), probe, st)
    lo, hi, flo, fhi, done = st[3], st[4], st[5], st[6], st[10]
    active = done < 0.5
    drop_one = jnp.logical_and(active, flo - kf <= 1.0)
    add_one = jnp.logical_and(jnp.logical_and(active, jnp.logical_not(drop_one)), kf - fhi <= 1.0)
    flip = jnp.where(add_one, -1, 0)
    edge = peel_min(as_col(jnp.where(add_one, -hi, lo)), as_col(flip))
    forced = jnp.where(add_one, edge ^ flip, edge + 1)
    st = probe(st, forced, jnp.logical_or(drop_one, add_one))
    st = lax.while_loop(lambda st: jnp.logical_and(st[0] < n_pass_cap + 2, st[2] > 0.0), probe, st)
    thr = as_col(st[3])
    tie_rows = jnp.logical_and(st[5] > kf, jnp.logical_not(few))
    need = as_col(jnp.where(tie_rows, kf - st[6], float(2 * seq)))

    run_ref[...] = jnp.zeros(run_ref.shape, F32)
    rr = lax.broadcasted_iota(I32, (LANES, 2 * LANES), 0)
    cc = lax.broadcasted_iota(I32, (LANES, 2 * LANES), 1)
    tri_ref[...] = jnp.where(jnp.logical_or(rr <= cc, cc >= LANES), 1.0, 0.0).astype(BF16)

    rep = ATT_HEADS // ATT_KV_HEADS
    rows = rep * tq
    rc = 32
    for g in range(ATT_KV_HEADS):
        for r in range(rep):
            hd = g * rep + r
            qg_ref[g, r * tq:(r + 1) * tq, :] = q_ref[:, hd * 128:(hd + 1) * 128]
    acc_ref[...] = jnp.zeros(acc_ref.shape, F32)
    l_ref[...] = jnp.zeros(l_ref.shape, F32)
    m_ref[...] = jnp.full(m_ref.shape, NEG_BIG, F32)

    def attend_block(j, carry):
        c0 = pl.multiple_of(j * kb, kb)
        left = jnp.broadcast_to(need - run_ref[...], (tq, LANES))
        for i in range(n_lane_tiles):
            keys = keys_ref[:, pl.ds(c0 + i * LANES, LANES)]
            tied = keys == thr
            cnt = _dot(jnp.where(tied, 1.0, 0.0).astype(BF16), tri_ref[...])
            ok = jnp.logical_and(tied, cnt[:, :LANES] <= left)
            bias_ref[:, i * LANES:(i + 1) * LANES] = jnp.where(keys > thr, 0.0, jnp.where(ok, 0.0, NEG_BIG))
            left = left - cnt[:, LANES:]
        run_ref[...] = need - left[:, :1]
        for g in range(ATT_KV_HEADS):
            for a0 in range(0, kb, 256):
                s_ref[g, :, a0:a0 + 256] = _dot(qg_ref[g], kt_ref[g * 128:(g + 1) * 128, pl.ds(c0 + a0, 256)])
        for g in range(ATT_KV_HEADS):
            for r0 in range(0, rows, rc):
                rs = slice(r0, r0 + rc)
                b0 = r0 % tq
                s = s_ref[g, rs, :] + bias_ref[b0:b0 + rc, :]
                m_prev = m_ref[g, rs, :]
                m_next = jnp.maximum(m_prev, jnp.max(s, axis=1, keepdims=True))
                p = jnp.exp2(s - jnp.concatenate([m_next] * n_lane_tiles, axis=1))
                alpha = jnp.exp2(m_prev - m_next)
                l_ref[g, rs, :] = alpha * l_ref[g, rs, :] + jnp.sum(p, axis=1, keepdims=True)
                m_ref[g, rs, :] = m_next
                acc_ref[g, rs, :] = acc_ref[g, rs, :] * alpha
                p_ref[g, rs, :] = p.astype(BF16)
            acc_ref[g] += _dot(p_ref[g], v_ref[pl.ds(c0, kb), g * 128:(g + 1) * 128])
        return carry

    lax.fori_loop(0, nkb, attend_block, 0)

    for g in range(ATT_KV_HEADS):
        out = acc_ref[g] / l_ref[g]
        for r in range(rep):
            hd = g * rep + r
            o_ref[:, hd * 128:(hd + 1) * 128] = out[r * tq:(r + 1) * tq]


def _dsa(q, qi, wi, kt, v, kit, topk, tq, kb):
    b, s, _ = q.shape
    rep = ATT_HEADS // ATT_KV_HEADS
    qspec = lambda c: pl.BlockSpec((None, tq, c), lambda bi, i: (bi, i, 0))
    per_batch = lambda r, c: pl.BlockSpec((None, r, c), lambda bi, i: (bi, 0, 0),
                                          pipeline_mode=pl.Buffered(1))
    return pl.pallas_call(
        functools.partial(_dsa_kernel, tq=tq, kb=kb, topk=topk, seq=s),
        grid=(b, s // tq),
        in_specs=[qspec(1024), qspec(512), qspec(128),
                  per_batch(256, s), per_batch(s, 256), per_batch(128, s)],
        out_specs=qspec(1024),
        out_shape=jax.ShapeDtypeStruct((b, s, 1024), F32),
        scratch_shapes=[pltpu.VMEM((tq, s), I32),
                        pltpu.VMEM((IDX_HEADS * tq, 128), BF16),
                        pltpu.VMEM((ATT_KV_HEADS, rep * tq, 128), BF16),
                        pltpu.VMEM((tq, kb), F32),
                        pltpu.VMEM((ATT_KV_HEADS, rep * tq, kb), F32),
                        pltpu.VMEM((ATT_KV_HEADS, rep * tq, kb), BF16),
                        pltpu.VMEM((ATT_KV_HEADS, rep * tq, 128), F32),
                        pltpu.VMEM((ATT_KV_HEADS, rep * tq, 128), F32),
                        pltpu.VMEM((ATT_KV_HEADS, rep * tq, 128), F32),
                        pltpu.VMEM((tq, 1), F32),
                        pltpu.VMEM((LANES, 2 * LANES), BF16)],
        compiler_params=_cparams(("parallel", "arbitrary")),
        name="dsa_attention",
    )(q, qi, wi, kt, v, kit)


def _softplus(x):
    return jnp.maximum(x, 0.0) + jnp.log(1.0 + jnp.exp(-jnp.abs(x)))


def _gdn_kernel(qkv_ref, ab_ref, abt_ref, cw_ref, ac_ref, dc_ref, ar_ref, dr_ref, gn_ref,
                y_ref, xbuf_ref, act_ref, state_ref, *, tb):
    ck = tb
    n_chunks = tb // ck

    @pl.when(pl.program_id(1) == 0)
    def _():
        xbuf_ref[0:8, :] = jnp.zeros((8, xbuf_ref.shape[1]), F32)
        state_ref[...] = jnp.zeros(state_ref.shape, F32)

    xbuf_ref[8:8 + tb, :] = qkv_ref[...]
    for c in range(xbuf_ref.shape[1] // LANES):
        sl = slice(c * LANES, (c + 1) * LANES)
        y = xbuf_ref[5:5 + tb, sl] * cw_ref[0:1, sl]
        for j in range(1, CONV_K):
            y = y + xbuf_ref[5 + j:5 + j + tb, sl] * cw_ref[j:j + 1, sl]
        act_ref[:, sl] = y * (1.0 / (1.0 + jnp.exp(-y)))
    xbuf_ref[0:8, :] = xbuf_ref[tb:tb + 8, :]

    ab = ab_ref[...]
    g_col = -jnp.exp(ac_ref[...]) * _softplus(ab + dc_ref[...])
    beta_col = 1.0 / (1.0 + jnp.exp(-ab))
    g_row = -jnp.exp(ar_ref[...]) * _softplus(abt_ref[0:8, :] + dr_ref[...])

    r = lax.broadcasted_iota(I32, (tb, tb), 0)
    c = lax.broadcasted_iota(I32, (tb, tb), 1)
    chunk_shift = ck.bit_length() - 1
    same = jnp.right_shift(r, chunk_shift) == jnp.right_shift(c, chunk_shift)
    lower = jnp.logical_and(same, r >= c)
    strict = jnp.logical_and(same, r > c)
    lower_m = jnp.where(lower, 1.0, 0.0).astype(BF16)
    upper_m = jnp.where(jnp.logical_and(same, r <= c), 1.0, 0.0).astype(BF16)
    same_m = jnp.where(same, 1.0, 0.0).astype(BF16)
    eye = jnp.where(r == c, 1.0, 0.0)
    n_levels = ck.bit_length() - 1
    lvl_subs = []
    for lvl in range(n_levels):
        lvl_subs.append(jnp.logical_and(jnp.right_shift(r, lvl + 1) == jnp.right_shift(c, lvl + 1),
                                        jnp.logical_and((jnp.right_shift(r, lvl) & 1) == 1,
                                                        (jnp.right_shift(c, lvl) & 1) == 0)))
    lvl_masks = [jnp.where(sub, 1.0, 0.0).astype(BF16) for sub in lvl_subs]

    gparts = _split3(g_col)
    gc_col = _dot(lower_m, gparts[0]) + (_dot(lower_m, gparts[1]) + _dot(lower_m, gparts[2]))
    gl_col = _dot(same_m, gparts[0]) + (_dot(same_m, gparts[1]) + _dot(same_m, gparts[2]))
    rparts = _split3(g_row)
    gc_row = _dot(rparts[0], upper_m) + (_dot(rparts[1], upper_m) + _dot(rparts[2], upper_m))

    gn = gn_ref[...]
    hg = 4
    for h0 in range(0, GDN_HEADS, hg):
        heads = range(h0, h0 + hg)
        q_l, k_l, kbeta_l, vbeta_l, decay_l, gc_l, gl_l, m16_l, d_l = ([] for _ in range(9))
        for h in heads:
            q = act_ref[:, h * 128:(h + 1) * 128]
            k = act_ref[:, 1024 + h * 128:1024 + (h + 1) * 128]
            v = act_ref[:, 2048 + h * 128:2048 + (h + 1) * 128]
            q = q * lax.rsqrt(jnp.sum(q * q, axis=-1, keepdims=True) + EPS) * (GDN_DK ** -0.5)
            k = k * lax.rsqrt(jnp.sum(k * k, axis=-1, keepdims=True) + EPS)
            gc = gc_col[:, h:h + 1]
            beta = beta_col[:, 8 + h:9 + h]
            diff = gc - gc_row[h:h + 1, :]
            decay = jnp.where(lower, jnp.exp(jnp.where(lower, diff, 0.0)), 0.0)
            kbeta = k * beta
            nm = jnp.where(strict, -(_dot_nt(kbeta.astype(BF16), k.astype(BF16)) * decay), 0.0)
            q_l.append(q); k_l.append(k); kbeta_l.append(kbeta); vbeta_l.append(v * beta)
            decay_l.append(decay); gc_l.append(gc); gl_l.append(gl_col[:, h:h + 1])
            m16_l.append(nm.astype(BF16)); d_l.append(eye + jnp.where(lvl_subs[0], nm, 0.0))
        for lvl in range(1, n_levels):
            for i in range(hg):
                dh = d_l[i].astype(BF16)
                lh = m16_l[i] * lvl_masks[lvl]
                th = _dot(dh, lh).astype(BF16)
                d_l[i] = d_l[i] + _dot(th, dh)
        u_l, w_l, attn_l, qd_l, kd_l = [], [], [], [], []
        for i in range(hg):
            egc = jnp.exp(gc_l[i])
            rhs = jnp.concatenate([vbeta_l[i], kbeta_l[i] * egc], axis=1)
            sol = _dot(d_l[i].astype(BF16), rhs.astype(BF16))
            u_l.append(sol[:, :128])
            w_l.append(sol[:, 128:].astype(BF16))
            k16 = k_l[i].astype(BF16)
            attn_l.append(jnp.where(lower, _dot_nt(q_l[i].astype(BF16), k16) * decay_l[i], 0.0).astype(BF16))
            qd_l.append((q_l[i] * egc).astype(BF16))
            kd_l.append(k_l[i] * jnp.exp(gl_l[i] - gc_l[i]))
        st_l = [state_ref[h] for h in heads]
        outs = [[] for _ in range(hg)]
        for ci in range(n_chunks):
            rc = slice(ci * ck, (ci + 1) * ck)
            for i in range(hg):
                st16 = st_l[i].astype(BF16)
                v_new = u_l[i][rc] - _dot(w_l[i][rc], st16)
                v_new16 = v_new.astype(BF16)
                outs[i].append(_dot(qd_l[i][rc], st16) + _dot(attn_l[i][rc, rc], v_new16))
                kd_t = jnp.transpose(kd_l[i][rc]).astype(BF16)
                st_l[i] = (st_l[i] * jnp.exp(gl_l[i][ci * ck:ci * ck + 1])
                           + _dot(kd_t, v_new16))
        for i, h in enumerate(heads):
            state_ref[h] = st_l[i]
            y_ref[:, h * 128:(h + 1) * 128] = _rms(jnp.concatenate(outs[i], axis=0), gn)


def _gdn(qkv, ab, abt, conv_w, ac, dc, ar, dr, gn, tb):
    b, s, _ = qkv.shape
    blk = lambda c: pl.BlockSpec((None, tb, c), lambda bi, i: (bi, i, 0))
    return pl.pallas_call(
        functools.partial(_gdn_kernel, tb=tb),
        grid=(b, s // tb),
        in_specs=[blk(3072), blk(128), pl.BlockSpec((None, 16, tb), lambda bi, i: (bi, 0, i)),
                  _const_spec(conv_w.shape), _const_spec(ac.shape), _const_spec(dc.shape),
                  _const_spec(ar.shape), _const_spec(dr.shape), _const_spec(gn.shape)],
        out_specs=blk(1024),
        out_shape=jax.ShapeDtypeStruct((b, s, 1024), F32),
        scratch_shapes=[pltpu.VMEM((tb + 8, 3072), F32),
                        pltpu.VMEM((tb, 3072), F32),
                        pltpu.VMEM((GDN_HEADS, GDN_DK, GDN_DV), F32)],
        compiler_params=_cparams(("parallel", "arbitrary")),
        name="gated_delta",
    )(qkv, ab, abt, conv_w, ac, dc, ar, dr, gn)


def _sigmoid(x):
    return 1.0 / (1.0 + jnp.exp(-x))


def _merge_kernel(x_ref, ya_ref, yb_ref, gpre_ref, wg_ref, wo_ref, gpost_ref, o_ref):
    x = x_ref[...]
    h = _rms(x, gpre_ref[...]).astype(BF16)
    gates = _dot(h, wg_ref[...])
    y = _sigmoid(gates[:, :1024]) * ya_ref[...] + _sigmoid(gates[:, 1024:]) * yb_ref[...]
    z = _dot(y.astype(BF16), wo_ref[...])
    o_ref[...] = x + _rms(z, gpost_ref[...])


def _merge(x2, ya, yb, gpre, wg, wo, gpost, tm):
    n, d = x2.shape
    row = pl.BlockSpec((tm, d), lambda i: (i, 0))
    return pl.pallas_call(
        _merge_kernel,
        grid=(n // tm,),
        in_specs=[row, row, row, _const_spec((1, d)), _const_spec(wg.shape),
                  _const_spec(wo.shape), _const_spec((1, d))],
        out_specs=row,
        out_shape=jax.ShapeDtypeStruct((n, d), F32),
        compiler_params=_cparams(("parallel",)),
        name="merge_out",
    )(x2, ya, yb, gpre, wg, wo, gpost)


def _mem_kv_kernel(m_ref, g_ref, w_ref, kv_ref):
    h = _rms(m_ref[...], g_ref[...]).astype(BF16)
    kv_ref[...] = _dot(h, w_ref[...]).astype(BF16)


def _mem_kv(mem2, g, w):
    n, d = mem2.shape
    return pl.pallas_call(
        _mem_kv_kernel,
        out_shape=jax.ShapeDtypeStruct((n, w.shape[1]), BF16),
        compiler_params=pltpu.CompilerParams(vmem_limit_bytes=VMEM_LIMIT),
        name="mem_kv",
    )(mem2, g, w)


def _xattn_kernel(x_ref, kt_ref, v_ref, gpre_ref, wq_ref, wo_ref, gpost_ref, o_ref):
    x = x_ref[...]
    d = x.shape[1]
    hd = d // XATT_HEADS
    h = _rms(x, gpre_ref[...]).astype(BF16)
    q = (_dot(h, wq_ref[...]) * (hd ** -0.5)).astype(BF16)
    heads = []
    for i in range(XATT_HEADS):
        sl = slice(i * hd, (i + 1) * hd)
        s = _dot(q[:, sl], kt_ref[sl, :])
        p = jnp.exp(s - jnp.max(s, axis=1, keepdims=True))
        o = _dot(p.astype(BF16), v_ref[:, sl]) / jnp.sum(p, axis=1, keepdims=True)
        heads.append(o.astype(BF16))
    z = _dot(jnp.concatenate(heads, axis=1), wo_ref[...])
    o_ref[...] = x + _rms(z, gpost_ref[...])


def _xattn(x, kt, v, gpre, wq, wo, gpost, tm):
    b, s, d = x.shape
    n_mem = v.shape[1]
    row = pl.BlockSpec((None, tm, d), lambda bi, i: (bi, i, 0))
    return pl.pallas_call(
        _xattn_kernel,
        grid=(b, s // tm),
        in_specs=[row,
                  pl.BlockSpec((None, d, n_mem), lambda bi, i: (bi, 0, 0)),
                  pl.BlockSpec((None, n_mem, d), lambda bi, i: (bi, 0, 0)),
                  _const_spec((1, d)), _const_spec(wq.shape), _const_spec(wo.shape),
                  _const_spec((1, d))],
        out_specs=row,
        out_shape=jax.ShapeDtypeStruct((b, s, d), F32),
        compiler_params=_cparams(("parallel", "parallel")),
        name="mem_xattn",
    )(x, kt, v, gpre, wq, wo, gpost)


def _ffn_kernel(x_ref, gpre_ref, wgu_ref, wd_ref, gpost_ref, o_ref, *, ff, fc):
    x = x_ref[...]
    h = _rms(x, gpre_ref[...]).astype(BF16)
    z = jnp.zeros(x.shape, F32)
    for c0 in range(0, ff, fc):
        gate = _dot(h, wgu_ref[:, c0:c0 + fc])
        up = _dot(h, wgu_ref[:, ff + c0:ff + c0 + fc])
        act = (gate * _sigmoid(gate) * up).astype(BF16)
        z = z + _dot(act, wd_ref[c0:c0 + fc, :])
    o_ref[...] = x + _rms(z, gpost_ref[...])


def _ffn(x2, gpre, wgu, wd, gpost, tm):
    n, d = x2.shape
    ff = wd.shape[0]
    fc = ff // 2 if (ff // 2) % LANES == 0 else ff
    row = pl.BlockSpec((tm, d), lambda i: (i, 0))
    return pl.pallas_call(
        functools.partial(_ffn_kernel, ff=ff, fc=fc),
        grid=(n // tm,),
        in_specs=[row, _const_spec((1, d)), _const_spec(wgu.shape), _const_spec(wd.shape),
                  _const_spec((1, d))],
        out_specs=row,
        out_shape=jax.ShapeDtypeStruct((n, d), F32),
        compiler_params=_cparams(("parallel",)),
        name="swiglu",
    )(x2, gpre, wgu, wd, gpost)


def _rope_tables(positions):
    pos = positions.astype(F32).reshape(-1)[:, None]

    def tables(dim, reps):
        half = dim // 2
        inv_freq = ROPE_THETA ** (-jnp.arange(half, dtype=F32) * 2.0 / dim)
        ang = pos * inv_freq
        cos, sin = jnp.cos(ang), jnp.sin(ang)
        return (jnp.tile(jnp.concatenate([cos, cos], axis=1), (1, reps)),
                jnp.tile(jnp.concatenate([-sin, sin], axis=1), (1, reps)))

    return tables(ATT_HEAD_DIM, 1) + tables(IDX_DIM, 2)


def _pick_tile(n, pref):
    t = min(pref, n)
    while n % t:
        t //= 2
    return t


def kernel(x, mem, positions, norm_mix_pre, w_in, conv_w, a_log, dt_bias, gdn_norm, w_out,
           norm_mix_post, norm_x_pre, norm_mem, w_xq, w_xkv, w_xo, norm_x_post,
           norm_ffn_pre, w_gu, w_down, norm_ffn_post):
    b, s, d = x.shape
    n = b * s
    n_mem = mem.shape[1]
    depth = w_in.shape[0]
    topk = min(TOPK_MAX, s // 4)
    tm = _pick_tile(s, 512)
    tq = _pick_tile(s, 256)
    kb = _pick_tile(s, 512)
    tb = _pick_tile(s, 256)

    ca, sa, ci, si = (t.reshape(b, s, 128) for t in _rope_tables(positions))

    row1 = lambda v: v.reshape(1, -1).astype(F32)

    x2 = x.reshape(n, d)
    for l in range(depth):
        w_att, w_gdn, w_gate = _prep_w_in(w_in, l)

        g_pre = row1(norm_mix_pre[l])
        x3 = x2.reshape(b, s, d)
        q, kt, v, qi, kit, wi = _proj_att(x3, g_pre, w_att, ca, sa, ci, si, tm)
        ya = _dsa(q, qi, wi, kt, v, kit, topk, tq, kb)

        qkv, ab, abt = _proj_gdn(x3, g_pre, w_gdn, tm)
        pad_lanes = lambda vec: jnp.pad(vec.astype(F32), (0, 128 - GDN_HEADS)).reshape(1, 128)
        bcast_rows = lambda vec: jnp.broadcast_to(vec.astype(F32)[:, None], (GDN_HEADS, tb))
        yb = _gdn(qkv, ab, abt, conv_w[l].astype(F32),
                  pad_lanes(a_log[l]), pad_lanes(dt_bias[l]), bcast_rows(a_log[l]), bcast_rows(dt_bias[l]),
                  row1(gdn_norm[l]), tb)

        x2 = _merge(x2, ya.reshape(n, 1024), yb.reshape(n, 1024), g_pre, w_gate,
                    w_out[l].astype(BF16), row1(norm_mix_post[l]), tm)

        kv = _mem_kv(mem.reshape(b * n_mem, d), row1(norm_mem[l]), w_xkv[l].astype(BF16))
        kv = kv.reshape(b, n_mem, 2 * d)
        kt_mem = jnp.swapaxes(kv[:, :, :d], 1, 2)
        x2 = _xattn(x2.reshape(b, s, d), kt_mem, kv[:, :, d:], row1(norm_x_pre[l]),
                    w_xq[l].astype(BF16), w_xo[l].astype(BF16), row1(norm_x_post[l]), tm).reshape(n, d)

        x2 = _ffn(x2, row1(norm_ffn_pre[l]), w_gu[l].astype(BF16), w_down[l].astype(BF16),
                  row1(norm_ffn_post[l]), tm)
    return x2.reshape(b, s, d)
```

```python
import functools

import jax
import jax.numpy as jnp
import numpy as np
from jax import lax
from jax.experimental import pallas as pl
from jax.experimental.pallas import tpu as pltpu

EPS = 1e-6
ROPE_THETA = 10000.0
ATT_HEADS = 8
ATT_KV_HEADS = 2
ATT_HEAD_DIM = 128
IDX_HEADS = 4
IDX_DIM = 64
TOPK_MAX = 256
GDN_HEADS = 8
GDN_DK = 128
GDN_DV = 128
CONV_K = 4
XATT_HEADS = 4

LANES = 128
VMEM_LIMIT = 56 * 1024 * 1024

F32 = jnp.float32
BF16 = jnp.bfloat16
I32 = jnp.int32

NEG_BIG = -1e30
LOG2E = 1.4426950408889634
INT_MAX = 2147483647
NEG_INF_KEY = -2139095041


def _cparams(sem):
    return pltpu.CompilerParams(dimension_semantics=sem, vmem_limit_bytes=VMEM_LIMIT)


def _const_spec(shape):
    nd = len(shape)
    return pl.BlockSpec(shape, lambda *_: (0,) * nd, pipeline_mode=pl.Buffered(1))


def _rms(x, g):
    return x * lax.rsqrt(jnp.mean(x * x, axis=-1, keepdims=True) + EPS) * g


def _dot(a, b):
    return jnp.dot(a, b, preferred_element_type=F32)


def _dot_nt(a, b):
    return lax.dot_general(a, b, (((1,), (1,)), ((), ())), preferred_element_type=F32)


def _split3(a):
    a1 = a.astype(BF16)
    r1 = a - a1.astype(F32)
    a2 = r1.astype(BF16)
    a3 = (r1 - a2.astype(F32)).astype(BF16)
    return a1, a2, a3


A_Q, A_KV, I_Q, I_K, I_W, B_QKV = 1024, 256, 256, 64, 4, 3072
O_IQ = A_Q + 2 * A_KV
O_IK = O_IQ + I_Q
O_IW = O_IK + I_K
O_B = O_IW + I_W
O_BA = O_B + B_QKV
O_GA = O_BA + 2 * GDN_HEADS
IN_COLS = O_GA + 2 * 1024


def _prep_w_in_kernel(w_ref, att_ref, gdn_ref, gate_ref):
    w = w_ref[...]
    zeros = lambda c: jnp.zeros((w.shape[0], c), F32)
    att = [w[:, :O_IQ]]
    for h in range(IDX_HEADS):
        att += [w[:, O_IQ + h * IDX_DIM:O_IQ + (h + 1) * IDX_DIM], zeros(128 - IDX_DIM)]
    att += [w[:, O_IK:O_IW], zeros(128 - I_K), w[:, O_IW:O_B], zeros(128 - I_W)]
    att_ref[...] = jnp.concatenate(att, axis=1).astype(BF16)
    gdn_ref[...] = jnp.concatenate([w[:, O_B:O_GA], zeros(128 - 2 * GDN_HEADS)], axis=1).astype(BF16)
    gate_ref[...] = w[:, O_GA:].astype(BF16)


def _prep_w_in(w_in, layer):
    _, d, cols = w_in.shape
    assert cols == IN_COLS
    tr = 128
    widths = (O_IQ + IDX_HEADS * 128 + 256, B_QKV + 128, 2048)
    return pl.pallas_call(
        _prep_w_in_kernel,
        grid=(d // tr,),
        in_specs=[pl.BlockSpec((None, tr, cols), lambda i: (layer, i, 0))],
        out_specs=[pl.BlockSpec((tr, c), lambda i: (i, 0)) for c in widths],
        out_shape=[jax.ShapeDtypeStruct((d, c), BF16) for c in widths],
        compiler_params=_cparams(("parallel",)),
        name="prep_w_in",
    )(w_in)


def _rope128(x, cos, sin_signed):
    return x * cos + pltpu.roll(x, 64, 1) * sin_signed


def _rope64(x, cos, sin_signed, first_half):
    partner = jnp.where(first_half, pltpu.roll(x, 96, 1), pltpu.roll(x, 32, 1))
    return x * cos + partner * sin_signed


def _proj_att_kernel(x_ref, g_ref, w_ref, ca_ref, sa_ref, ci_ref, si_ref,
                     q_ref, kt_ref, v_ref, qi_ref, kit_ref, wi_ref):
    h = _rms(x_ref[...], g_ref[...]).astype(BF16)
    p = _dot(h, w_ref[...])
    ca, sa = ca_ref[...], sa_ref[...]
    ci, si = ci_ref[...], si_ref[...]
    q_scale = ATT_HEAD_DIM ** -0.5 * LOG2E
    for hd in range(ATT_HEADS):
        sl = slice(hd * 128, (hd + 1) * 128)
        q_ref[:, sl] = (_rope128(p[:, sl], ca, sa) * q_scale).astype(BF16)
    for g in range(ATT_KV_HEADS):
        k_g = _rope128(p[:, A_Q + g * 128:A_Q + (g + 1) * 128], ca, sa)
        kt_ref[g * 128:(g + 1) * 128, :] = jnp.transpose(k_g).astype(BF16)
    v_ref[...] = p[:, A_Q + A_KV:O_IQ].astype(BF16)
    lane = lax.broadcasted_iota(I32, ci.shape, 1)
    first_half = (lane & 63) < 32
    idx_scale = IDX_DIM ** -0.5
    for hd in range(IDX_HEADS):
        off = O_IQ + hd * 128
        qi_ref[:, hd * 128:(hd + 1) * 128] = (
            _rope64(p[:, off:off + 128], ci, si, first_half) * idx_scale).astype(BF16)
    o_ki = O_IQ + IDX_HEADS * 128
    kit_ref[...] = jnp.transpose(_rope64(p[:, o_ki:o_ki + 128], ci, si, first_half)).astype(BF16)
    wi_ref[...] = p[:, o_ki + 128:o_ki + 256] * (IDX_HEADS ** -0.5)


def _proj_att(x, g, w, ca, sa, ci, si, tm):
    b, s, d = x.shape
    row = lambda c: pl.BlockSpec((None, tm, c), lambda bi, i: (bi, i, 0))
    col = lambda r: pl.BlockSpec((None, r, tm), lambda bi, i: (bi, 0, i))
    sds = jax.ShapeDtypeStruct
    return pl.pallas_call(
        _proj_att_kernel,
        grid=(b, s // tm),
        in_specs=[row(d), _const_spec((1, d)), _const_spec(w.shape),
                  row(128), row(128), row(128), row(128)],
        out_specs=[row(1024), col(256), row(256), row(512), col(128), row(128)],
        out_shape=[sds((b, s, 1024), BF16), sds((b, 256, s), BF16), sds((b, s, 256), BF16),
                   sds((b, s, 512), BF16), sds((b, 128, s), BF16), sds((b, s, 128), F32)],
        compiler_params=_cparams(("parallel", "parallel")),
        name="proj_att",
    )(x, g, w, ca, sa, ci, si)


def _proj_gdn_kernel(x_ref, g_ref, w_ref, qkv_ref, ab_ref, abt_ref):
    h = _rms(x_ref[...], g_ref[...]).astype(BF16)
    p = _dot(h, w_ref[...])
    qkv_ref[...] = p[:, :B_QKV]
    ab = p[:, B_QKV:B_QKV + 128]
    ab_ref[...] = ab
    abt_ref[...] = jnp.transpose(ab)[:16, :]


def _proj_gdn(x, g, w, tm):
    b, s, d = x.shape
    row = lambda c: pl.BlockSpec((None, tm, c), lambda bi, i: (bi, i, 0))
    sds = jax.ShapeDtypeStruct
    return pl.pallas_call(
        _proj_gdn_kernel,
        grid=(b, s // tm),
        in_specs=[row(d), _const_spec((1, d)), _const_spec(w.shape)],
        out_specs=[row(3072), row(128), pl.BlockSpec((None, 16, tm), lambda bi, i: (bi, 0, i))],
        out_shape=[sds((b, s, 3072), F32), sds((b, s, 128), F32), sds((b, 16, s), F32)],
        compiler_params=_cparams(("parallel", "parallel")),
        name="proj_gdn",
    )(x, g, w)


def _key_of(v):
    b = pltpu.bitcast(v, I32)
    return jnp.where(b < 0, b ^ 0x7FFFFFFF, b)


def _val_of(k):
    return pltpu.bitcast(jnp.where(k < 0, k ^ 0x7FFFFFFF, k), F32)


def _dsa_kernel(q_ref, qi_ref, wi_ref, kt_ref, v_ref, kit_ref, o_ref,
                keys_ref, qis_ref, qg_ref, bias_ref, s_ref, p_ref, acc_ref, m_ref, l_ref, run_ref, tri_ref,
                *, tq, kb, topk, seq):
    t0 = pl.program_id(1) * tq
    nkb = (t0 + tq + kb - 1) // kb
    n_lane_tiles = kb // LANES
    row_ids = t0 + lax.broadcasted_iota(I32, (tq, kb), 0)
    col_iota = lax.broadcasted_iota(I32, (tq, kb), 1)
    kf = float(topk)

    for h in range(IDX_HEADS):
        qis_ref[h * tq:(h + 1) * tq, :] = qi_ref[:, h * 128:(h + 1) * 128]
    wi = wi_ref[...]
    w_heads = [wi[:, h:h + 1] for h in range(IDX_HEADS)]

    def score_block(j, carry, masked):
        mx, mn = carry
        c0 = pl.multiple_of(j * kb, kb)
        halves = []
        for a0 in range(0, kb, 256):
            lg = _dot(qis_ref[...], kit_ref[:, pl.ds(c0 + a0, 256)])
            part = jnp.zeros((tq, 256), F32)
            for h in range(IDX_HEADS):
                part = part + w_heads[h] * jnp.maximum(lg[h * tq:(h + 1) * tq], 0.0)
            halves.append(part)
        sc = jnp.concatenate(halves, axis=1)
        sc = jnp.where(sc == 0.0, 0.0, sc)
        for i in range(n_lane_tiles):
            mx = jnp.maximum(mx, sc[:, i * LANES:(i + 1) * LANES])
            mn = jnp.minimum(mn, sc[:, i * LANES:(i + 1) * LANES])
        if masked:
            sc = jnp.where(c0 + col_iota <= row_ids, sc, -jnp.inf)
        keys_ref[:, pl.ds(c0, kb)] = _key_of(sc)
        return mx, mn

    n_full = t0 // kb
    ext0 = (jnp.full((tq, LANES), -jnp.inf, F32), jnp.full((tq, LANES), jnp.inf, F32))
    ext1 = lax.fori_loop(0, n_full, functools.partial(score_block, masked=False), ext0)
    mx, mn = lax.fori_loop(n_full, nkb, functools.partial(score_block, masked=True), ext1)
    row_max = jnp.max(mx, axis=1, keepdims=True)
    row_min = jnp.min(mn, axis=1, keepdims=True)

    n_row_tiles = tq // LANES if tq % LANES == 0 else 1
    rt = tq // n_row_tiles

    def as_col(row):
        return jnp.transpose(row)[:, :1]

    def count_ge(cands):
        nc = len(cands)
        tiles = range(n_row_tiles)
        cb = [[c if isinstance(c, int) else
               jnp.transpose(jnp.broadcast_to(c[:1, t * rt:(t + 1) * rt], (LANES, rt)))
               for t in tiles] for c in cands]

        def tile_counts(t):
            def body(j, accs):
                c0 = pl.multiple_of(j * kb, kb)
                accs = list(accs)
                blk = keys_ref[t * rt:(t + 1) * rt, pl.ds(c0, kb)]
                for i in range(n_lane_tiles):
                    for k in range(nc):
                        accs[k] = accs[k] + jnp.where(blk[:, i * LANES:(i + 1) * LANES] >= cb[k][t], 1.0, 0.0)
                return tuple(accs)

            z = jnp.zeros((rt, LANES), F32)
            return lax.fori_loop(0, nkb, body, (z,) * nc)

        accs = [tile_counts(t) for t in tiles]
        row_sums = lambda a: jnp.broadcast_to(jnp.sum(jnp.transpose(a), axis=0, keepdims=True), (8, rt))
        return [jnp.concatenate([row_sums(accs[t][k]) for t in tiles], axis=1) for k in range(nc)]

    as_row = lambda col: jnp.transpose(jnp.broadcast_to(col, (tq, 8)))
    n_visible = (t0 + lax.broadcasted_iota(I32, (8, tq), 1) + 1).astype(F32)
    few = n_visible < kf
    f_pos, f_zero = count_ge([1, 0])
    above = f_pos >= kf
    below = f_zero < kf
    at_zero = jnp.logical_and(jnp.logical_not(above), jnp.logical_not(below))
    key_min = _key_of(as_row(row_min))
    key_max = _key_of(as_row(row_max))
    lo0 = jnp.where(few, NEG_INF_KEY + 1, jnp.where(above, 1, jnp.where(below, key_min, 0)))
    flo0 = jnp.where(few, n_visible, jnp.where(above, f_pos, jnp.where(below, n_visible, f_zero)))
    hi0 = jnp.where(above, jnp.minimum(key_max, 2147483646) + 1, jnp.where(below, 0, 1))
    fhi0 = jnp.where(above, 0.0, jnp.where(below, f_zero, f_pos))
    log_target = float(np.log(topk + 0.5))
    glog = lambda c: jnp.log(jnp.maximum(c, 0.5)) - log_target
    done0 = jnp.where(jnp.logical_or(jnp.logical_or(few, at_zero), flo0 == kf), 1.0, 0.0)
    n_interp = 24

    def peel_min(base, flip):
        tiles = range(n_row_tiles)
        bb = [jnp.broadcast_to(base[t * rt:(t + 1) * rt], (rt, LANES)) for t in tiles]
        fb = [jnp.broadcast_to(flip[t * rt:(t + 1) * rt], (rt, LANES)) for t in tiles]

        def tile_min(t):
            def body(j, acc):
                c0 = pl.multiple_of(j * kb, kb)
                blk = keys_ref[t * rt:(t + 1) * rt, pl.ds(c0, kb)]
                for i in range(n_lane_tiles):
                    x = blk[:, i * LANES:(i + 1) * LANES] ^ fb[t]
                    acc = jnp.minimum(acc, jnp.where(x >= bb[t], x, INT_MAX))
                return acc

            acc = lax.fori_loop(0, nkb, body, jnp.full((rt, LANES), INT_MAX, I32))
            return jnp.min(acc, axis=1, keepdims=True)

        col = jnp.concatenate([tile_min(t) for t in tiles], axis=0)
        return jnp.transpose(jnp.broadcast_to(col, (tq, 8)))

    def probe(st, forced=None, use_forced=None):
        it, _, _, lo, hi, flo, fhi, glo, ghi, side, done = st
        v_lo, v_hi = _val_of(lo), _val_of(hi)
        v_model = (v_lo * ghi - v_hi * glo) / (ghi - glo)
        inside = flo - fhi
        v_even = v_lo + (v_hi - v_lo) * ((flo - kf + 0.5) / (inside + 1.0))
        c_interp = _key_of(jnp.where(inside <= 64.0, v_even, v_model))
        c_mid = jnp.right_shift(lo, 1) + jnp.right_shift(hi, 1) + (lo & hi & 1)
        cand = jnp.where(it >= n_interp, c_mid, c_interp)
        if forced is not None:
            cand = jnp.where(use_forced, forced, cand)
        cand = jnp.minimum(jnp.maximum(cand, lo + 1), hi - 1)
        c = count_ge([cand])[0]
        active = done < 0.5
        to_lo = jnp.logical_and(active, c >= kf)
        to_hi = jnp.logical_and(active, c < kf)
        gc = glog(c)
        ghi_n = jnp.where(to_hi, gc, jnp.where(jnp.logical_and(to_lo, side > 0.0), ghi * 0.5, ghi))
        glo_n = jnp.where(to_lo, gc, jnp.where(jnp.logical_and(to_hi, side < 0.0), glo * 0.5, glo))
        side = jnp.where(to_lo, 1.0, jnp.where(to_hi, -1.0, side))
        lo = jnp.where(to_lo, cand, lo)
        flo = jnp.where(to_lo, c, flo)
        hi = jnp.where(to_hi, cand, hi)
        fhi = jnp.where(to_hi, c, fhi)
        fin = jnp.logical_or(flo == kf, hi - 1 == lo)
        done = jnp.where(fin, 1.0, done)
        near = jnp.logical_or(flo - kf <= 1.0, kf - fhi <= 1.0)
        n_far = jnp.sum(jnp.where(near, 0.0, 1.0 - done))
        return it + 1, n_far, jnp.sum(1.0 - done), lo, hi, flo, fhi, glo_n, ghi_n, side, done

    n_pass_cap = n_interp + 34
    near0 = jnp.logical_or(flo0 - kf <= 1.0, kf - fhi0 <= 1.0)
    st = (jnp.int32(0), jnp.sum(jnp.where(near0, 0.0, 1.0 - done0)), jnp.sum(1.0 - done0),
          lo0, hi0, flo0, fhi0, glog(flo0), glog(fhi0), jnp.zeros((8, tq), F32), done0)
    st = lax.while_loop(lambda st: jnp.logical_and(st[0] < n_pass_cap, st[1] > 0.0), probe, st)
    lo, hi, flo, fhi, done = st[3], st[4], st[5], st[6], st[10]
    active = done < 0.5
    drop_one = jnp.logical_and(active, flo - kf <= 1.0)
    add_one = jnp.logical_and(jnp.logical_and(active, jnp.logical_not(drop_one)), kf - fhi <= 1.0)
    flip = jnp.where(add_one, -1, 0)
    edge = peel_min(as_col(jnp.where(add_one, -hi, lo)), as_col(flip))
    forced = jnp.where(add_one, edge ^ flip, edge + 1)
    st = probe(st, forced, jnp.logical_or(drop_one, add_one))
    st = lax.while_loop(lambda st: jnp.logical_and(st[0] < n_pass_cap + 2, st[2] > 0.0), probe, st)
    thr = as_col(st[3])
    tie_rows = jnp.logical_and(st[5] > kf, jnp.logical_not(few))
    need = as_col(jnp.where(tie_rows, kf - st[6], float(2 * seq)))

    run_ref[...] = jnp.zeros(run_ref.shape, F32)
    rr = lax.broadcasted_iota(I32, (LANES, 2 * LANES), 0)
    cc = lax.broadcasted_iota(I32, (LANES, 2 * LANES), 1)
    tri_ref[...] = jnp.where(jnp.logical_or(rr <= cc, cc >= LANES), 1.0, 0.0).astype(BF16)

    rep = ATT_HEADS // ATT_KV_HEADS
    rows = rep * tq
    rc = 32
    for g in range(ATT_KV_HEADS):
        for r in range(rep):
            hd = g * rep + r
            qg_ref[g, r * tq:(r + 1) * tq, :] = q_ref[:, hd * 128:(hd + 1) * 128]
    acc_ref[...] = jnp.zeros(acc_ref.shape, F32)
    l_ref[...] = jnp.zeros(l_ref.shape, F32)
    m_ref[...] = jnp.full(m_ref.shape, NEG_BIG, F32)

    def attend_block(j, carry):
        c0 = pl.multiple_of(j * kb, kb)
        left = jnp.broadcast_to(need - run_ref[...], (tq, LANES))
        for i in range(n_lane_tiles):
            keys = keys_ref[:, pl.ds(c0 + i * LANES, LANES)]
            tied = keys == thr
            cnt = _dot(jnp.where(tied, 1.0, 0.0).astype(BF16), tri_ref[...])
            ok = jnp.logical_and(tied, cnt[:, :LANES] <= left)
            bias_ref[:, i * LANES:(i + 1) * LANES] = jnp.where(keys > thr, 0.0, jnp.where(ok, 0.0, NEG_BIG))
            left = left - cnt[:, LANES:]
        run_ref[...] = need - left[:, :1]
        for g in range(ATT_KV_HEADS):
            for a0 in range(0, kb, 256):
                s_ref[g, :, a0:a0 + 256] = _dot(qg_ref[g], kt_ref[g * 128:(g + 1) * 128, pl.ds(c0 + a0, 256)])
        for g in range(ATT_KV_HEADS):
            for r0 in range(0, rows, rc):
                rs = slice(r0, r0 + rc)
                b0 = r0 % tq
                s = s_ref[g, rs, :] + bias_ref[b0:b0 + rc, :]
                m_prev = m_ref[g, rs, :]
                m_next = jnp.maximum(m_prev, jnp.max(s, axis=1, keepdims=True))
                p = jnp.exp2(s - jnp.concatenate([m_next] * n_lane_tiles, axis=1))
                alpha = jnp.exp2(m_prev - m_next)
                l_ref[g, rs, :] = alpha * l_ref[g, rs, :] + jnp.sum(p, axis=1, keepdims=True)
                m_ref[g, rs, :] = m_next
                acc_ref[g, rs, :] = acc_ref[g, rs, :] * alpha
                p_ref[g, rs, :] = p.astype(BF16)
            acc_ref[g] += _dot(p_ref[g], v_ref[pl.ds(c0, kb), g * 128:(g + 1) * 128])
        return carry

    lax.fori_loop(0, nkb, attend_block, 0)

    for g in range(ATT_KV_HEADS):
        out = acc_ref[g] / l_ref[g]
        for r in range(rep):
            hd = g * rep + r
            o_ref[:, hd * 128:(hd + 1) * 128] = out[r * tq:(r + 1) * tq]


def _dsa(q, qi, wi, kt, v, kit, topk, tq, kb):
    b, s, _ = q.shape
    rep = ATT_HEADS // ATT_KV_HEADS
    qspec = lambda c: pl.BlockSpec((None, tq, c), lambda bi, i: (bi, i, 0))
    per_batch = lambda r, c: pl.BlockSpec((None, r, c), lambda bi, i: (bi, 0, 0),
                                          pipeline_mode=pl.Buffered(1))
    return pl.pallas_call(
        functools.partial(_dsa_kernel, tq=tq, kb=kb, topk=topk, seq=s),
        grid=(b, s // tq),
        in_specs=[qspec(1024), qspec(512), qspec(128),
                  per_batch(256, s), per_batch(s, 256), per_batch(128, s)],
        out_specs=qspec(1024),
        out_shape=jax.ShapeDtypeStruct((b, s, 1024), F32),
        scratch_shapes=[pltpu.VMEM((tq, s), I32),
                        pltpu.VMEM((IDX_HEADS * tq, 128), BF16),
                        pltpu.VMEM((ATT_KV_HEADS, rep * tq, 128), BF16),
                        pltpu.VMEM((tq, kb), F32),
                        pltpu.VMEM((ATT_KV_HEADS, rep * tq, kb), F32),
                        pltpu.VMEM((ATT_KV_HEADS, rep * tq, kb), BF16),
                        pltpu.VMEM((ATT_KV_HEADS, rep * tq, 128), F32),
                        pltpu.VMEM((ATT_KV_HEADS, rep * tq, 128), F32),
                        pltpu.VMEM((ATT_KV_HEADS, rep * tq, 128), F32),
                        pltpu.VMEM((tq, 1), F32),
                        pltpu.VMEM((LANES, 2 * LANES), BF16)],
        compiler_params=_cparams(("parallel", "arbitrary")),
        name="dsa_attention",
    )(q, qi, wi, kt, v, kit)


def _softplus(x):
    return jnp.maximum(x, 0.0) + jnp.log(1.0 + jnp.exp(-jnp.abs(x)))


def _gdn_kernel(qkv_ref, ab_ref, abt_ref, cw_ref, ac_ref, dc_ref, ar_ref, dr_ref, gn_ref,
                y_ref, xbuf_ref, act_ref, state_ref, *, tb):
    ck = tb
    n_chunks = tb // ck

    @pl.when(pl.program_id(1) == 0)
    def _():
        xbuf_ref[0:8, :] = jnp.zeros((8, xbuf_ref.shape[1]), F32)
        state_ref[...] = jnp.zeros(state_ref.shape, F32)

    xbuf_ref[8:8 + tb, :] = qkv_ref[...]
    for c in range(xbuf_ref.shape[1] // LANES):
        sl = slice(c * LANES, (c + 1) * LANES)
        y = xbuf_ref[5:5 + tb, sl] * cw_ref[0:1, sl]
        for j in range(1, CONV_K):
            y = y + xbuf_ref[5 + j:5 + j + tb, sl] * cw_ref[j:j + 1, sl]
        act_ref[:, sl] = y * (1.0 / (1.0 + jnp.exp(-y)))
    xbuf_ref[0:8, :] = xbuf_ref[tb:tb + 8, :]

    ab = ab_ref[...]
    g_col = -jnp.exp(ac_ref[...]) * _softplus(ab + dc_ref[...])
    beta_col = 1.0 / (1.0 + jnp.exp(-ab))
    g_row = -jnp.exp(ar_ref[...]) * _softplus(abt_ref[0:8, :] + dr_ref[...])

    r = lax.broadcasted_iota(I32, (tb, tb), 0)
    c = lax.broadcasted_iota(I32, (tb, tb), 1)
    chunk_shift = ck.bit_length() - 1
    same = jnp.right_shift(r, chunk_shift) == jnp.right_shift(c, chunk_shift)
    lower = jnp.logical_and(same, r >= c)
    strict = jnp.logical_and(same, r > c)
    lower_m = jnp.where(lower, 1.0, 0.0).astype(BF16)
    upper_m = jnp.where(jnp.logical_and(same, r <= c), 1.0, 0.0).astype(BF16)
    same_m = jnp.where(same, 1.0, 0.0).astype(BF16)
    eye = jnp.where(r == c, 1.0, 0.0)
    n_levels = ck.bit_length() - 1
    lvl_subs = []
    for lvl in range(n_levels):
        lvl_subs.append(jnp.logical_and(jnp.right_shift(r, lvl + 1) == jnp.right_shift(c, lvl + 1),
                                        jnp.logical_and((jnp.right_shift(r, lvl) & 1) == 1,
                                                        (jnp.right_shift(c, lvl) & 1) == 0)))
    lvl_masks = [jnp.where(sub, 1.0, 0.0).astype(BF16) for sub in lvl_subs]

    gparts = _split3(g_col)
    gc_col = _dot(lower_m, gparts[0]) + (_dot(lower_m, gparts[1]) + _dot(lower_m, gparts[2]))
    gl_col = _dot(same_m, gparts[0]) + (_dot(same_m, gparts[1]) + _dot(same_m, gparts[2]))
    rparts = _split3(g_row)
    gc_row = _dot(rparts[0], upper_m) + (_dot(rparts[1], upper_m) + _dot(rparts[2], upper_m))

    gn = gn_ref[...]
    hg = 4
    for h0 in range(0, GDN_HEADS, hg):
        heads = range(h0, h0 + hg)
        q_l, k_l, kbeta_l, vbeta_l, decay_l, gc_l, gl_l, m16_l, d_l = ([] for _ in range(9))
        for h in heads:
            q = act_ref[:, h * 128:(h + 1) * 128]
            k = act_ref[:, 1024 + h * 128:1024 + (h + 1) * 128]
            v = act_ref[:, 2048 + h * 128:2048 + (h + 1) * 128]
            q = q * lax.rsqrt(jnp.sum(q * q, axis=-1, keepdims=True) + EPS) * (GDN_DK ** -0.5)
            k = k * lax.rsqrt(jnp.sum(k * k, axis=-1, keepdims=True) + EPS)
            gc = gc_col[:, h:h + 1]
            beta = beta_col[:, 8 + h:9 + h]
            diff = gc - gc_row[h:h + 1, :]
            decay = jnp.where(lower, jnp.exp(jnp.where(lower, diff, 0.0)), 0.0)
            kbeta = k * beta
            nm = jnp.where(strict, -(_dot_nt(kbeta.astype(BF16), k.astype(BF16)) * decay), 0.0)
            q_l.append(q); k_l.append(k); kbeta_l.append(kbeta); vbeta_l.append(v * beta)
            decay_l.append(decay); gc_l.append(gc); gl_l.append(gl_col[:, h:h + 1])
            m16_l.append(nm.astype(BF16)); d_l.append(eye + jnp.where(lvl_subs[0], nm, 0.0))
        for lvl in range(1, n_levels):
            for i in range(hg):
                dh = d_l[i].astype(BF16)
                lh = m16_l[i] * lvl_masks[lvl]
                th = _dot(dh, lh).astype(BF16)
                d_l[i] = d_l[i] + _dot(th, dh)
        u_l, w_l, attn_l, qd_l, kd_l = [], [], [], [], []
        for i in range(hg):
            egc = jnp.exp(gc_l[i])
            rhs = jnp.concatenate([vbeta_l[i], kbeta_l[i] * egc], axis=1)
            sol = _dot(d_l[i].astype(BF16), rhs.astype(BF16))
            u_l.append(sol[:, :128])
            w_l.append(sol[:, 128:].astype(BF16))
            k16 = k_l[i].astype(BF16)
            attn_l.append(jnp.where(lower, _dot_nt(q_l[i].astype(BF16), k16) * decay_l[i], 0.0).astype(BF16))
            qd_l.append((q_l[i] * egc).astype(BF16))
            kd_l.append(k_l[i] * jnp.exp(gl_l[i] - gc_l[i]))
        st_l = [state_ref[h] for h in heads]
        outs = [[] for _ in range(hg)]
        for ci in range(n_chunks):
            rc = slice(ci * ck, (ci + 1) * ck)
            for i in range(hg):
                st16 = st_l[i].astype(BF16)
                v_new = u_l[i][rc] - _dot(w_l[i][rc], st16)
                v_new16 = v_new.astype(BF16)
                outs[i].append(_dot(qd_l[i][rc], st16) + _dot(attn_l[i][rc, rc], v_new16))
                kd_t = jnp.transpose(kd_l[i][rc]).astype(BF16)
                st_l[i] = (st_l[i] * jnp.exp(gl_l[i][ci * ck:ci * ck + 1])
                           + _dot(kd_t, v_new16))
        for i, h in enumerate(heads):
            state_ref[h] = st_l[i]
            y_ref[:, h * 128:(h + 1) * 128] = _rms(jnp.concatenate(outs[i], axis=0), gn)


def _gdn(qkv, ab, abt, conv_w, ac, dc, ar, dr, gn, tb):
    b, s, _ = qkv.shape
    blk = lambda c: pl.BlockSpec((None, tb, c), lambda bi, i: (bi, i, 0))
    return pl.pallas_call(
        functools.partial(_gdn_kernel, tb=tb),
        grid=(b, s // tb),
        in_specs=[blk(3072), blk(128), pl.BlockSpec((None, 16, tb), lambda bi, i: (bi, 0, i)),
                  _const_spec(conv_w.shape), _const_spec(ac.shape), _const_spec(dc.shape),
                  _const_spec(ar.shape), _const_spec(dr.shape), _const_spec(gn.shape)],
        out_specs=blk(1024),
        out_shape=jax.ShapeDtypeStruct((b, s, 1024), F32),
        scratch_shapes=[pltpu.VMEM((tb + 8, 3072), F32),
                        pltpu.VMEM((tb, 3072), F32),
                        pltpu.VMEM((GDN_HEADS, GDN_DK, GDN_DV), F32)],
        compiler_params=_cparams(("parallel", "arbitrary")),
        name="gated_delta",
    )(qkv, ab, abt, conv_w, ac, dc, ar, dr, gn)


def _sigmoid(x):
    return 1.0 / (1.0 + jnp.exp(-x))


def _merge_kernel(x_ref, ya_ref, yb_ref, gpre_ref, wg_ref, wo_ref, gpost_ref, o_ref):
    x = x_ref[...]
    h = _rms(x, gpre_ref[...]).astype(BF16)
    gates = _dot(h, wg_ref[...])
    y = _sigmoid(gates[:, :1024]) * ya_ref[...] + _sigmoid(gates[:, 1024:]) * yb_ref[...]
    z = _dot(y.astype(BF16), wo_ref[...])
    o_ref[...] = x + _rms(z, gpost_ref[...])


def _merge(x2, ya, yb, gpre, wg, wo, gpost, tm):
    n, d = x2.shape
    row = pl.BlockSpec((tm, d), lambda i: (i, 0))
    return pl.pallas_call(
        _merge_kernel,
        grid=(n // tm,),
        in_specs=[row, row, row, _const_spec((1, d)), _const_spec(wg.shape),
                  _const_spec(wo.shape), _const_spec((1, d))],
        out_specs=row,
        out_shape=jax.ShapeDtypeStruct((n, d), F32),
        compiler_params=_cparams(("parallel",)),
        name="merge_out",
    )(x2, ya, yb, gpre, wg, wo, gpost)


def _mem_kv_kernel(m_ref, g_ref, w_ref, kv_ref):
    h = _rms(m_ref[...], g_ref[...]).astype(BF16)
    kv_ref[...] = _dot(h, w_ref[...]).astype(BF16)


def _mem_kv(mem2, g, w):
    n, d = mem2.shape
    return pl.pallas_call(
        _mem_kv_kernel,
        out_shape=jax.ShapeDtypeStruct((n, w.shape[1]), BF16),
        compiler_params=pltpu.CompilerParams(vmem_limit_bytes=VMEM_LIMIT),
        name="mem_kv",
    )(mem2, g, w)


def _xattn_kernel(x_ref, kt_ref, v_ref, gpre_ref, wq_ref, wo_ref, gpost_ref, o_ref):
    x = x_ref[...]
    d = x.shape[1]
    hd = d // XATT_HEADS
    h = _rms(x, gpre_ref[...]).astype(BF16)
    q = (_dot(h, wq_ref[...]) * (hd ** -0.5)).astype(BF16)
    heads = []
    for i in range(XATT_HEADS):
        sl = slice(i * hd, (i + 1) * hd)
        s = _dot(q[:, sl], kt_ref[sl, :])
        p = jnp.exp(s - jnp.max(s, axis=1, keepdims=True))
        o = _dot(p.astype(BF16), v_ref[:, sl]) / jnp.sum(p, axis=1, keepdims=True)
        heads.append(o.astype(BF16))
    z = _dot(jnp.concatenate(heads, axis=1), wo_ref[...])
    o_ref[...] = x + _rms(z, gpost_ref[...])


def _xattn(x, kt, v, gpre, wq, wo, gpost, tm):
    b, s, d = x.shape
    n_mem = v.shape[1]
    row = pl.BlockSpec((None, tm, d), lambda bi, i: (bi, i, 0))
    return pl.pallas_call(
        _xattn_kernel,
        grid=(b, s // tm),
        in_specs=[row,
                  pl.BlockSpec((None, d, n_mem), lambda bi, i: (bi, 0, 0)),
                  pl.BlockSpec((None, n_mem, d), lambda bi, i: (bi, 0, 0)),
                  _const_spec((1, d)), _const_spec(wq.shape), _const_spec(wo.shape),
                  _const_spec((1, d))],
        out_specs=row,
        out_shape=jax.ShapeDtypeStruct((b, s, d), F32),
        compiler_params=_cparams(("parallel", "parallel")),
        name="mem_xattn",
    )(x, kt, v, gpre, wq, wo, gpost)


def _ffn_kernel(x_ref, gpre_ref, wgu_ref, wd_ref, gpost_ref, o_ref, *, ff, fc):
    x = x_ref[...]
    h = _rms(x, gpre_ref[...]).astype(BF16)
    z = jnp.zeros(x.shape, F32)
    for c0 in range(0, ff, fc):
        gate = _dot(h, wgu_ref[:, c0:c0 + fc])
        up = _dot(h, wgu_ref[:, ff + c0:ff + c0 + fc])
        act = (gate * _sigmoid(gate) * up).astype(BF16)
        z = z + _dot(act, wd_ref[c0:c0 + fc, :])
    o_ref[...] = x + _rms(z, gpost_ref[...])


def _ffn(x2, gpre, wgu, wd, gpost, tm):
    n, d = x2.shape
    ff = wd.shape[0]
    fc = ff // 2 if (ff // 2) % LANES == 0 else ff
    row = pl.BlockSpec((tm, d), lambda i: (i, 0))
    return pl.pallas_call(
        functools.partial(_ffn_kernel, ff=ff, fc=fc),
        grid=(n // tm,),
        in_specs=[row, _const_spec((1, d)), _const_spec(wgu.shape), _const_spec(wd.shape),
                  _const_spec((1, d))],
        out_specs=row,
        out_shape=jax.ShapeDtypeStruct((n, d), F32),
        compiler_params=_cparams(("parallel",)),
        name="swiglu",
    )(x2, gpre, wgu, wd, gpost)


def _rope_tables(positions):
    pos = positions.astype(F32).reshape(-1)[:, None]
    lane = jnp.arange(LANES)

    def tables(dim):
        half = dim // 2
        in_head = lane % dim
        inv_freq = ROPE_THETA ** (-(in_head % half).astype(F32) * 2.0 / dim)
        ang = pos * inv_freq
        return jnp.cos(ang), jnp.where(in_head < half, -1.0, 1.0) * jnp.sin(ang)

    return tables(ATT_HEAD_DIM) + tables(IDX_DIM)


def _pick_tile(n, pref):
    t = min(pref, n)
    while n % t:
        t //= 2
    return t


def kernel(x, mem, positions, norm_mix_pre, w_in, conv_w, a_log, dt_bias, gdn_norm, w_out,
           norm_mix_post, norm_x_pre, norm_mem, w_xq, w_xkv, w_xo, norm_x_post,
           norm_ffn_pre, w_gu, w_down, norm_ffn_post):
    b, s, d = x.shape
    n = b * s
    n_mem = mem.shape[1]
    depth = w_in.shape[0]
    topk = min(TOPK_MAX, s // 4)
    tm = _pick_tile(s, 512)
    tq = _pick_tile(s, 256)
    kb = _pick_tile(s, 512)
    tb = _pick_tile(s, 256)

    ca, sa, ci, si = (t.reshape(b, s, 128) for t in _rope_tables(positions))

    row1 = lambda v: v.reshape(1, -1).astype(F32)

    x2 = x.reshape(n, d)
    for l in range(depth):
        w_att, w_gdn, w_gate = _prep_w_in(w_in, l)

        g_pre = row1(norm_mix_pre[l])
        x3 = x2.reshape(b, s, d)
        q, kt, v, qi, kit, wi = _proj_att(x3, g_pre, w_att, ca, sa, ci, si, tm)
        ya = _dsa(q, qi, wi, kt, v, kit, topk, tq, kb)

        qkv, ab, abt = _proj_gdn(x3, g_pre, w_gdn, tm)
        pad_lanes = lambda vec: jnp.pad(vec.astype(F32), (0, 128 - GDN_HEADS)).reshape(1, 128)
        bcast_rows = lambda vec: jnp.broadcast_to(vec.astype(F32)[:, None], (GDN_HEADS, tb))
        yb = _gdn(qkv, ab, abt, conv_w[l].astype(F32),
                  pad_lanes(a_log[l]), pad_lanes(dt_bias[l]), bcast_rows(a_log[l]), bcast_rows(dt_bias[l]),
                  row1(gdn_norm[l]), tb)

        x2 = _merge(x2, ya.reshape(n, 1024), yb.reshape(n, 1024), g_pre, w_gate,
                    w_out[l].astype(BF16), row1(norm_mix_post[l]), tm)

        kv = _mem_kv(mem.reshape(b * n_mem, d), row1(norm_mem[l]), w_xkv[l].astype(BF16))
        kv = kv.reshape(b, n_mem, 2 * d)
        kt_mem = jnp.swapaxes(kv[:, :, :d], 1, 2)
        x2 = _xattn(x2.reshape(b, s, d), kt_mem, kv[:, :, d:], row1(norm_x_pre[l]),
                    w_xq[l].astype(BF16), w_xo[l].astype(BF16), row1(norm_x_post[l]), tm).reshape(n, d)

        x2 = _ffn(x2, row1(norm_ffn_pre[l]), w_gu[l].astype(BF16), w_down[l].astype(BF16),
                  row1(norm_ffn_post[l]), tm)
    return x2.reshape(b, s, d)
```

```python
import functools

import jax
import jax.numpy as jnp
import numpy as np
from jax import lax
from jax.experimental import pallas as pl
from jax.experimental.pallas import tpu as pltpu

EPS = 1e-6
ROPE_THETA = 10000.0
ATT_HEADS = 8
ATT_KV_HEADS = 2
ATT_HEAD_DIM = 128
IDX_HEADS = 4
IDX_DIM = 64
TOPK_MAX = 256
GDN_HEADS = 8
GDN_DK = 128
GDN_DV = 128
CONV_K = 4
XATT_HEADS = 4

LANES = 128
VMEM_LIMIT = 56 * 1024 * 1024

F32 = jnp.float32
BF16 = jnp.bfloat16
I32 = jnp.int32

NEG_BIG = -1e30
LOG2E = 1.4426950408889634
INT_MAX = 2147483647
NEG_INF_KEY = -2139095041


def _cparams(sem):
    return pltpu.CompilerParams(dimension_semantics=sem, vmem_limit_bytes=VMEM_LIMIT)


def _const_spec(shape):
    nd = len(shape)
    return pl.BlockSpec(shape, lambda *_: (0,) * nd, pipeline_mode=pl.Buffered(1))


def _rms(x, g):
    return x * lax.rsqrt(jnp.mean(x * x, axis=-1, keepdims=True) + EPS) * g


def _dot(a, b):
    return jnp.dot(a, b, preferred_element_type=F32)


def _dot_nt(a, b):
    return lax.dot_general(a, b, (((1,), (1,)), ((), ())), preferred_element_type=F32)


def _split3(a):
    a1 = a.astype(BF16)
    r1 = a - a1.astype(F32)
    a2 = r1.astype(BF16)
    a3 = (r1 - a2.astype(F32)).astype(BF16)
    return a1, a2, a3


A_Q, A_KV, I_Q, I_K, I_W, B_QKV = 1024, 256, 256, 64, 4, 3072
O_IQ = A_Q + 2 * A_KV
O_IK = O_IQ + I_Q
O_IW = O_IK + I_K
O_B = O_IW + I_W
O_BA = O_B + B_QKV
O_GA = O_BA + 2 * GDN_HEADS
IN_COLS = O_GA + 2 * 1024


def _prep_w_in_kernel(w_ref, att_ref, gdn_ref, gate_ref):
    w = w_ref[...]
    zeros = lambda c: jnp.zeros((w.shape[0], c), F32)
    att = [w[:, :O_IQ]]
    for h in range(IDX_HEADS):
        att += [w[:, O_IQ + h * IDX_DIM:O_IQ + (h + 1) * IDX_DIM], zeros(128 - IDX_DIM)]
    att += [w[:, O_IK:O_IW], zeros(128 - I_K), w[:, O_IW:O_B], zeros(128 - I_W)]
    att_ref[...] = jnp.concatenate(att, axis=1).astype(BF16)
    gdn_ref[...] = jnp.concatenate([w[:, O_B:O_GA], zeros(128 - 2 * GDN_HEADS)], axis=1).astype(BF16)
    gate_ref[...] = w[:, O_GA:].astype(BF16)


def _prep_w_in(w_in, layer):
    _, d, cols = w_in.shape
    assert cols == IN_COLS
    tr = 128
    widths = (O_IQ + IDX_HEADS * 128 + 256, B_QKV + 128, 2048)
    return pl.pallas_call(
        _prep_w_in_kernel,
        grid=(d // tr,),
        in_specs=[pl.BlockSpec((None, tr, cols), lambda i: (layer, i, 0))],
        out_specs=[pl.BlockSpec((tr, c), lambda i: (i, 0)) for c in widths],
        out_shape=[jax.ShapeDtypeStruct((d, c), BF16) for c in widths],
        compiler_params=_cparams(("parallel",)),
        name="prep_w_in",
    )(w_in)


def _rope128(x, cos, sin_signed):
    return x * cos + pltpu.roll(x, 64, 1) * sin_signed


def _rope64(x, cos, sin_signed, first_half):
    partner = jnp.where(first_half, pltpu.roll(x, 96, 1), pltpu.roll(x, 32, 1))
    return x * cos + partner * sin_signed


def _proj_att_kernel(x_ref, g_ref, w_ref, ca_ref, sa_ref, ci_ref, si_ref,
                     q_ref, kt_ref, v_ref, qi_ref, kit_ref, wi_ref):
    h = _rms(x_ref[...], g_ref[...]).astype(BF16)
    p = _dot(h, w_ref[...])
    ca, sa = ca_ref[...], sa_ref[...]
    ci, si = ci_ref[...], si_ref[...]
    q_scale = ATT_HEAD_DIM ** -0.5 * LOG2E
    for hd in range(ATT_HEADS):
        sl = slice(hd * 128, (hd + 1) * 128)
        q_ref[:, sl] = (_rope128(p[:, sl], ca, sa) * q_scale).astype(BF16)
    for g in range(ATT_KV_HEADS):
        k_g = _rope128(p[:, A_Q + g * 128:A_Q + (g + 1) * 128], ca, sa)
        kt_ref[g * 128:(g + 1) * 128, :] = jnp.transpose(k_g).astype(BF16)
    v_ref[...] = p[:, A_Q + A_KV:O_IQ].astype(BF16)
    lane = lax.broadcasted_iota(I32, ci.shape, 1)
    first_half = (lane & 63) < 32
    idx_scale = IDX_DIM ** -0.5
    for hd in range(IDX_HEADS):
        off = O_IQ + hd * 128
        qi_ref[:, hd * 128:(hd + 1) * 128] = (
            _rope64(p[:, off:off + 128], ci, si, first_half) * idx_scale).astype(BF16)
    o_ki = O_IQ + IDX_HEADS * 128
    kit_ref[...] = jnp.transpose(_rope64(p[:, o_ki:o_ki + 128], ci, si, first_half)).astype(BF16)
    wi_ref[...] = p[:, o_ki + 128:o_ki + 256] * (IDX_HEADS ** -0.5)


def _proj_att(x, g, w, ca, sa, ci, si, tm):
    b, s, d = x.shape
    row = lambda c: pl.BlockSpec((None, tm, c), lambda bi, i: (bi, i, 0))
    col = lambda r: pl.BlockSpec((None, r, tm), lambda bi, i: (bi, 0, i))
    sds = jax.ShapeDtypeStruct
    return pl.pallas_call(
        _proj_att_kernel,
        grid=(b, s // tm),
        in_specs=[row(d), _const_spec((1, d)), _const_spec(w.shape),
                  row(128), row(128), row(128), row(128)],
        out_specs=[row(1024), col(256), row(256), row(512), col(128), row(128)],
        out_shape=[sds((b, s, 1024), BF16), sds((b, 256, s), BF16), sds((b, s, 256), BF16),
                   sds((b, s, 512), BF16), sds((b, 128, s), BF16), sds((b, s, 128), F32)],
        compiler_params=_cparams(("parallel", "parallel")),
        name="proj_att",
    )(x, g, w, ca, sa, ci, si)


def _proj_gdn_kernel(x_ref, g_ref, w_ref, qkv_ref, ab_ref, abt_ref):
    h = _rms(x_ref[...], g_ref[...]).astype(BF16)
    p = _dot(h, w_ref[...])
    qkv_ref[...] = p[:, :B_QKV]
    ab = p[:, B_QKV:B_QKV + 128]
    ab_ref[...] = ab
    abt_ref[...] = jnp.transpose(ab)[:16, :]


def _proj_gdn(x, g, w, tm):
    b, s, d = x.shape
    row = lambda c: pl.BlockSpec((None, tm, c), lambda bi, i: (bi, i, 0))
    sds = jax.ShapeDtypeStruct
    return pl.pallas_call(
        _proj_gdn_kernel,
        grid=(b, s // tm),
        in_specs=[row(d), _const_spec((1, d)), _const_spec(w.shape)],
        out_specs=[row(3072), row(128), pl.BlockSpec((None, 16, tm), lambda bi, i: (bi, 0, i))],
        out_shape=[sds((b, s, 3072), F32), sds((b, s, 128), F32), sds((b, 16, s), F32)],
        compiler_params=_cparams(("parallel", "parallel")),
        name="proj_gdn",
    )(x, g, w)


def _key_of(v):
    b = pltpu.bitcast(v, I32)
    return jnp.where(b < 0, b ^ 0x7FFFFFFF, b)


def _val_of(k):
    return pltpu.bitcast(jnp.where(k < 0, k ^ 0x7FFFFFFF, k), F32)


def _dsa_kernel(q_ref, qi_ref, wi_ref, kt_ref, v_ref, kit_ref, o_ref,
                keys_ref, qis_ref, qg_ref, bias_ref, s_ref, p_ref, acc_ref, m_ref, l_ref, run_ref, tri_ref,
                *, tq, kb, topk, seq):
    t0 = pl.program_id(1) * tq
    nkb = (t0 + tq + kb - 1) // kb
    n_lane_tiles = kb // LANES
    row_ids = t0 + lax.broadcasted_iota(I32, (tq, kb), 0)
    col_iota = lax.broadcasted_iota(I32, (tq, kb), 1)
    kf = float(topk)

    for h in range(IDX_HEADS):
        qis_ref[h * tq:(h + 1) * tq, :] = qi_ref[:, h * 128:(h + 1) * 128]
    wi = wi_ref[...]
    w_heads = [wi[:, h:h + 1] for h in range(IDX_HEADS)]

    def score_block(j, carry, masked):
        mx, mn = carry
        c0 = pl.multiple_of(j * kb, kb)
        halves = []
        for a0 in range(0, kb, 256):
            lg = _dot(qis_ref[...], kit_ref[:, pl.ds(c0 + a0, 256)])
            part = jnp.zeros((tq, 256), F32)
            for h in range(IDX_HEADS):
                part = part + w_heads[h] * jnp.maximum(lg[h * tq:(h + 1) * tq], 0.0)
            halves.append(part)
        sc = jnp.concatenate(halves, axis=1)
        sc = jnp.where(sc == 0.0, 0.0, sc)
        for i in range(n_lane_tiles):
            mx = jnp.maximum(mx, sc[:, i * LANES:(i + 1) * LANES])
            mn = jnp.minimum(mn, sc[:, i * LANES:(i + 1) * LANES])
        if masked:
            sc = jnp.where(c0 + col_iota <= row_ids, sc, -jnp.inf)
        keys_ref[:, pl.ds(c0, kb)] = _key_of(sc)
        return mx, mn

    n_full = t0 // kb
    ext = (jnp.full((tq, LANES), -jnp.inf, F32), jnp.full((tq, LANES), jnp.inf, F32))

    done_blocks = 0
    for group in (4, 2, 1):
        def score_group(jj, carry, group=group, first=done_blocks):
            for u in range(group):
                carry = score_block(first + group * jj + u, carry, False)
            return carry

        n_groups = (n_full - done_blocks) // group
        ext = lax.fori_loop(0, n_groups, score_group, ext)
        done_blocks = done_blocks + group * n_groups
    mx, mn = lax.fori_loop(n_full, nkb, functools.partial(score_block, masked=True), ext)
    row_max = jnp.max(mx, axis=1, keepdims=True)
    row_min = jnp.min(mn, axis=1, keepdims=True)

    n_row_tiles = tq // LANES if tq % LANES == 0 else 1
    rt = tq // n_row_tiles

    def as_col(row):
        return jnp.transpose(row)[:, :1]

    def count_ge(cands):
        nc = len(cands)
        tiles = range(n_row_tiles)
        cb = [[c if isinstance(c, int) else
               jnp.transpose(jnp.broadcast_to(c[:1, t * rt:(t + 1) * rt], (LANES, rt)))
               for t in tiles] for c in cands]

        def tile_counts(t):
            def body(j, accs):
                c0 = pl.multiple_of(j * kb, kb)
                accs = list(accs)
                blk = keys_ref[t * rt:(t + 1) * rt, pl.ds(c0, kb)]
                for i in range(n_lane_tiles):
                    for k in range(nc):
                        accs[k] = accs[k] + jnp.where(blk[:, i * LANES:(i + 1) * LANES] >= cb[k][t], 1.0, 0.0)
                return tuple(accs)

            z = jnp.zeros((rt, LANES), F32)
            return lax.fori_loop(0, nkb, body, (z,) * nc)

        accs = [tile_counts(t) for t in tiles]
        row_sums = lambda a: jnp.broadcast_to(jnp.sum(jnp.transpose(a), axis=0, keepdims=True), (8, rt))
        return [jnp.concatenate([row_sums(accs[t][k]) for t in tiles], axis=1) for k in range(nc)]

    as_row = lambda col: jnp.transpose(jnp.broadcast_to(col, (tq, 8)))
    n_visible = (t0 + lax.broadcasted_iota(I32, (8, tq), 1) + 1).astype(F32)
    few = n_visible < kf
    f_pos, f_zero = count_ge([1, 0])
    above = f_pos >= kf
    below = f_zero < kf
    at_zero = jnp.logical_and(jnp.logical_not(above), jnp.logical_not(below))
    key_min = _key_of(as_row(row_min))
    key_max = _key_of(as_row(row_max))
    lo0 = jnp.where(few, NEG_INF_KEY + 1, jnp.where(above, 1, jnp.where(below, key_min, 0)))
    flo0 = jnp.where(few, n_visible, jnp.where(above, f_pos, jnp.where(below, n_visible, f_zero)))
    hi0 = jnp.where(above, jnp.minimum(key_max, 2147483646) + 1, jnp.where(below, 0, 1))
    fhi0 = jnp.where(above, 0.0, jnp.where(below, f_zero, f_pos))
    log_target = float(np.log(topk + 0.5))
    glog = lambda c: jnp.log(jnp.maximum(c, 0.5)) - log_target
    done0 = jnp.where(jnp.logical_or(jnp.logical_or(few, at_zero), flo0 == kf), 1.0, 0.0)
    n_interp = 24

    def peel_min(base, flip):
        tiles = range(n_row_tiles)
        bb = [jnp.broadcast_to(base[t * rt:(t + 1) * rt], (rt, LANES)) for t in tiles]
        fb = [jnp.broadcast_to(flip[t * rt:(t + 1) * rt], (rt, LANES)) for t in tiles]

        def tile_min(t):
            def body(j, acc):
                c0 = pl.multiple_of(j * kb, kb)
                blk = keys_ref[t * rt:(t + 1) * rt, pl.ds(c0, kb)]
                for i in range(n_lane_tiles):
                    x = blk[:, i * LANES:(i + 1) * LANES] ^ fb[t]
                    acc = jnp.minimum(acc, jnp.where(x >= bb[t], x, INT_MAX))
                return acc

            acc = lax.fori_loop(0, nkb, body, jnp.full((rt, LANES), INT_MAX, I32))
            return jnp.min(acc, axis=1, keepdims=True)

        col = jnp.concatenate([tile_min(t) for t in tiles], axis=0)
        return jnp.transpose(jnp.broadcast_to(col, (tq, 8)))

    def probe(st, forced=None, use_forced=None):
        it, _, _, lo, hi, flo, fhi, glo, ghi, side, done = st
        v_lo, v_hi = _val_of(lo), _val_of(hi)
        v_model = (v_lo * ghi - v_hi * glo) / (ghi - glo)
        inside = flo - fhi
        v_even = v_lo + (v_hi - v_lo) * ((flo - kf + 0.5) / (inside + 1.0))
        c_interp = _key_of(jnp.where(inside <= 64.0, v_even, v_model))
        c_mid = jnp.right_shift(lo, 1) + jnp.right_shift(hi, 1) + (lo & hi & 1)
        cand = jnp.where(it >= n_interp, c_mid, c_interp)
        if forced is not None:
            cand = jnp.where(use_forced, forced, cand)
        cand = jnp.minimum(jnp.maximum(cand, lo + 1), hi - 1)
        c = count_ge([cand])[0]
        active = done < 0.5
        to_lo = jnp.logical_and(active, c >= kf)
        to_hi = jnp.logical_and(active, c < kf)
        gc = glog(c)
        ghi_n = jnp.where(to_hi, gc, jnp.where(jnp.logical_and(to_lo, side > 0.0), ghi * 0.5, ghi))
        glo_n = jnp.where(to_lo, gc, jnp.where(jnp.logical_and(to_hi, side < 0.0), glo * 0.5, glo))
        side = jnp.where(to_lo, 1.0, jnp.where(to_hi, -1.0, side))
        lo = jnp.where(to_lo, cand, lo)
        flo = jnp.where(to_lo, c, flo)
        hi = jnp.where(to_hi, cand, hi)
        fhi = jnp.where(to_hi, c, fhi)
        fin = jnp.logical_or(flo == kf, hi - 1 == lo)
        done = jnp.where(fin, 1.0, done)
        near = jnp.logical_or(flo - kf <= 1.0, kf - fhi <= 1.0)
        n_far = jnp.sum(jnp.where(near, 0.0, 1.0 - done))
        return it + 1, n_far, jnp.sum(1.0 - done), lo, hi, flo, fhi, glo_n, ghi_n, side, done

    n_pass_cap = n_interp + 34
    near0 = jnp.logical_or(flo0 - kf <= 1.0, kf - fhi0 <= 1.0)
    st = (jnp.int32(0), jnp.sum(jnp.where(near0, 0.0, 1.0 - done0)), jnp.sum(1.0 - done0),
          lo0, hi0, flo0, fhi0, glog(flo0), glog(fhi0), jnp.zeros((8, tq), F32), done0)
    st = lax.while_loop(lambda st: jnp.logical_and(st[0] < n_pass_cap, st[1] > 0.0), probe, st)
    lo, hi, flo, fhi, done = st[3], st[4], st[5], st[6], st[10]
    active = done < 0.5
    drop_one = jnp.logical_and(active, flo - kf <= 1.0)
    add_one = jnp.logical_and(jnp.logical_and(active, jnp.logical_not(drop_one)), kf - fhi <= 1.0)
    flip = jnp.where(add_one, -1, 0)
    edge = peel_min(as_col(jnp.where(add_one, -hi, lo)), as_col(flip))
    forced = jnp.where(add_one, edge ^ flip, edge + 1)
    st = probe(st, forced, jnp.logical_or(drop_one, add_one))
    st = lax.while_loop(lambda st: jnp.logical_and(st[0] < n_pass_cap + 2, st[2] > 0.0), probe, st)
    thr = as_col(st[3])
    tie_rows = jnp.logical_and(st[5] > kf, jnp.logical_not(few))
    need = as_col(jnp.where(tie_rows, kf - st[6], float(2 * seq)))

    run_ref[...] = jnp.zeros(run_ref.shape, F32)
    rr = lax.broadcasted_iota(I32, (LANES, 2 * LANES), 0)
    cc = lax.broadcasted_iota(I32, (LANES, 2 * LANES), 1)
    tri_ref[...] = jnp.where(jnp.logical_or(rr <= cc, cc >= LANES), 1.0, 0.0).astype(BF16)

    rep = ATT_HEADS // ATT_KV_HEADS
    rows = rep * tq
    rc = 32
    for g in range(ATT_KV_HEADS):
        for r in range(rep):
            hd = g * rep + r
            qg_ref[g, r * tq:(r + 1) * tq, :] = q_ref[:, hd * 128:(hd + 1) * 128]
    acc_ref[...] = jnp.zeros(acc_ref.shape, F32)
    l_ref[...] = jnp.zeros(l_ref.shape, F32)
    m_ref[...] = jnp.full(m_ref.shape, NEG_BIG, F32)

    def attend_block(j, carry):
        c0 = pl.multiple_of(j * kb, kb)
        left = jnp.broadcast_to(need - run_ref[...], (tq, LANES))
        for i in range(n_lane_tiles):
            keys = keys_ref[:, pl.ds(c0 + i * LANES, LANES)]
            tied = keys == thr
            cnt = _dot(jnp.where(tied, 1.0, 0.0).astype(BF16), tri_ref[...])
            ok = jnp.logical_and(tied, cnt[:, :LANES] <= left)
            bias_ref[:, i * LANES:(i + 1) * LANES] = jnp.where(keys > thr, 0.0, jnp.where(ok, 0.0, NEG_BIG))
            left = left - cnt[:, LANES:]
        run_ref[...] = need - left[:, :1]
        for g in range(ATT_KV_HEADS):
            for a0 in range(0, kb, 256):
                s_ref[g, :, a0:a0 + 256] = _dot(qg_ref[g], kt_ref[g * 128:(g + 1) * 128, pl.ds(c0 + a0, 256)])
        for g in range(ATT_KV_HEADS):
            for r0 in range(0, rows, rc):
                rs = slice(r0, r0 + rc)
                b0 = r0 % tq
                s = s_ref[g, rs, :] + bias_ref[b0:b0 + rc, :]
                m_prev = m_ref[g, rs, :]
                m_next = jnp.maximum(m_prev, jnp.max(s, axis=1, keepdims=True))
                p = jnp.exp2(s - jnp.concatenate([m_next] * n_lane_tiles, axis=1))
                alpha = jnp.exp2(m_prev - m_next)
                l_ref[g, rs, :] = alpha * l_ref[g, rs, :] + jnp.sum(p, axis=1, keepdims=True)
                m_ref[g, rs, :] = m_next
                acc_ref[g, rs, :] = acc_ref[g, rs, :] * alpha
                p_ref[g, rs, :] = p.astype(BF16)
            acc_ref[g] += _dot(p_ref[g], v_ref[pl.ds(c0, kb), g * 128:(g + 1) * 128])
        return carry

    lax.fori_loop(0, nkb, attend_block, 0)

    for g in range(ATT_KV_HEADS):
        out = acc_ref[g] / l_ref[g]
        for r in range(rep):
            hd = g * rep + r
            o_ref[:, hd * 128:(hd + 1) * 128] = out[r * tq:(r + 1) * tq]


def _dsa(q, qi, wi, kt, v, kit, topk, tq, kb):
    b, s, _ = q.shape
    rep = ATT_HEADS // ATT_KV_HEADS
    qspec = lambda c: pl.BlockSpec((None, tq, c), lambda bi, i: (bi, i, 0))
    per_batch = lambda r, c: pl.BlockSpec((None, r, c), lambda bi, i: (bi, 0, 0),
                                          pipeline_mode=pl.Buffered(1))
    return pl.pallas_call(
        functools.partial(_dsa_kernel, tq=tq, kb=kb, topk=topk, seq=s),
        grid=(b, s // tq),
        in_specs=[qspec(1024), qspec(512), qspec(128),
                  per_batch(256, s), per_batch(s, 256), per_batch(128, s)],
        out_specs=qspec(1024),
        out_shape=jax.ShapeDtypeStruct((b, s, 1024), F32),
        scratch_shapes=[pltpu.VMEM((tq, s), I32),
                        pltpu.VMEM((IDX_HEADS * tq, 128), BF16),
                        pltpu.VMEM((ATT_KV_HEADS, rep * tq, 128), BF16),
                        pltpu.VMEM((tq, kb), F32),
                        pltpu.VMEM((ATT_KV_HEADS, rep * tq, kb), F32),
                        pltpu.VMEM((ATT_KV_HEADS, rep * tq, kb), BF16),
                        pltpu.VMEM((ATT_KV_HEADS, rep * tq, 128), F32),
                        pltpu.VMEM((ATT_KV_HEADS, rep * tq, 128), F32),
                        pltpu.VMEM((ATT_KV_HEADS, rep * tq, 128), F32),
                        pltpu.VMEM((tq, 1), F32),
                        pltpu.VMEM((LANES, 2 * LANES), BF16)],
        compiler_params=_cparams(("parallel", "arbitrary")),
        name="dsa_attention",
    )(q, qi, wi, kt, v, kit)


def _softplus(x):
    return jnp.maximum(x, 0.0) + jnp.log(1.0 + jnp.exp(-jnp.abs(x)))


def _gdn_kernel(qkv_ref, ab_ref, abt_ref, cw_ref, ac_ref, dc_ref, ar_ref, dr_ref, gn_ref,
                y_ref, xbuf_ref, act_ref, state_ref, *, tb):
    ck = tb
    n_chunks = tb // ck

    @pl.when(pl.program_id(1) == 0)
    def _():
        xbuf_ref[0:8, :] = jnp.zeros((8, xbuf_ref.shape[1]), F32)
        state_ref[...] = jnp.zeros(state_ref.shape, F32)

    xbuf_ref[8:8 + tb, :] = qkv_ref[...]
    for c in range(xbuf_ref.shape[1] // LANES):
        sl = slice(c * LANES, (c + 1) * LANES)
        y = xbuf_ref[5:5 + tb, sl] * cw_ref[0:1, sl]
        for j in range(1, CONV_K):
            y = y + xbuf_ref[5 + j:5 + j + tb, sl] * cw_ref[j:j + 1, sl]
        act_ref[:, sl] = y * (1.0 / (1.0 + jnp.exp(-y)))
    xbuf_ref[0:8, :] = xbuf_ref[tb:tb + 8, :]

    ab = ab_ref[...]
    g_col = -jnp.exp(ac_ref[...]) * _softplus(ab + dc_ref[...])
    beta_col = 1.0 / (1.0 + jnp.exp(-ab))
    g_row = -jnp.exp(ar_ref[...]) * _softplus(abt_ref[0:8, :] + dr_ref[...])

    r = lax.broadcasted_iota(I32, (tb, tb), 0)
    c = lax.broadcasted_iota(I32, (tb, tb), 1)
    chunk_shift = ck.bit_length() - 1
    same = jnp.right_shift(r, chunk_shift) == jnp.right_shift(c, chunk_shift)
    lower = jnp.logical_and(same, r >= c)
    strict = jnp.logical_and(same, r > c)
    lower_m = jnp.where(lower, 1.0, 0.0).astype(BF16)
    upper_m = jnp.where(jnp.logical_and(same, r <= c), 1.0, 0.0).astype(BF16)
    same_m = jnp.where(same, 1.0, 0.0).astype(BF16)
    eye = jnp.where(r == c, 1.0, 0.0)
    n_levels = ck.bit_length() - 1
    lvl_subs = []
    for lvl in range(n_levels):
        lvl_subs.append(jnp.logical_and(jnp.right_shift(r, lvl + 1) == jnp.right_shift(c, lvl + 1),
                                        jnp.logical_and((jnp.right_shift(r, lvl) & 1) == 1,
                                                        (jnp.right_shift(c, lvl) & 1) == 0)))
    lvl_masks = [jnp.where(sub, 1.0, 0.0).astype(BF16) for sub in lvl_subs]

    gparts = _split3(g_col)
    gc_col = _dot(lower_m, gparts[0]) + (_dot(lower_m, gparts[1]) + _dot(lower_m, gparts[2]))
    gl_col = _dot(same_m, gparts[0]) + (_dot(same_m, gparts[1]) + _dot(same_m, gparts[2]))
    rparts = _split3(g_row)
    gc_row = _dot(rparts[0], upper_m) + (_dot(rparts[1], upper_m) + _dot(rparts[2], upper_m))

    gn = gn_ref[...]
    hg = 4
    for h0 in range(0, GDN_HEADS, hg):
        heads = range(h0, h0 + hg)
        q_l, k_l, kbeta_l, vbeta_l, decay_l, gc_l, gl_l, m16_l, d_l = ([] for _ in range(9))
        for h in heads:
            q = act_ref[:, h * 128:(h + 1) * 128]
            k = act_ref[:, 1024 + h * 128:1024 + (h + 1) * 128]
            v = act_ref[:, 2048 + h * 128:2048 + (h + 1) * 128]
            q = q * lax.rsqrt(jnp.sum(q * q, axis=-1, keepdims=True) + EPS) * (GDN_DK ** -0.5)
            k = k * lax.rsqrt(jnp.sum(k * k, axis=-1, keepdims=True) + EPS)
            gc = gc_col[:, h:h + 1]
            beta = beta_col[:, 8 + h:9 + h]
            diff = gc - gc_row[h:h + 1, :]
            decay = jnp.where(lower, jnp.exp(jnp.where(lower, diff, 0.0)), 0.0)
            kbeta = k * beta
            nm = jnp.where(strict, -(_dot_nt(kbeta.astype(BF16), k.astype(BF16)) * decay), 0.0)
            q_l.append(q); k_l.append(k); kbeta_l.append(kbeta); vbeta_l.append(v * beta)
            decay_l.append(decay); gc_l.append(gc); gl_l.append(gl_col[:, h:h + 1])
            m16_l.append(nm.astype(BF16)); d_l.append(eye + jnp.where(lvl_subs[0], nm, 0.0))
        for lvl in range(1, n_levels):
            for i in range(hg):
                dh = d_l[i].astype(BF16)
                lh = m16_l[i] * lvl_masks[lvl]
                th = _dot(dh, lh).astype(BF16)
                d_l[i] = d_l[i] + _dot(th, dh)
        u_l, w_l, attn_l, qd_l, kd_l = [], [], [], [], []
        for i in range(hg):
            egc = jnp.exp(gc_l[i])
            rhs = jnp.concatenate([vbeta_l[i], kbeta_l[i] * egc], axis=1)
            sol = _dot(d_l[i].astype(BF16), rhs.astype(BF16))
            u_l.append(sol[:, :128])
            w_l.append(sol[:, 128:].astype(BF16))
            k16 = k_l[i].astype(BF16)
            attn_l.append(jnp.where(lower, _dot_nt(q_l[i].astype(BF16), k16) * decay_l[i], 0.0).astype(BF16))
            qd_l.append((q_l[i] * egc).astype(BF16))
            kd_l.append(k_l[i] * jnp.exp(gl_l[i] - gc_l[i]))
        st_l = [state_ref[h] for h in heads]
        outs = [[] for _ in range(hg)]
        for ci in range(n_chunks):
            rc = slice(ci * ck, (ci + 1) * ck)
            for i in range(hg):
                st16 = st_l[i].astype(BF16)
                v_new = u_l[i][rc] - _dot(w_l[i][rc], st16)
                v_new16 = v_new.astype(BF16)
                outs[i].append(_dot(qd_l[i][rc], st16) + _dot(attn_l[i][rc, rc], v_new16))
                kd_t = jnp.transpose(kd_l[i][rc]).astype(BF16)
                st_l[i] = (st_l[i] * jnp.exp(gl_l[i][ci * ck:ci * ck + 1])
                           + _dot(kd_t, v_new16))
        for i, h in enumerate(heads):
            state_ref[h] = st_l[i]
            y_ref[:, h * 128:(h + 1) * 128] = _rms(jnp.concatenate(outs[i], axis=0), gn)


def _gdn(qkv, ab, abt, conv_w, ac, dc, ar, dr, gn, tb):
    b, s, _ = qkv.shape
    blk = lambda c: pl.BlockSpec((None, tb, c), lambda bi, i: (bi, i, 0))
    return pl.pallas_call(
        functools.partial(_gdn_kernel, tb=tb),
        grid=(b, s // tb),
        in_specs=[blk(3072), blk(128), pl.BlockSpec((None, 16, tb), lambda bi, i: (bi, 0, i)),
                  _const_spec(conv_w.shape), _const_spec(ac.shape), _const_spec(dc.shape),
                  _const_spec(ar.shape), _const_spec(dr.shape), _const_spec(gn.shape)],
        out_specs=blk(1024),
        out_shape=jax.ShapeDtypeStruct((b, s, 1024), F32),
        scratch_shapes=[pltpu.VMEM((tb + 8, 3072), F32),
                        pltpu.VMEM((tb, 3072), F32),
                        pltpu.VMEM((GDN_HEADS, GDN_DK, GDN_DV), F32)],
        compiler_params=_cparams(("parallel", "arbitrary")),
        name="gated_delta",
    )(qkv, ab, abt, conv_w, ac, dc, ar, dr, gn)


def _sigmoid(x):
    return 1.0 / (1.0 + jnp.exp(-x))


def _merge_kernel(x_ref, ya_ref, yb_ref, gpre_ref, wg_ref, wo_ref, gpost_ref, o_ref):
    x = x_ref[...]
    h = _rms(x, gpre_ref[...]).astype(BF16)
    gates = _dot(h, wg_ref[...])
    y = _sigmoid(gates[:, :1024]) * ya_ref[...] + _sigmoid(gates[:, 1024:]) * yb_ref[...]
    z = _dot(y.astype(BF16), wo_ref[...])
    o_ref[...] = x + _rms(z, gpost_ref[...])


def _merge(x2, ya, yb, gpre, wg, wo, gpost, tm):
    n, d = x2.shape
    row = pl.BlockSpec((tm, d), lambda i: (i, 0))
    return pl.pallas_call(
        _merge_kernel,
        grid=(n // tm,),
        in_specs=[row, row, row, _const_spec((1, d)), _const_spec(wg.shape),
                  _const_spec(wo.shape), _const_spec((1, d))],
        out_specs=row,
        out_shape=jax.ShapeDtypeStruct((n, d), F32),
        compiler_params=_cparams(("parallel",)),
        name="merge_out",
    )(x2, ya, yb, gpre, wg, wo, gpost)


def _mem_kv_kernel(m_ref, g_ref, w_ref, kv_ref):
    h = _rms(m_ref[...], g_ref[...]).astype(BF16)
    kv_ref[...] = _dot(h, w_ref[...]).astype(BF16)


def _mem_kv(mem2, g, w):
    n, d = mem2.shape
    return pl.pallas_call(
        _mem_kv_kernel,
        out_shape=jax.ShapeDtypeStruct((n, w.shape[1]), BF16),
        compiler_params=pltpu.CompilerParams(vmem_limit_bytes=VMEM_LIMIT),
        name="mem_kv",
    )(mem2, g, w)


def _xattn_kernel(x_ref, kt_ref, v_ref, gpre_ref, wq_ref, wo_ref, gpost_ref, o_ref):
    x = x_ref[...]
    d = x.shape[1]
    hd = d // XATT_HEADS
    h = _rms(x, gpre_ref[...]).astype(BF16)
    q = (_dot(h, wq_ref[...]) * (hd ** -0.5)).astype(BF16)
    heads = []
    for i in range(XATT_HEADS):
        sl = slice(i * hd, (i + 1) * hd)
        s = _dot(q[:, sl], kt_ref[sl, :])
        p = jnp.exp(s - jnp.max(s, axis=1, keepdims=True))
        o = _dot(p.astype(BF16), v_ref[:, sl]) / jnp.sum(p, axis=1, keepdims=True)
        heads.append(o.astype(BF16))
    z = _dot(jnp.concatenate(heads, axis=1), wo_ref[...])
    o_ref[...] = x + _rms(z, gpost_ref[...])


def _xattn(x, kt, v, gpre, wq, wo, gpost, tm):
    b, s, d = x.shape
    n_mem = v.shape[1]
    row = pl.BlockSpec((None, tm, d), lambda bi, i: (bi, i, 0))
    return pl.pallas_call(
        _xattn_kernel,
        grid=(b, s // tm),
        in_specs=[row,
                  pl.BlockSpec((None, d, n_mem), lambda bi, i: (bi, 0, 0)),
                  pl.BlockSpec((None, n_mem, d), lambda bi, i: (bi, 0, 0)),
                  _const_spec((1, d)), _const_spec(wq.shape), _const_spec(wo.shape),
                  _const_spec((1, d))],
        out_specs=row,
        out_shape=jax.ShapeDtypeStruct((b, s, d), F32),
        compiler_params=_cparams(("parallel", "parallel")),
        name="mem_xattn",
    )(x, kt, v, gpre, wq, wo, gpost)


def _ffn_kernel(x_ref, gpre_ref, wgu_ref, wd_ref, gpost_ref, o_ref, *, ff, fc):
    x = x_ref[...]
    h = _rms(x, gpre_ref[...]).astype(BF16)
    z = jnp.zeros(x.shape, F32)
    for c0 in range(0, ff, fc):
        gate = _dot(h, wgu_ref[:, c0:c0 + fc])
        up = _dot(h, wgu_ref[:, ff + c0:ff + c0 + fc])
        act = (gate * _sigmoid(gate) * up).astype(BF16)
        z = z + _dot(act, wd_ref[c0:c0 + fc, :])
    o_ref[...] = x + _rms(z, gpost_ref[...])


def _ffn(x2, gpre, wgu, wd, gpost, tm):
    n, d = x2.shape
    ff = wd.shape[0]
    fc = ff // 2 if (ff // 2) % LANES == 0 else ff
    row = pl.BlockSpec((tm, d), lambda i: (i, 0))
    return pl.pallas_call(
        functools.partial(_ffn_kernel, ff=ff, fc=fc),
        grid=(n // tm,),
        in_specs=[row, _const_spec((1, d)), _const_spec(wgu.shape), _const_spec(wd.shape),
                  _const_spec((1, d))],
        out_specs=row,
        out_shape=jax.ShapeDtypeStruct((n, d), F32),
        compiler_params=_cparams(("parallel",)),
        name="swiglu",
    )(x2, gpre, wgu, wd, gpost)


def _rope_tables(positions):
    pos = positions.astype(F32).reshape(-1)[:, None]
    lane = jnp.arange(LANES)

    def tables(dim):
        half = dim // 2
        in_head = lane % dim
        inv_freq = ROPE_THETA ** (-(in_head % half).astype(F32) * 2.0 / dim)
        ang = pos * inv_freq
        return jnp.cos(ang), jnp.where(in_head < half, -1.0, 1.0) * jnp.sin(ang)

    return tables(ATT_HEAD_DIM) + tables(IDX_DIM)


def _pick_tile(n, pref):
    t = min(pref, n)
    while n % t:
        t //= 2
    return t


def kernel(x, mem, positions, norm_mix_pre, w_in, conv_w, a_log, dt_bias, gdn_norm, w_out,
           norm_mix_post, norm_x_pre, norm_mem, w_xq, w_xkv, w_xo, norm_x_post,
           norm_ffn_pre, w_gu, w_down, norm_ffn_post):
    b, s, d = x.shape
    n = b * s
    n_mem = mem.shape[1]
    depth = w_in.shape[0]
    topk = min(TOPK_MAX, s // 4)
    tm = _pick_tile(s, 512)
    tq = _pick_tile(s, 256)
    kb = _pick_tile(s, 512)
    tb = _pick_tile(s, 256)

    ca, sa, ci, si = (t.reshape(b, s, 128) for t in _rope_tables(positions))

    row1 = lambda v: v.reshape(1, -1).astype(F32)

    x2 = x.reshape(n, d)
    for l in range(depth):
        w_att, w_gdn, w_gate = _prep_w_in(w_in, l)

        g_pre = row1(norm_mix_pre[l])
        x3 = x2.reshape(b, s, d)
        q, kt, v, qi, kit, wi = _proj_att(x3, g_pre, w_att, ca, sa, ci, si, tm)
        ya = _dsa(q, qi, wi, kt, v, kit, topk, tq, kb)

        qkv, ab, abt = _proj_gdn(x3, g_pre, w_gdn, tm)
        pad_lanes = lambda vec: jnp.pad(vec.astype(F32), (0, 128 - GDN_HEADS)).reshape(1, 128)
        bcast_rows = lambda vec: jnp.broadcast_to(vec.astype(F32)[:, None], (GDN_HEADS, tb))
        yb = _gdn(qkv, ab, abt, conv_w[l].astype(F32),
                  pad_lanes(a_log[l]), pad_lanes(dt_bias[l]), bcast_rows(a_log[l]), bcast_rows(dt_bias[l]),
                  row1(gdn_norm[l]), tb)

        x2 = _merge(x2, ya.reshape(n, 1024), yb.reshape(n, 1024), g_pre, w_gate,
                    w_out[l].astype(BF16), row1(norm_mix_post[l]), tm)

        kv = _mem_kv(mem.reshape(b * n_mem, d), row1(norm_mem[l]), w_xkv[l].astype(BF16))
        kv = kv.reshape(b, n_mem, 2 * d)
        kt_mem = jnp.swapaxes(kv[:, :, :d], 1, 2)
        x2 = _xattn(x2.reshape(b, s, d), kt_mem, kv[:, :, d:], row1(norm_x_pre[l]),
                    w_xq[l].astype(BF16), w_xo[l].astype(BF16), row1(norm_x_post[l]), tm).reshape(n, d)

        x2 = _ffn(x2, row1(norm_ffn_pre[l]), w_gu[l].astype(BF16), w_down[l].astype(BF16),
                  row1(norm_ffn_post[l]), tm)
    return x2.reshape(b, s, d)
```

```python
import functools

import jax
import jax.numpy as jnp
import numpy as np
from jax import lax
from jax.experimental import pallas as pl
from jax.experimental.pallas import tpu as pltpu

EPS = 1e-6
ROPE_THETA = 10000.0
ATT_HEADS = 8
ATT_KV_HEADS = 2
ATT_HEAD_DIM = 128
IDX_HEADS = 4
IDX_DIM = 64
TOPK_MAX = 256
GDN_HEADS = 8
GDN_DK = 128
GDN_DV = 128
CONV_K = 4
XATT_HEADS = 4

LANES = 128
VMEM_LIMIT = 56 * 1024 * 1024

F32 = jnp.float32
BF16 = jnp.bfloat16
I32 = jnp.int32

NEG_BIG = -1e30
LOG2E = 1.4426950408889634
INT_MAX = 2147483647
NEG_INF_KEY = -2139095041


def _cparams(sem):
    return pltpu.CompilerParams(dimension_semantics=sem, vmem_limit_bytes=VMEM_LIMIT)


def _const_spec(shape):
    nd = len(shape)
    return pl.BlockSpec(shape, lambda *_: (0,) * nd, pipeline_mode=pl.Buffered(1))


def _rms(x, g):
    return x * lax.rsqrt(jnp.mean(x * x, axis=-1, keepdims=True) + EPS) * g


def _dot(a, b):
    return jnp.dot(a, b, preferred_element_type=F32)


def _dot_nt(a, b):
    return lax.dot_general(a, b, (((1,), (1,)), ((), ())), preferred_element_type=F32)


def _split3(a):
    a1 = a.astype(BF16)
    r1 = a - a1.astype(F32)
    a2 = r1.astype(BF16)
    a3 = (r1 - a2.astype(F32)).astype(BF16)
    return a1, a2, a3


A_Q, A_KV, I_Q, I_K, I_W, B_QKV = 1024, 256, 256, 64, 4, 3072
O_IQ = A_Q + 2 * A_KV
O_IK = O_IQ + I_Q
O_IW = O_IK + I_K
O_B = O_IW + I_W
O_BA = O_B + B_QKV
O_GA = O_BA + 2 * GDN_HEADS
IN_COLS = O_GA + 2 * 1024


def _prep_w_in_kernel(w_ref, att_ref, gdn_ref, gate_ref):
    w = w_ref[...]
    zeros = lambda c: jnp.zeros((w.shape[0], c), F32)
    att = [w[:, :O_IQ]]
    for h in range(IDX_HEADS):
        att += [w[:, O_IQ + h * IDX_DIM:O_IQ + (h + 1) * IDX_DIM], zeros(128 - IDX_DIM)]
    att += [w[:, O_IK:O_IW], zeros(128 - I_K), w[:, O_IW:O_B], zeros(128 - I_W)]
    att_ref[...] = jnp.concatenate(att, axis=1).astype(BF16)
    gdn_ref[...] = jnp.concatenate([w[:, O_B:O_GA], zeros(128 - 2 * GDN_HEADS)], axis=1).astype(BF16)
    gate_ref[...] = w[:, O_GA:].astype(BF16)


def _prep_w_in(w_in, layer):
    _, d, cols = w_in.shape
    assert cols == IN_COLS
    tr = 128
    widths = (O_IQ + IDX_HEADS * 128 + 256, B_QKV + 128, 2048)
    return pl.pallas_call(
        _prep_w_in_kernel,
        grid=(d // tr,),
        in_specs=[pl.BlockSpec((None, tr, cols), lambda i: (layer, i, 0))],
        out_specs=[pl.BlockSpec((tr, c), lambda i: (i, 0)) for c in widths],
        out_shape=[jax.ShapeDtypeStruct((d, c), BF16) for c in widths],
        compiler_params=_cparams(("parallel",)),
        name="prep_w_in",
    )(w_in)


def _rope128(x, cos, sin_signed):
    return x * cos + pltpu.roll(x, 64, 1) * sin_signed


def _rope64(x, cos, sin_signed, first_half):
    partner = jnp.where(first_half, pltpu.roll(x, 96, 1), pltpu.roll(x, 32, 1))
    return x * cos + partner * sin_signed


def _proj_att_kernel(x_ref, g_ref, w_ref, ca_ref, sa_ref, ci_ref, si_ref,
                     q_ref, kt_ref, v_ref, qi_ref, kit_ref, wi_ref):
    h = _rms(x_ref[...], g_ref[...]).astype(BF16)
    p = _dot(h, w_ref[...])
    ca, sa = ca_ref[...], sa_ref[...]
    ci, si = ci_ref[...], si_ref[...]
    q_scale = ATT_HEAD_DIM ** -0.5 * LOG2E
    for hd in range(ATT_HEADS):
        sl = slice(hd * 128, (hd + 1) * 128)
        q_ref[:, sl] = (_rope128(p[:, sl], ca, sa) * q_scale).astype(BF16)
    for g in range(ATT_KV_HEADS):
        k_g = _rope128(p[:, A_Q + g * 128:A_Q + (g + 1) * 128], ca, sa)
        kt_ref[g * 128:(g + 1) * 128, :] = jnp.transpose(k_g).astype(BF16)
    v_ref[...] = p[:, A_Q + A_KV:O_IQ].astype(BF16)
    lane = lax.broadcasted_iota(I32, ci.shape, 1)
    first_half = (lane & 63) < 32
    idx_scale = IDX_DIM ** -0.5
    for hd in range(IDX_HEADS):
        off = O_IQ + hd * 128
        qi_ref[:, hd * 128:(hd + 1) * 128] = (
            _rope64(p[:, off:off + 128], ci, si, first_half) * idx_scale).astype(BF16)
    o_ki = O_IQ + IDX_HEADS * 128
    kit_ref[...] = jnp.transpose(_rope64(p[:, o_ki:o_ki + 128], ci, si, first_half)).astype(BF16)
    wi_ref[...] = p[:, o_ki + 128:o_ki + 256] * (IDX_HEADS ** -0.5)


def _proj_att(x, g, w, ca, sa, ci, si, tm):
    b, s, d = x.shape
    row = lambda c: pl.BlockSpec((None, tm, c), lambda bi, i: (bi, i, 0))
    col = lambda r: pl.BlockSpec((None, r, tm), lambda bi, i: (bi, 0, i))
    sds = jax.ShapeDtypeStruct
    return pl.pallas_call(
        _proj_att_kernel,
        grid=(b, s // tm),
        in_specs=[row(d), _const_spec((1, d)), _const_spec(w.shape),
                  row(128), row(128), row(128), row(128)],
        out_specs=[row(1024), col(256), row(256), row(512), col(128), row(128)],
        out_shape=[sds((b, s, 1024), BF16), sds((b, 256, s), BF16), sds((b, s, 256), BF16),
                   sds((b, s, 512), BF16), sds((b, 128, s), BF16), sds((b, s, 128), F32)],
        compiler_params=_cparams(("parallel", "parallel")),
        name="proj_att",
    )(x, g, w, ca, sa, ci, si)


def _proj_gdn_kernel(x_ref, g_ref, w_ref, qkv_ref, ab_ref, abt_ref):
    h = _rms(x_ref[...], g_ref[...]).astype(BF16)
    p = _dot(h, w_ref[...])
    qkv_ref[...] = p[:, :B_QKV]
    ab = p[:, B_QKV:B_QKV + 128]
    ab_ref[...] = ab
    abt_ref[...] = jnp.transpose(ab)[:16, :]


def _proj_gdn(x, g, w, tm):
    b, s, d = x.shape
    row = lambda c: pl.BlockSpec((None, tm, c), lambda bi, i: (bi, i, 0))
    sds = jax.ShapeDtypeStruct
    return pl.pallas_call(
        _proj_gdn_kernel,
        grid=(b, s // tm),
        in_specs=[row(d), _const_spec((1, d)), _const_spec(w.shape)],
        out_specs=[row(3072), row(128), pl.BlockSpec((None, 16, tm), lambda bi, i: (bi, 0, i))],
        out_shape=[sds((b, s, 3072), F32), sds((b, s, 128), F32), sds((b, 16, s), F32)],
        compiler_params=_cparams(("parallel", "parallel")),
        name="proj_gdn",
    )(x, g, w)


def _key_of(v):
    b = pltpu.bitcast(v, I32)
    return jnp.where(b < 0, b ^ 0x7FFFFFFF, b)


def _val_of(k):
    return pltpu.bitcast(jnp.where(k < 0, k ^ 0x7FFFFFFF, k), F32)


def _dsa_kernel(q_ref, qi_ref, wi_ref, kt_ref, v_ref, kit_ref, o_ref,
                keys_ref, qis_ref, qg_ref, bias_ref, s_ref, p_ref, acc_ref, m_ref, l_ref, run_ref, tri_ref,
                *, tq, kb, topk, seq):
    t0 = pl.program_id(1) * tq
    nkb = (t0 + tq + kb - 1) // kb
    n_lane_tiles = kb // LANES
    row_ids = t0 + lax.broadcasted_iota(I32, (tq, kb), 0)
    col_iota = lax.broadcasted_iota(I32, (tq, kb), 1)
    kf = float(topk)

    for h in range(IDX_HEADS):
        qis_ref[h * tq:(h + 1) * tq, :] = qi_ref[:, h * 128:(h + 1) * 128]
    wi = wi_ref[...]
    w_heads = [wi[:, h:h + 1] for h in range(IDX_HEADS)]

    def score_block(j, carry, masked):
        mx, mn = carry
        c0 = pl.multiple_of(j * kb, kb)
        halves = []
        for a0 in range(0, kb, 256):
            lg = _dot(qis_ref[...], kit_ref[:, pl.ds(c0 + a0, 256)])
            part = jnp.zeros((tq, 256), F32)
            for h in range(IDX_HEADS):
                part = part + w_heads[h] * jnp.maximum(lg[h * tq:(h + 1) * tq], 0.0)
            halves.append(part)
        sc = jnp.concatenate(halves, axis=1)
        sc = jnp.where(sc == 0.0, 0.0, sc)
        for i in range(n_lane_tiles):
            mx = jnp.maximum(mx, sc[:, i * LANES:(i + 1) * LANES])
            mn = jnp.minimum(mn, sc[:, i * LANES:(i + 1) * LANES])
        if masked:
            sc = jnp.where(c0 + col_iota <= row_ids, sc, -jnp.inf)
        keys_ref[:, pl.ds(c0, kb)] = _key_of(sc)
        return mx, mn

    n_full = t0 // kb
    ext = (jnp.full((tq, LANES), -jnp.inf, F32), jnp.full((tq, LANES), jnp.inf, F32))

    done_blocks = 0
    for group in (4, 2, 1):
        def score_group(jj, carry, group=group, first=done_blocks):
            for u in range(group):
                carry = score_block(first + group * jj + u, carry, False)
            return carry

        n_groups = (n_full - done_blocks) // group
        ext = lax.fori_loop(0, n_groups, score_group, ext)
        done_blocks = done_blocks + group * n_groups
    mx, mn = lax.fori_loop(n_full, nkb, functools.partial(score_block, masked=True), ext)
    row_max = jnp.max(mx, axis=1, keepdims=True)
    row_min = jnp.min(mn, axis=1, keepdims=True)

    n_row_tiles = tq // LANES if tq % LANES == 0 else 1
    rt = tq // n_row_tiles

    def as_col(row):
        return jnp.transpose(row)[:, :1]

    def count_ge(cands):
        nc = len(cands)
        tiles = range(n_row_tiles)
        cb = [[c if isinstance(c, int) else
               jnp.transpose(jnp.broadcast_to(c[:1, t * rt:(t + 1) * rt], (LANES, rt)))
               for t in tiles] for c in cands]

        def tile_counts(t):
            def body(j, accs):
                c0 = pl.multiple_of(j * kb, kb)
                accs = list(accs)
                blk = keys_ref[t * rt:(t + 1) * rt, pl.ds(c0, kb)]
                for i in range(n_lane_tiles):
                    for k in range(nc):
                        accs[k] = accs[k] + jnp.where(blk[:, i * LANES:(i + 1) * LANES] >= cb[k][t], 1.0, 0.0)
                return tuple(accs)

            z = jnp.zeros((rt, LANES), F32)
            return lax.fori_loop(0, nkb, body, (z,) * nc)

        accs = [tile_counts(t) for t in tiles]
        row_sums = lambda a: jnp.broadcast_to(jnp.sum(jnp.transpose(a), axis=0, keepdims=True), (8, rt))
        return [jnp.concatenate([row_sums(accs[t][k]) for t in tiles], axis=1) for k in range(nc)]

    as_row = lambda col: jnp.transpose(jnp.broadcast_to(col, (tq, 8)))
    n_visible = (t0 + lax.broadcasted_iota(I32, (8, tq), 1) + 1).astype(F32)
    few = n_visible < kf
    f_pos, f_zero = count_ge([1, 0])
    above = f_pos >= kf
    below = f_zero < kf
    at_zero = jnp.logical_and(jnp.logical_not(above), jnp.logical_not(below))
    key_min = _key_of(as_row(row_min))
    key_max = _key_of(as_row(row_max))
    lo0 = jnp.where(few, NEG_INF_KEY + 1, jnp.where(above, 1, jnp.where(below, key_min, 0)))
    flo0 = jnp.where(few, n_visible, jnp.where(above, f_pos, jnp.where(below, n_visible, f_zero)))
    hi0 = jnp.where(above, jnp.minimum(key_max, 2147483646) + 1, jnp.where(below, 0, 1))
    fhi0 = jnp.where(above, 0.0, jnp.where(below, f_zero, f_pos))
    log_target = float(np.log(topk + 0.5))
    glog = lambda c: jnp.log(jnp.maximum(c, 0.5)) - log_target
    done0 = jnp.where(jnp.logical_or(jnp.logical_or(few, at_zero), flo0 == kf), 1.0, 0.0)
    n_interp = 24

    def peel_min(base, flip):
        tiles = range(n_row_tiles)
        bb = [jnp.broadcast_to(base[t * rt:(t + 1) * rt], (rt, LANES)) for t in tiles]
        fb = [jnp.broadcast_to(flip[t * rt:(t + 1) * rt], (rt, LANES)) for t in tiles]

        def tile_min(t):
            def body(j, acc):
                c0 = pl.multiple_of(j * kb, kb)
                blk = keys_ref[t * rt:(t + 1) * rt, pl.ds(c0, kb)]
                for i in range(n_lane_tiles):
                    x = blk[:, i * LANES:(i + 1) * LANES] ^ fb[t]
                    acc = jnp.minimum(acc, jnp.where(x >= bb[t], x, INT_MAX))
                return acc

            acc = lax.fori_loop(0, nkb, body, jnp.full((rt, LANES), INT_MAX, I32))
            return jnp.min(acc, axis=1, keepdims=True)

        col = jnp.concatenate([tile_min(t) for t in tiles], axis=0)
        return jnp.transpose(jnp.broadcast_to(col, (tq, 8)))

    def probe(st, forced=None, use_forced=None):
        it, _, _, lo, hi, flo, fhi, glo, ghi, side, done = st
        v_lo, v_hi = _val_of(lo), _val_of(hi)
        v_model = (v_lo * ghi - v_hi * glo) / (ghi - glo)
        inside = flo - fhi
        v_even = v_lo + (v_hi - v_lo) * ((flo - kf + 0.5) / (inside + 1.0))
        c_interp = _key_of(jnp.where(inside <= 64.0, v_even, v_model))
        c_mid = jnp.right_shift(lo, 1) + jnp.right_shift(hi, 1) + (lo & hi & 1)
        cand = jnp.where(it >= n_interp, c_mid, c_interp)
        if forced is not None:
            cand = jnp.where(use_forced, forced, cand)
        cand = jnp.minimum(jnp.maximum(cand, lo + 1), hi - 1)
        c = count_ge([cand])[0]
        active = done < 0.5
        to_lo = jnp.logical_and(active, c >= kf)
        to_hi = jnp.logical_and(active, c < kf)
        gc = glog(c)
        ghi_n = jnp.where(to_hi, gc, jnp.where(jnp.logical_and(to_lo, side > 0.0), ghi * 0.5, ghi))
        glo_n = jnp.where(to_lo, gc, jnp.where(jnp.logical_and(to_hi, side < 0.0), glo * 0.5, glo))
        side = jnp.where(to_lo, 1.0, jnp.where(to_hi, -1.0, side))
        lo = jnp.where(to_lo, cand, lo)
        flo = jnp.where(to_lo, c, flo)
        hi = jnp.where(to_hi, cand, hi)
        fhi = jnp.where(to_hi, c, fhi)
        fin = jnp.logical_or(flo == kf, hi - 1 == lo)
        done = jnp.where(fin, 1.0, done)
        near = jnp.logical_or(flo - kf <= 1.0, kf - fhi <= 1.0)
        n_far = jnp.sum(jnp.where(near, 0.0, 1.0 - done))
        return it + 1, n_far, jnp.sum(1.0 - done), lo, hi, flo, fhi, glo_n, ghi_n, side, done

    n_pass_cap = n_interp + 34
    near0 = jnp.logical_or(flo0 - kf <= 1.0, kf - fhi0 <= 1.0)
    st = (jnp.int32(0), jnp.sum(jnp.where(near0, 0.0, 1.0 - done0)), jnp.sum(1.0 - done0),
          lo0, hi0, flo0, fhi0, glog(flo0), glog(fhi0), jnp.zeros((8, tq), F32), done0)
    st = lax.while_loop(lambda st: jnp.logical_and(st[0] < n_pass_cap, st[1] > 0.0), probe, st)
    lo, hi, flo, fhi, done = st[3], st[4], st[5], st[6], st[10]
    active = done < 0.5
    drop_one = jnp.logical_and(active, flo - kf <= 1.0)
    add_one = jnp.logical_and(jnp.logical_and(active, jnp.logical_not(drop_one)), kf - fhi <= 1.0)
    flip = jnp.where(add_one, -1, 0)
    edge = peel_min(as_col(jnp.where(add_one, -hi, lo)), as_col(flip))
    forced = jnp.where(add_one, edge ^ flip, edge + 1)
    st = probe(st, forced, jnp.logical_or(drop_one, add_one))
    st = lax.while_loop(lambda st: jnp.logical_and(st[0] < n_pass_cap + 2, st[2] > 0.0), probe, st)
    thr = as_col(st[3])
    tie_rows = jnp.logical_and(st[5] > kf, jnp.logical_not(few))
    need = as_col(jnp.where(tie_rows, kf - st[6], float(2 * seq)))

    run_ref[...] = jnp.zeros(run_ref.shape, F32)
    rr = lax.broadcasted_iota(I32, (LANES, 2 * LANES), 0)
    cc = lax.broadcasted_iota(I32, (LANES, 2 * LANES), 1)
    tri_ref[...] = jnp.where(jnp.logical_or(rr <= cc, cc >= LANES), 1.0, 0.0).astype(BF16)

    rep = ATT_HEADS // ATT_KV_HEADS
    rows = rep * tq
    rc = 32
    for g in range(ATT_KV_HEADS):
        for r in range(rep):
            hd = g * rep + r
            qg_ref[g, r * tq:(r + 1) * tq, :] = q_ref[:, hd * 128:(hd + 1) * 128]
    acc_ref[...] = jnp.zeros(acc_ref.shape, F32)
    l_ref[...] = jnp.zeros(l_ref.shape, F32)
    m_ref[...] = jnp.full(m_ref.shape, NEG_BIG, F32)

    def select_block(j, buf, count_ties):
        c0 = pl.multiple_of(j * kb, kb)
        run = run_ref[...]
        left = jnp.broadcast_to(need - run, (tq, LANES))
        for i in range(n_lane_tiles):
            keys = keys_ref[:, pl.ds(c0 + i * LANES, LANES)]
            tied = keys == thr
            cnt = _dot(jnp.where(tied, 1.0, 0.0).astype(BF16), tri_ref[...])
            ok = jnp.logical_and(tied, cnt[:, :LANES] <= left)
            bias_ref[buf, :, i * LANES:(i + 1) * LANES] = jnp.where(keys > thr, 0.0,
                                                                      jnp.where(ok, 0.0, NEG_BIG))
            left = left - cnt[:, LANES:]
        run_ref[...] = jnp.where(count_ties, need - left[:, :1], run)

    select_block(0, 0, True)

    def attend_block(j, carry):
        c0 = pl.multiple_of(j * kb, kb)
        buf = j & 1
        for g in range(ATT_KV_HEADS):
            for a0 in range(0, kb, 256):
                s_ref[g, :, a0:a0 + 256] = _dot(qg_ref[g], kt_ref[g * 128:(g + 1) * 128, pl.ds(c0 + a0, 256)])
        for g in range(ATT_KV_HEADS):
            for r0 in range(0, rows, rc):
                rs = slice(r0, r0 + rc)
                b0 = r0 % tq
                s = s_ref[g, rs, :] + bias_ref[buf, pl.ds(b0, rc), :]
                m_prev = m_ref[g, rs, :]
                m_next = jnp.maximum(m_prev, jnp.max(s, axis=1, keepdims=True))
                p = jnp.exp2(s - jnp.concatenate([m_next] * n_lane_tiles, axis=1))
                alpha = jnp.exp2(m_prev - m_next)
                l_ref[g, rs, :] = alpha * l_ref[g, rs, :] + jnp.sum(p, axis=1, keepdims=True)
                m_ref[g, rs, :] = m_next
                acc_ref[g, rs, :] = acc_ref[g, rs, :] * alpha
                p_ref[g, rs, :] = p.astype(BF16)
            acc_ref[g] += _dot(p_ref[g], v_ref[pl.ds(c0, kb), g * 128:(g + 1) * 128])
        select_block(jnp.minimum(j + 1, nkb - 1), 1 - buf, j + 1 < nkb)
        return carry

    lax.fori_loop(0, nkb, attend_block, 0)

    for g in range(ATT_KV_HEADS):
        out = acc_ref[g] / l_ref[g]
        for r in range(rep):
            hd = g * rep + r
            o_ref[:, hd * 128:(hd + 1) * 128] = out[r * tq:(r + 1) * tq]


def _dsa(q, qi, wi, kt, v, kit, topk, tq, kb):
    b, s, _ = q.shape
    rep = ATT_HEADS // ATT_KV_HEADS
    qspec = lambda c: pl.BlockSpec((None, tq, c), lambda bi, i: (bi, i, 0))
    per_batch = lambda r, c: pl.BlockSpec((None, r, c), lambda bi, i: (bi, 0, 0),
                                          pipeline_mode=pl.Buffered(1))
    return pl.pallas_call(
        functools.partial(_dsa_kernel, tq=tq, kb=kb, topk=topk, seq=s),
        grid=(b, s // tq),
        in_specs=[qspec(1024), qspec(512), qspec(128),
                  per_batch(256, s), per_batch(s, 256), per_batch(128, s)],
        out_specs=qspec(1024),
        out_shape=jax.ShapeDtypeStruct((b, s, 1024), F32),
        scratch_shapes=[pltpu.VMEM((tq, s), I32),
                        pltpu.VMEM((IDX_HEADS * tq, 128), BF16),
                        pltpu.VMEM((ATT_KV_HEADS, rep * tq, 128), BF16),
                        pltpu.VMEM((2, tq, kb), F32),
                        pltpu.VMEM((ATT_KV_HEADS, rep * tq, kb), F32),
                        pltpu.VMEM((ATT_KV_HEADS, rep * tq, kb), BF16),
                        pltpu.VMEM((ATT_KV_HEADS, rep * tq, 128), F32),
                        pltpu.VMEM((ATT_KV_HEADS, rep * tq, 128), F32),
                        pltpu.VMEM((ATT_KV_HEADS, rep * tq, 128), F32),
                        pltpu.VMEM((tq, 1), F32),
                        pltpu.VMEM((LANES, 2 * LANES), BF16)],
        compiler_params=_cparams(("parallel", "arbitrary")),
        name="dsa_attention",
    )(q, qi, wi, kt, v, kit)


def _softplus(x):
    return jnp.maximum(x, 0.0) + jnp.log(1.0 + jnp.exp(-jnp.abs(x)))


def _gdn_kernel(qkv_ref, ab_ref, abt_ref, cw_ref, ac_ref, dc_ref, ar_ref, dr_ref, gn_ref,
                y_ref, xbuf_ref, act_ref, state_ref, *, tb):
    ck = tb
    n_chunks = tb // ck

    @pl.when(pl.program_id(1) == 0)
    def _():
        xbuf_ref[0:8, :] = jnp.zeros((8, xbuf_ref.shape[1]), F32)
        state_ref[...] = jnp.zeros(state_ref.shape, F32)

    xbuf_ref[8:8 + tb, :] = qkv_ref[...]
    for c in range(xbuf_ref.shape[1] // LANES):
        sl = slice(c * LANES, (c + 1) * LANES)
        y = xbuf_ref[5:5 + tb, sl] * cw_ref[0:1, sl]
        for j in range(1, CONV_K):
            y = y + xbuf_ref[5 + j:5 + j + tb, sl] * cw_ref[j:j + 1, sl]
        act_ref[:, sl] = y * (1.0 / (1.0 + jnp.exp(-y)))
    xbuf_ref[0:8, :] = xbuf_ref[tb:tb + 8, :]

    ab = ab_ref[...]
    g_col = -jnp.exp(ac_ref[...]) * _softplus(ab + dc_ref[...])
    beta_col = 1.0 / (1.0 + jnp.exp(-ab))
    g_row = -jnp.exp(ar_ref[...]) * _softplus(abt_ref[0:8, :] + dr_ref[...])

    r = lax.broadcasted_iota(I32, (tb, tb), 0)
    c = lax.broadcasted_iota(I32, (tb, tb), 1)
    chunk_shift = ck.bit_length() - 1
    same = jnp.right_shift(r, chunk_shift) == jnp.right_shift(c, chunk_shift)
    lower = jnp.logical_and(same, r >= c)
    strict = jnp.logical_and(same, r > c)
    lower_m = jnp.where(lower, 1.0, 0.0).astype(BF16)
    upper_m = jnp.where(jnp.logical_and(same, r <= c), 1.0, 0.0).astype(BF16)
    same_m = jnp.where(same, 1.0, 0.0).astype(BF16)
    eye = jnp.where(r == c, 1.0, 0.0)
    n_levels = ck.bit_length() - 1
    lvl_subs = []
    for lvl in range(n_levels):
        lvl_subs.append(jnp.logical_and(jnp.right_shift(r, lvl + 1) == jnp.right_shift(c, lvl + 1),
                                        jnp.logical_and((jnp.right_shift(r, lvl) & 1) == 1,
                                                        (jnp.right_shift(c, lvl) & 1) == 0)))
    lvl_masks = [jnp.where(sub, 1.0, 0.0).astype(BF16) for sub in lvl_subs]

    gparts = _split3(g_col)
    gc_col = _dot(lower_m, gparts[0]) + (_dot(lower_m, gparts[1]) + _dot(lower_m, gparts[2]))
    gl_col = _dot(same_m, gparts[0]) + (_dot(same_m, gparts[1]) + _dot(same_m, gparts[2]))
    rparts = _split3(g_row)
    gc_row = _dot(rparts[0], upper_m) + (_dot(rparts[1], upper_m) + _dot(rparts[2], upper_m))

    gn = gn_ref[...]
    hg = 4
    for h0 in range(0, GDN_HEADS, hg):
        heads = range(h0, h0 + hg)
        q_l, k_l, kbeta_l, vbeta_l, decay_l, gc_l, gl_l, m16_l, d_l = ([] for _ in range(9))
        for h in heads:
            q = act_ref[:, h * 128:(h + 1) * 128]
            k = act_ref[:, 1024 + h * 128:1024 + (h + 1) * 128]
            v = act_ref[:, 2048 + h * 128:2048 + (h + 1) * 128]
            q = q * lax.rsqrt(jnp.sum(q * q, axis=-1, keepdims=True) + EPS) * (GDN_DK ** -0.5)
            k = k * lax.rsqrt(jnp.sum(k * k, axis=-1, keepdims=True) + EPS)
            gc = gc_col[:, h:h + 1]
            beta = beta_col[:, 8 + h:9 + h]
            diff = gc - gc_row[h:h + 1, :]
            decay = jnp.where(lower, jnp.exp(jnp.where(lower, diff, 0.0)), 0.0)
            kbeta = k * beta
            nm = jnp.where(strict, -(_dot_nt(kbeta.astype(BF16), k.astype(BF16)) * decay), 0.0)
            q_l.append(q); k_l.append(k); kbeta_l.append(kbeta); vbeta_l.append(v * beta)
            decay_l.append(decay); gc_l.append(gc); gl_l.append(gl_col[:, h:h + 1])
            m16_l.append(nm.astype(BF16)); d_l.append(eye + jnp.where(lvl_subs[0], nm, 0.0))
        for lvl in range(1, n_levels):
            for i in range(hg):
                dh = d_l[i].astype(BF16)
                lh = m16_l[i] * lvl_masks[lvl]
                th = _dot(dh, lh).astype(BF16)
                d_l[i] = d_l[i] + _dot(th, dh)
        u_l, w_l, attn_l, qd_l, kd_l = [], [], [], [], []
        for i in range(hg):
            egc = jnp.exp(gc_l[i])
            rhs = jnp.concatenate([vbeta_l[i], kbeta_l[i] * egc], axis=1)
            sol = _dot(d_l[i].astype(BF16), rhs.astype(BF16))
            u_l.append(sol[:, :128])
            w_l.append(sol[:, 128:].astype(BF16))
            k16 = k_l[i].astype(BF16)
            attn_l.append(jnp.where(lower, _dot_nt(q_l[i].astype(BF16), k16) * decay_l[i], 0.0).astype(BF16))
            qd_l.append((q_l[i] * egc).astype(BF16))
            kd_l.append(k_l[i] * jnp.exp(gl_l[i] - gc_l[i]))
        st_l = [state_ref[h] for h in heads]
        outs = [[] for _ in range(hg)]
        for ci in range(n_chunks):
            rc = slice(ci * ck, (ci + 1) * ck)
            for i in range(hg):
                st16 = st_l[i].astype(BF16)
                v_new = u_l[i][rc] - _dot(w_l[i][rc], st16)
                v_new16 = v_new.astype(BF16)
                outs[i].append(_dot(qd_l[i][rc], st16) + _dot(attn_l[i][rc, rc], v_new16))
                kd_t = jnp.transpose(kd_l[i][rc]).astype(BF16)
                st_l[i] = (st_l[i] * jnp.exp(gl_l[i][ci * ck:ci * ck + 1])
                           + _dot(kd_t, v_new16))
        for i, h in enumerate(heads):
            state_ref[h] = st_l[i]
            y_ref[:, h * 128:(h + 1) * 128] = _rms(jnp.concatenate(outs[i], axis=0), gn)


def _gdn(qkv, ab, abt, conv_w, ac, dc, ar, dr, gn, tb):
    b, s, _ = qkv.shape
    blk = lambda c: pl.BlockSpec((None, tb, c), lambda bi, i: (bi, i, 0))
    return pl.pallas_call(
        functools.partial(_gdn_kernel, tb=tb),
        grid=(b, s // tb),
        in_specs=[blk(3072), blk(128), pl.BlockSpec((None, 16, tb), lambda bi, i: (bi, 0, i)),
                  _const_spec(conv_w.shape), _const_spec(ac.shape), _const_spec(dc.shape),
                  _const_spec(ar.shape), _const_spec(dr.shape), _const_spec(gn.shape)],
        out_specs=blk(1024),
        out_shape=jax.ShapeDtypeStruct((b, s, 1024), F32),
        scratch_shapes=[pltpu.VMEM((tb + 8, 3072), F32),
                        pltpu.VMEM((tb, 3072), F32),
                        pltpu.VMEM((GDN_HEADS, GDN_DK, GDN_DV), F32)],
        compiler_params=_cparams(("parallel", "arbitrary")),
        name="gated_delta",
    )(qkv, ab, abt, conv_w, ac, dc, ar, dr, gn)


def _sigmoid(x):
    return 1.0 / (1.0 + jnp.exp(-x))


def _merge_kernel(x_ref, ya_ref, yb_ref, gpre_ref, wg_ref, wo_ref, gpost_ref, o_ref):
    x = x_ref[...]
    h = _rms(x, gpre_ref[...]).astype(BF16)
    gates = _dot(h, wg_ref[...])
    y = _sigmoid(gates[:, :1024]) * ya_ref[...] + _sigmoid(gates[:, 1024:]) * yb_ref[...]
    z = _dot(y.astype(BF16), wo_ref[...])
    o_ref[...] = x + _rms(z, gpost_ref[...])


def _merge(x2, ya, yb, gpre, wg, wo, gpost, tm):
    n, d = x2.shape
    row = pl.BlockSpec((tm, d), lambda i: (i, 0))
    return pl.pallas_call(
        _merge_kernel,
        grid=(n // tm,),
        in_specs=[row, row, row, _const_spec((1, d)), _const_spec(wg.shape),
                  _const_spec(wo.shape), _const_spec((1, d))],
        out_specs=row,
        out_shape=jax.ShapeDtypeStruct((n, d), F32),
        compiler_params=_cparams(("parallel",)),
        name="merge_out",
    )(x2, ya, yb, gpre, wg, wo, gpost)


def _mem_kv_kernel(m_ref, g_ref, w_ref, kv_ref):
    h = _rms(m_ref[...], g_ref[...]).astype(BF16)
    kv_ref[...] = _dot(h, w_ref[...]).astype(BF16)


def _mem_kv(mem2, g, w):
    n, d = mem2.shape
    return pl.pallas_call(
        _mem_kv_kernel,
        out_shape=jax.ShapeDtypeStruct((n, w.shape[1]), BF16),
        compiler_params=pltpu.CompilerParams(vmem_limit_bytes=VMEM_LIMIT),
        name="mem_kv",
    )(mem2, g, w)


def _xattn_kernel(x_ref, kt_ref, v_ref, gpre_ref, wq_ref, wo_ref, gpost_ref, o_ref):
    x = x_ref[...]
    d = x.shape[1]
    hd = d // XATT_HEADS
    h = _rms(x, gpre_ref[...]).astype(BF16)
    q = (_dot(h, wq_ref[...]) * (hd ** -0.5)).astype(BF16)
    heads = []
    for i in range(XATT_HEADS):
        sl = slice(i * hd, (i + 1) * hd)
        s = _dot(q[:, sl], kt_ref[sl, :])
        p = jnp.exp(s - jnp.max(s, axis=1, keepdims=True))
        o = _dot(p.astype(BF16), v_ref[:, sl]) / jnp.sum(p, axis=1, keepdims=True)
        heads.append(o.astype(BF16))
    z = _dot(jnp.concatenate(heads, axis=1), wo_ref[...])
    o_ref[...] = x + _rms(z, gpost_ref[...])


def _xattn(x, kt, v, gpre, wq, wo, gpost, tm):
    b, s, d = x.shape
    n_mem = v.shape[1]
    row = pl.BlockSpec((None, tm, d), lambda bi, i: (bi, i, 0))
    return pl.pallas_call(
        _xattn_kernel,
        grid=(b, s // tm),
        in_specs=[row,
                  pl.BlockSpec((None, d, n_mem), lambda bi, i: (bi, 0, 0)),
                  pl.BlockSpec((None, n_mem, d), lambda bi, i: (bi, 0, 0)),
                  _const_spec((1, d)), _const_spec(wq.shape), _const_spec(wo.shape),
                  _const_spec((1, d))],
        out_specs=row,
        out_shape=jax.ShapeDtypeStruct((b, s, d), F32),
        compiler_params=_cparams(("parallel", "parallel")),
        name="mem_xattn",
    )(x, kt, v, gpre, wq, wo, gpost)


def _ffn_kernel(x_ref, gpre_ref, wgu_ref, wd_ref, gpost_ref, o_ref, *, ff, fc):
    x = x_ref[...]
    h = _rms(x, gpre_ref[...]).astype(BF16)
    z = jnp.zeros(x.shape, F32)
    for c0 in range(0, ff, fc):
        gate = _dot(h, wgu_ref[:, c0:c0 + fc])
        up = _dot(h, wgu_ref[:, ff + c0:ff + c0 + fc])
        act = (gate * _sigmoid(gate) * up).astype(BF16)
        z = z + _dot(act, wd_ref[c0:c0 + fc, :])
    o_ref[...] = x + _rms(z, gpost_ref[...])


def _ffn(x2, gpre, wgu, wd, gpost, tm):
    n, d = x2.shape
    ff = wd.shape[0]
    fc = ff // 2 if (ff // 2) % LANES == 0 else ff
    row = pl.BlockSpec((tm, d), lambda i: (i, 0))
    return pl.pallas_call(
        functools.partial(_ffn_kernel, ff=ff, fc=fc),
        grid=(n // tm,),
        in_specs=[row, _const_spec((1, d)), _const_spec(wgu.shape), _const_spec(wd.shape),
                  _const_spec((1, d))],
        out_specs=row,
        out_shape=jax.ShapeDtypeStruct((n, d), F32),
        compiler_params=_cparams(("parallel",)),
        name="swiglu",
    )(x2, gpre, wgu, wd, gpost)


def _rope_tables(positions):
    pos = positions.astype(F32).reshape(-1)[:, None]
    lane = jnp.arange(LANES)

    def tables(dim):
        half = dim // 2
        in_head = lane % dim
        inv_freq = ROPE_THETA ** (-(in_head % half).astype(F32) * 2.0 / dim)
        ang = pos * inv_freq
        return jnp.cos(ang), jnp.where(in_head < half, -1.0, 1.0) * jnp.sin(ang)

    return tables(ATT_HEAD_DIM) + tables(IDX_DIM)


def _pick_tile(n, pref):
    t = min(pref, n)
    while n % t:
        t //= 2
    return t


def kernel(x, mem, positions, norm_mix_pre, w_in, conv_w, a_log, dt_bias, gdn_norm, w_out,
           norm_mix_post, norm_x_pre, norm_mem, w_xq, w_xkv, w_xo, norm_x_post,
           norm_ffn_pre, w_gu, w_down, norm_ffn_post):
    b, s, d = x.shape
    n = b * s
    n_mem = mem.shape[1]
    depth = w_in.shape[0]
    topk = min(TOPK_MAX, s // 4)
    tm = _pick_tile(s, 512)
    tq = _pick_tile(s, 256)
    kb = _pick_tile(s, 512)
    tb = _pick_tile(s, 256)

    ca, sa, ci, si = (t.reshape(b, s, 128) for t in _rope_tables(positions))

    row1 = lambda v: v.reshape(1, -1).astype(F32)

    x2 = x.reshape(n, d)
    for l in range(depth):
        w_att, w_gdn, w_gate = _prep_w_in(w_in, l)

        g_pre = row1(norm_mix_pre[l])
        x3 = x2.reshape(b, s, d)
        q, kt, v, qi, kit, wi = _proj_att(x3, g_pre, w_att, ca, sa, ci, si, tm)
        ya = _dsa(q, qi, wi, kt, v, kit, topk, tq, kb)

        qkv, ab, abt = _proj_gdn(x3, g_pre, w_gdn, tm)
        pad_lanes = lambda vec: jnp.pad(vec.astype(F32), (0, 128 - GDN_HEADS)).reshape(1, 128)
        bcast_rows = lambda vec: jnp.broadcast_to(vec.astype(F32)[:, None], (GDN_HEADS, tb))
        yb = _gdn(qkv, ab, abt, conv_w[l].astype(F32),
                  pad_lanes(a_log[l]), pad_lanes(dt_bias[l]), bcast_rows(a_log[l]), bcast_rows(dt_bias[l]),
                  row1(gdn_norm[l]), tb)

        x2 = _merge(x2, ya.reshape(n, 1024), yb.reshape(n, 1024), g_pre, w_gate,
                    w_out[l].astype(BF16), row1(norm_mix_post[l]), tm)

        kv = _mem_kv(mem.reshape(b * n_mem, d), row1(norm_mem[l]), w_xkv[l].astype(BF16))
        kv = kv.reshape(b, n_mem, 2 * d)
        kt_mem = jnp.swapaxes(kv[:, :, :d], 1, 2)
        x2 = _xattn(x2.reshape(b, s, d), kt_mem, kv[:, :, d:], row1(norm_x_pre[l]),
                    w_xq[l].astype(BF16), w_xo[l].astype(BF16), row1(norm_x_post[l]), tm).reshape(n, d)

        x2 = _ffn(x2, row1(norm_ffn_pre[l]), w_gu[l].astype(BF16), w_down[l].astype(BF16),
                  row1(norm_ffn_post[l]), tm)
    return x2.reshape(b, s, d)
```

```python
import functools

import jax
import jax.numpy as jnp
import numpy as np
from jax import lax
from jax.experimental import pallas as pl
from jax.experimental.pallas import tpu as pltpu

EPS = 1e-6
ROPE_THETA = 10000.0
ATT_HEADS = 8
ATT_KV_HEADS = 2
ATT_HEAD_DIM = 128
IDX_HEADS = 4
IDX_DIM = 64
TOPK_MAX = 256
GDN_HEADS = 8
GDN_DK = 128
GDN_DV = 128
CONV_K = 4
XATT_HEADS = 4

LANES = 128
VMEM_LIMIT = 56 * 1024 * 1024

F32 = jnp.float32
BF16 = jnp.bfloat16
I32 = jnp.int32

NEG_BIG = -1e30
LOG2E = 1.4426950408889634
INT_MAX = 2147483647
NEG_INF_KEY = -2139095041


def _cparams(sem):
    return pltpu.CompilerParams(dimension_semantics=sem, vmem_limit_bytes=VMEM_LIMIT)


def _const_spec(shape):
    nd = len(shape)
    return pl.BlockSpec(shape, lambda *_: (0,) * nd, pipeline_mode=pl.Buffered(1))


def _rms(x, g):
    return x * lax.rsqrt(jnp.mean(x * x, axis=-1, keepdims=True) + EPS) * g


def _dot(a, b):
    return jnp.dot(a, b, preferred_element_type=F32)


def _dot_nt(a, b):
    return lax.dot_general(a, b, (((1,), (1,)), ((), ())), preferred_element_type=F32)


def _split3(a):
    a1 = a.astype(BF16)
    r1 = a - a1.astype(F32)
    a2 = r1.astype(BF16)
    a3 = (r1 - a2.astype(F32)).astype(BF16)
    return a1, a2, a3


A_Q, A_KV, I_Q, I_K, I_W, B_QKV = 1024, 256, 256, 64, 4, 3072
O_IQ = A_Q + 2 * A_KV
O_IK = O_IQ + I_Q
O_IW = O_IK + I_K
O_B = O_IW + I_W
O_BA = O_B + B_QKV
O_GA = O_BA + 2 * GDN_HEADS
IN_COLS = O_GA + 2 * 1024


def _prep_w_in_kernel(w_ref, att_ref, gdn_ref, gate_ref):
    w = w_ref[...]
    zeros = lambda c: jnp.zeros((w.shape[0], c), F32)
    att = [w[:, :O_IQ]]
    for h in range(IDX_HEADS):
        att += [w[:, O_IQ + h * IDX_DIM:O_IQ + (h + 1) * IDX_DIM], zeros(128 - IDX_DIM)]
    att += [w[:, O_IK:O_IW], zeros(128 - I_K), w[:, O_IW:O_B], zeros(128 - I_W)]
    att_ref[...] = jnp.concatenate(att, axis=1).astype(BF16)
    gdn_ref[...] = jnp.concatenate([w[:, O_B:O_GA], zeros(128 - 2 * GDN_HEADS)], axis=1).astype(BF16)
    gate_ref[...] = w[:, O_GA:].astype(BF16)


def _prep_w_in(w_in, layer):
    _, d, cols = w_in.shape
    assert cols == IN_COLS
    tr = 128
    widths = (O_IQ + IDX_HEADS * 128 + 256, B_QKV + 128, 2048)
    return pl.pallas_call(
        _prep_w_in_kernel,
        grid=(d // tr,),
        in_specs=[pl.BlockSpec((None, tr, cols), lambda i: (layer, i, 0))],
        out_specs=[pl.BlockSpec((tr, c), lambda i: (i, 0)) for c in widths],
        out_shape=[jax.ShapeDtypeStruct((d, c), BF16) for c in widths],
        compiler_params=_cparams(("parallel",)),
        name="prep_w_in",
    )(w_in)


def _rope128(x, cos, sin_signed):
    return x * cos + pltpu.roll(x, 64, 1) * sin_signed


def _rope64(x, cos, sin_signed, first_half):
    partner = jnp.where(first_half, pltpu.roll(x, 96, 1), pltpu.roll(x, 32, 1))
    return x * cos + partner * sin_signed


def _proj_att_kernel(x_ref, g_ref, w_ref, ca_ref, sa_ref, ci_ref, si_ref,
                     q_ref, kt_ref, v_ref, qi_ref, kit_ref, wi_ref):
    h = _rms(x_ref[...], g_ref[...]).astype(BF16)
    p = _dot(h, w_ref[...])
    ca, sa = ca_ref[...], sa_ref[...]
    ci, si = ci_ref[...], si_ref[...]
    q_scale = ATT_HEAD_DIM ** -0.5 * LOG2E
    for hd in range(ATT_HEADS):
        sl = slice(hd * 128, (hd + 1) * 128)
        q_ref[:, sl] = (_rope128(p[:, sl], ca, sa) * q_scale).astype(BF16)
    for g in range(ATT_KV_HEADS):
        k_g = _rope128(p[:, A_Q + g * 128:A_Q + (g + 1) * 128], ca, sa)
        kt_ref[g * 128:(g + 1) * 128, :] = jnp.transpose(k_g).astype(BF16)
    v_ref[...] = p[:, A_Q + A_KV:O_IQ].astype(BF16)
    lane = lax.broadcasted_iota(I32, ci.shape, 1)
    first_half = (lane & 63) < 32
    idx_scale = IDX_DIM ** -0.5
    for hd in range(IDX_HEADS):
        off = O_IQ + hd * 128
        qi_ref[:, hd * 128:(hd + 1) * 128] = (
            _rope64(p[:, off:off + 128], ci, si, first_half) * idx_scale).astype(BF16)
    o_ki = O_IQ + IDX_HEADS * 128
    kit_ref[...] = jnp.transpose(_rope64(p[:, o_ki:o_ki + 128], ci, si, first_half)).astype(BF16)
    wi_ref[...] = p[:, o_ki + 128:o_ki + 256] * (IDX_HEADS ** -0.5)


def _proj_att(x, g, w, ca, sa, ci, si, tm):
    b, s, d = x.shape
    row = lambda c: pl.BlockSpec((None, tm, c), lambda bi, i: (bi, i, 0))
    col = lambda r: pl.BlockSpec((None, r, tm), lambda bi, i: (bi, 0, i))
    sds = jax.ShapeDtypeStruct
    return pl.pallas_call(
        _proj_att_kernel,
        grid=(b, s // tm),
        in_specs=[row(d), _const_spec((1, d)), _const_spec(w.shape),
                  row(128), row(128), row(128), row(128)],
        out_specs=[row(1024), col(256), row(256), row(512), col(128), row(128)],
        out_shape=[sds((b, s, 1024), BF16), sds((b, 256, s), BF16), sds((b, s, 256), BF16),
                   sds((b, s, 512), BF16), sds((b, 128, s), BF16), sds((b, s, 128), F32)],
        compiler_params=_cparams(("parallel", "parallel")),
        name="proj_att",
    )(x, g, w, ca, sa, ci, si)


def _proj_gdn_kernel(x_ref, g_ref, w_ref, qkv_ref, ab_ref, abt_ref):
    h = _rms(x_ref[...], g_ref[...]).astype(BF16)
    p = _dot(h, w_ref[...])
    qkv_ref[...] = p[:, :B_QKV]
    ab = p[:, B_QKV:B_QKV + 128]
    ab_ref[...] = ab
    abt_ref[...] = jnp.transpose(ab)[:16, :]


def _proj_gdn(x, g, w, tm):
    b, s, d = x.shape
    row = lambda c: pl.BlockSpec((None, tm, c), lambda bi, i: (bi, i, 0))
    sds = jax.ShapeDtypeStruct
    return pl.pallas_call(
        _proj_gdn_kernel,
        grid=(b, s // tm),
        in_specs=[row(d), _const_spec((1, d)), _const_spec(w.shape)],
        out_specs=[row(3072), row(128), pl.BlockSpec((None, 16, tm), lambda bi, i: (bi, 0, i))],
        out_shape=[sds((b, s, 3072), F32), sds((b, s, 128), F32), sds((b, 16, s), F32)],
        compiler_params=_cparams(("parallel", "parallel")),
        name="proj_gdn",
    )(x, g, w)


def _key_of(v):
    b = pltpu.bitcast(v, I32)
    return jnp.where(b < 0, b ^ 0x7FFFFFFF, b)


def _val_of(k):
    return pltpu.bitcast(jnp.where(k < 0, k ^ 0x7FFFFFFF, k), F32)


def _dsa_kernel(q_ref, qi_ref, wi_ref, kt_ref, v_ref, kit_ref, o_ref,
                keys_ref, qis_ref, qg_ref, bias_ref, s_ref, p_ref, acc_ref, m_ref, l_ref, run_ref, tri_ref,
                *, tq, kb, topk, seq):
    t0 = pl.program_id(1) * tq
    nkb = (t0 + tq + kb - 1) // kb
    n_lane_tiles = kb // LANES
    row_ids = t0 + lax.broadcasted_iota(I32, (tq, kb), 0)
    col_iota = lax.broadcasted_iota(I32, (tq, kb), 1)
    kf = float(topk)

    for h in range(IDX_HEADS):
        qis_ref[h * tq:(h + 1) * tq, :] = qi_ref[:, h * 128:(h + 1) * 128]
    wi = wi_ref[...]
    w_heads = [wi[:, h:h + 1] for h in range(IDX_HEADS)]

    def score_block(j, carry, masked):
        mx, mn = carry
        c0 = pl.multiple_of(j * kb, kb)
        halves = []
        for a0 in range(0, kb, 256):
            lg = _dot(qis_ref[...], kit_ref[:, pl.ds(c0 + a0, 256)])
            part = jnp.zeros((tq, 256), F32)
            for h in range(IDX_HEADS):
                part = part + w_heads[h] * jnp.maximum(lg[h * tq:(h + 1) * tq], 0.0)
            halves.append(part)
        sc = jnp.concatenate(halves, axis=1)
        sc = jnp.where(sc == 0.0, 0.0, sc)
        for i in range(n_lane_tiles):
            mx = jnp.maximum(mx, sc[:, i * LANES:(i + 1) * LANES])
            mn = jnp.minimum(mn, sc[:, i * LANES:(i + 1) * LANES])
        if masked:
            sc = jnp.where(c0 + col_iota <= row_ids, sc, -jnp.inf)
        keys_ref[:, pl.ds(c0, kb)] = _key_of(sc)
        return mx, mn

    n_full = t0 // kb
    ext = (jnp.full((tq, LANES), -jnp.inf, F32), jnp.full((tq, LANES), jnp.inf, F32))

    done_blocks = 0
    for group in (4, 2, 1):
        def score_group(jj, carry, group=group, first=done_blocks):
            for u in range(group):
                carry = score_block(first + group * jj + u, carry, False)
            return carry

        n_groups = (n_full - done_blocks) // group
        ext = lax.fori_loop(0, n_groups, score_group, ext)
        done_blocks = done_blocks + group * n_groups
    mx, mn = lax.fori_loop(n_full, nkb, functools.partial(score_block, masked=True), ext)
    row_max = jnp.max(mx, axis=1, keepdims=True)
    row_min = jnp.min(mn, axis=1, keepdims=True)

    n_row_tiles = tq // LANES if tq % LANES == 0 else 1
    rt = tq // n_row_tiles

    def as_col(row):
        return jnp.transpose(row)[:, :1]

    def count_ge(cands):
        nc = len(cands)
        tiles = range(n_row_tiles)
        cb = [[c if isinstance(c, int) else
               jnp.transpose(jnp.broadcast_to(c[:1, t * rt:(t + 1) * rt], (LANES, rt)))
               for t in tiles] for c in cands]

        def tile_counts(t):
            def body(j, accs):
                c0 = pl.multiple_of(j * kb, kb)
                accs = list(accs)
                blk = keys_ref[t * rt:(t + 1) * rt, pl.ds(c0, kb)]
                for i in range(n_lane_tiles):
                    for k in range(nc):
                        accs[k] = accs[k] + jnp.where(blk[:, i * LANES:(i + 1) * LANES] >= cb[k][t], 1.0, 0.0)
                return tuple(accs)

            z = jnp.zeros((rt, LANES), F32)
            return lax.fori_loop(0, nkb, body, (z,) * nc)

        accs = [tile_counts(t) for t in tiles]
        row_sums = lambda a: jnp.broadcast_to(jnp.sum(jnp.transpose(a), axis=0, keepdims=True), (8, rt))
        return [jnp.concatenate([row_sums(accs[t][k]) for t in tiles], axis=1) for k in range(nc)]

    as_row = lambda col: jnp.transpose(jnp.broadcast_to(col, (tq, 8)))
    n_visible = (t0 + lax.broadcasted_iota(I32, (8, tq), 1) + 1).astype(F32)
    few = n_visible < kf
    f_pos, f_zero = count_ge([1, 0])
    above = f_pos >= kf
    below = f_zero < kf
    at_zero = jnp.logical_and(jnp.logical_not(above), jnp.logical_not(below))
    key_min = _key_of(as_row(row_min))
    key_max = _key_of(as_row(row_max))
    lo0 = jnp.where(few, NEG_INF_KEY + 1, jnp.where(above, 1, jnp.where(below, key_min, 0)))
    flo0 = jnp.where(few, n_visible, jnp.where(above, f_pos, jnp.where(below, n_visible, f_zero)))
    hi0 = jnp.where(above, jnp.minimum(key_max, 2147483646) + 1, jnp.where(below, 0, 1))
    fhi0 = jnp.where(above, 0.0, jnp.where(below, f_zero, f_pos))
    log_target = float(np.log(topk + 0.5))
    glog = lambda c: jnp.log(jnp.maximum(c, 0.5)) - log_target
    done0 = jnp.where(jnp.logical_or(jnp.logical_or(few, at_zero), flo0 == kf), 1.0, 0.0)
    n_interp = 24

    def peel_min(base, flip):
        tiles = range(n_row_tiles)
        bb = [jnp.broadcast_to(base[t * rt:(t + 1) * rt], (rt, LANES)) for t in tiles]
        fb = [jnp.broadcast_to(flip[t * rt:(t + 1) * rt], (rt, LANES)) for t in tiles]

        def tile_min(t):
            def body(j, acc):
                c0 = pl.multiple_of(j * kb, kb)
                blk = keys_ref[t * rt:(t + 1) * rt, pl.ds(c0, kb)]
                for i in range(n_lane_tiles):
                    x = blk[:, i * LANES:(i + 1) * LANES] ^ fb[t]
                    acc = jnp.minimum(acc, jnp.where(x >= bb[t], x, INT_MAX))
                return acc

            acc = lax.fori_loop(0, nkb, body, jnp.full((rt, LANES), INT_MAX, I32))
            return jnp.min(acc, axis=1, keepdims=True)

        col = jnp.concatenate([tile_min(t) for t in tiles], axis=0)
        return jnp.transpose(jnp.broadcast_to(col, (tq, 8)))

    def probe(st, forced=None, use_forced=None):
        it, _, _, lo, hi, flo, fhi, glo, ghi, side, done = st
        v_lo, v_hi = _val_of(lo), _val_of(hi)
        v_model = (v_lo * ghi - v_hi * glo) / (ghi - glo)
        inside = flo - fhi
        v_even = v_lo + (v_hi - v_lo) * ((flo - kf + 0.5) / (inside + 1.0))
        c_interp = _key_of(jnp.where(inside <= 64.0, v_even, v_model))
        c_mid = jnp.right_shift(lo, 1) + jnp.right_shift(hi, 1) + (lo & hi & 1)
        cand = jnp.where(it >= n_interp, c_mid, c_interp)
        if forced is not None:
            cand = jnp.where(use_forced, forced, cand)
        cand = jnp.minimum(jnp.maximum(cand, lo + 1), hi - 1)
        c = count_ge([cand])[0]
        active = done < 0.5
        to_lo = jnp.logical_and(active, c >= kf)
        to_hi = jnp.logical_and(active, c < kf)
        gc = glog(c)
        ghi_n = jnp.where(to_hi, gc, jnp.where(jnp.logical_and(to_lo, side > 0.0), ghi * 0.5, ghi))
        glo_n = jnp.where(to_lo, gc, jnp.where(jnp.logical_and(to_hi, side < 0.0), glo * 0.5, glo))
        side = jnp.where(to_lo, 1.0, jnp.where(to_hi, -1.0, side))
        lo = jnp.where(to_lo, cand, lo)
        flo = jnp.where(to_lo, c, flo)
        hi = jnp.where(to_hi, cand, hi)
        fhi = jnp.where(to_hi, c, fhi)
        fin = jnp.logical_or(flo == kf, hi - 1 == lo)
        done = jnp.where(fin, 1.0, done)
        near = jnp.logical_or(flo - kf <= 1.0, kf - fhi <= 1.0)
        n_far = jnp.sum(jnp.where(near, 0.0, 1.0 - done))
        return it + 1, n_far, jnp.sum(1.0 - done), lo, hi, flo, fhi, glo_n, ghi_n, side, done

    n_pass_cap = n_interp + 34
    near0 = jnp.logical_or(flo0 - kf <= 1.0, kf - fhi0 <= 1.0)
    st = (jnp.int32(0), jnp.sum(jnp.where(near0, 0.0, 1.0 - done0)), jnp.sum(1.0 - done0),
          lo0, hi0, flo0, fhi0, glog(flo0), glog(fhi0), jnp.zeros((8, tq), F32), done0)
    st = lax.while_loop(lambda st: jnp.logical_and(st[0] < n_pass_cap, st[1] > 0.0), probe, st)
    lo, hi, flo, fhi, done = st[3], st[4], st[5], st[6], st[10]
    active = done < 0.5
    drop_one = jnp.logical_and(active, flo - kf <= 1.0)
    add_one = jnp.logical_and(jnp.logical_and(active, jnp.logical_not(drop_one)), kf - fhi <= 1.0)
    flip = jnp.where(add_one, -1, 0)
    edge = peel_min(as_col(jnp.where(add_one, -hi, lo)), as_col(flip))
    forced = jnp.where(add_one, edge ^ flip, edge + 1)
    st = probe(st, forced, jnp.logical_or(drop_one, add_one))
    st = lax.while_loop(lambda st: jnp.logical_and(st[0] < n_pass_cap + 2, st[2] > 0.0), probe, st)
    thr = as_col(st[3])
    tie_rows = jnp.logical_and(st[5] > kf, jnp.logical_not(few))
    need = as_col(jnp.where(tie_rows, kf - st[6], float(2 * seq)))

    run_ref[...] = jnp.zeros(run_ref.shape, F32)
    rr = lax.broadcasted_iota(I32, (LANES, 2 * LANES), 0)
    cc = lax.broadcasted_iota(I32, (LANES, 2 * LANES), 1)
    tri_ref[...] = jnp.where(jnp.logical_or(rr <= cc, cc >= LANES), 1.0, 0.0).astype(BF16)

    rep = ATT_HEADS // ATT_KV_HEADS
    rows = rep * tq
    rc = 32
    for g in range(ATT_KV_HEADS):
        for r in range(rep):
            hd = g * rep + r
            qg_ref[g, r * tq:(r + 1) * tq, :] = q_ref[:, hd * 128:(hd + 1) * 128]
    acc_ref[...] = jnp.zeros(acc_ref.shape, F32)
    l_ref[...] = jnp.zeros(l_ref.shape, F32)
    m_ref[...] = jnp.full(m_ref.shape, NEG_BIG, F32)

    def select_block(j, buf, count_ties):
        c0 = pl.multiple_of(j * kb, kb)
        run = run_ref[...]
        left = jnp.broadcast_to(need - run, (tq, LANES))
        for i in range(n_lane_tiles):
            keys = keys_ref[:, pl.ds(c0 + i * LANES, LANES)]
            tied = keys == thr
            cnt = _dot(jnp.where(tied, 1.0, 0.0).astype(BF16), tri_ref[...])
            ok = jnp.logical_and(tied, cnt[:, :LANES] <= left)
            bias_ref[buf, :, i * LANES:(i + 1) * LANES] = jnp.where(keys > thr, 0.0,
                                                                      jnp.where(ok, 0.0, NEG_BIG))
            left = left - cnt[:, LANES:]
        run_ref[...] = jnp.where(count_ties, need - left[:, :1], run)

    select_block(0, 0, True)

    def attend_block(j, carry):
        c0 = pl.multiple_of(j * kb, kb)
        buf = j & 1
        for g in range(ATT_KV_HEADS):
            for a0 in range(0, kb, 256):
                s_ref[g, :, a0:a0 + 256] = _dot(qg_ref[g], kt_ref[g * 128:(g + 1) * 128, pl.ds(c0 + a0, 256)])
        for g in range(ATT_KV_HEADS):
            for r0 in range(0, rows, rc):
                rs = slice(r0, r0 + rc)
                b0 = r0 % tq
                s = s_ref[g, rs, :] + bias_ref[buf, pl.ds(b0, rc), :]
                m_prev = m_ref[g, rs, :]
                m_next = jnp.maximum(m_prev, jnp.max(s, axis=1, keepdims=True))
                p = jnp.exp2(s - jnp.concatenate([m_next] * n_lane_tiles, axis=1))
                alpha = jnp.exp2(m_prev - m_next)
                p_lanes = p[:, :LANES]
                for i in range(1, n_lane_tiles):
                    p_lanes = p_lanes + p[:, i * LANES:(i + 1) * LANES]
                l_ref[g, rs, :] = alpha * l_ref[g, rs, :] + p_lanes
                m_ref[g, rs, :] = m_next
                acc_ref[g, rs, :] = acc_ref[g, rs, :] * alpha
                p_ref[g, rs, :] = p.astype(BF16)
            acc_ref[g] += _dot(p_ref[g], v_ref[pl.ds(c0, kb), g * 128:(g + 1) * 128])
        select_block(jnp.minimum(j + 1, nkb - 1), 1 - buf, j + 1 < nkb)
        return carry

    lax.fori_loop(0, nkb, attend_block, 0)

    for g in range(ATT_KV_HEADS):
        out = acc_ref[g] / jnp.sum(l_ref[g], axis=1, keepdims=True)
        for r in range(rep):
            hd = g * rep + r
            o_ref[:, hd * 128:(hd + 1) * 128] = out[r * tq:(r + 1) * tq]


def _dsa(q, qi, wi, kt, v, kit, topk, tq, kb):
    b, s, _ = q.shape
    rep = ATT_HEADS // ATT_KV_HEADS
    qspec = lambda c: pl.BlockSpec((None, tq, c), lambda bi, i: (bi, i, 0))
    per_batch = lambda r, c: pl.BlockSpec((None, r, c), lambda bi, i: (bi, 0, 0),
                                          pipeline_mode=pl.Buffered(1))
    return pl.pallas_call(
        functools.partial(_dsa_kernel, tq=tq, kb=kb, topk=topk, seq=s),
        grid=(b, s // tq),
        in_specs=[qspec(1024), qspec(512), qspec(128),
                  per_batch(256, s), per_batch(s, 256), per_batch(128, s)],
        out_specs=qspec(1024),
        out_shape=jax.ShapeDtypeStruct((b, s, 1024), F32),
        scratch_shapes=[pltpu.VMEM((tq, s), I32),
                        pltpu.VMEM((IDX_HEADS * tq, 128), BF16),
                        pltpu.VMEM((ATT_KV_HEADS, rep * tq, 128), BF16),
                        pltpu.VMEM((2, tq, kb), F32),
                        pltpu.VMEM((ATT_KV_HEADS, rep * tq, kb), F32),
                        pltpu.VMEM((ATT_KV_HEADS, rep * tq, kb), BF16),
                        pltpu.VMEM((ATT_KV_HEADS, rep * tq, 128), F32),
                        pltpu.VMEM((ATT_KV_HEADS, rep * tq, 128), F32),
                        pltpu.VMEM((ATT_KV_HEADS, rep * tq, 128), F32),
                        pltpu.VMEM((tq, 1), F32),
                        pltpu.VMEM((LANES, 2 * LANES), BF16)],
        compiler_params=_cparams(("parallel", "arbitrary")),
        name="dsa_attention",
    )(q, qi, wi, kt, v, kit)


def _softplus(x):
    return jnp.maximum(x, 0.0) + jnp.log1p(jnp.exp(-jnp.abs(x)))


def _gdn_kernel(qkv_ref, ab_ref, abt_ref, cw_ref, ac_ref, dc_ref, ar_ref, dr_ref, gn_ref,
                y_ref, xbuf_ref, act_ref, state_ref, *, tb):
    ck = tb
    n_chunks = tb // ck

    @pl.when(pl.program_id(1) == 0)
    def _():
        xbuf_ref[0:8, :] = jnp.zeros((8, xbuf_ref.shape[1]), F32)
        state_ref[...] = jnp.zeros(state_ref.shape, F32)

    xbuf_ref[8:8 + tb, :] = qkv_ref[...]
    for c in range(xbuf_ref.shape[1] // LANES):
        sl = slice(c * LANES, (c + 1) * LANES)
        y = xbuf_ref[5:5 + tb, sl] * cw_ref[0:1, sl]
        for j in range(1, CONV_K):
            y = y + xbuf_ref[5 + j:5 + j + tb, sl] * cw_ref[j:j + 1, sl]
        act_ref[:, sl] = y * (1.0 / (1.0 + jnp.exp(-y)))
    xbuf_ref[0:8, :] = xbuf_ref[tb:tb + 8, :]

    ab = ab_ref[...]
    g_col = -jnp.exp(ac_ref[...]) * _softplus(ab + dc_ref[...])
    beta_col = 1.0 / (1.0 + jnp.exp(-ab))
    g_row = -jnp.exp(ar_ref[...]) * _softplus(abt_ref[0:8, :] + dr_ref[...])

    r = lax.broadcasted_iota(I32, (tb, tb), 0)
    c = lax.broadcasted_iota(I32, (tb, tb), 1)
    chunk_shift = ck.bit_length() - 1
    same = jnp.right_shift(r, chunk_shift) == jnp.right_shift(c, chunk_shift)
    lower = jnp.logical_and(same, r >= c)
    strict = jnp.logical_and(same, r > c)
    lower_m = jnp.where(lower, 1.0, 0.0).astype(BF16)
    upper_m = jnp.where(jnp.logical_and(same, r <= c), 1.0, 0.0).astype(BF16)
    same_m = jnp.where(same, 1.0, 0.0).astype(BF16)
    eye = jnp.where(r == c, 1.0, 0.0)
    n_levels = ck.bit_length() - 1
    lvl_subs = []
    for lvl in range(n_levels):
        lvl_subs.append(jnp.logical_and(jnp.right_shift(r, lvl + 1) == jnp.right_shift(c, lvl + 1),
                                        jnp.logical_and((jnp.right_shift(r, lvl) & 1) == 1,
                                                        (jnp.right_shift(c, lvl) & 1) == 0)))
    lvl_masks = [jnp.where(sub, 1.0, 0.0).astype(BF16) for sub in lvl_subs]

    gparts = _split3(g_col)
    gc_col = _dot(lower_m, gparts[0]) + (_dot(lower_m, gparts[1]) + _dot(lower_m, gparts[2]))
    gl_col = _dot(same_m, gparts[0]) + (_dot(same_m, gparts[1]) + _dot(same_m, gparts[2]))
    rparts = _split3(g_row)
    gc_row = _dot(rparts[0], upper_m) + (_dot(rparts[1], upper_m) + _dot(rparts[2], upper_m))

    gn = gn_ref[...]
    hg = 4
    for h0 in range(0, GDN_HEADS, hg):
        heads = range(h0, h0 + hg)
        q_l, k_l, kbeta_l, vbeta_l, decay_l, gc_l, gl_l, m16_l, d_l = ([] for _ in range(9))
        for h in heads:
            q = act_ref[:, h * 128:(h + 1) * 128]
            k = act_ref[:, 1024 + h * 128:1024 + (h + 1) * 128]
            v = act_ref[:, 2048 + h * 128:2048 + (h + 1) * 128]
            q = q * lax.rsqrt(jnp.sum(q * q, axis=-1, keepdims=True) + EPS) * (GDN_DK ** -0.5)
            k = k * lax.rsqrt(jnp.sum(k * k, axis=-1, keepdims=True) + EPS)
            gc = gc_col[:, h:h + 1]
            beta = beta_col[:, 8 + h:9 + h]
            diff = gc - gc_row[h:h + 1, :]
            decay = jnp.where(lower, jnp.exp(jnp.where(lower, diff, 0.0)), 0.0)
            kbeta = k * beta
            nm = jnp.where(strict, -(_dot_nt(kbeta.astype(BF16), k.astype(BF16)) * decay), 0.0)
            q_l.append(q); k_l.append(k); kbeta_l.append(kbeta); vbeta_l.append(v * beta)
            decay_l.append(decay); gc_l.append(gc); gl_l.append(gl_col[:, h:h + 1])
            m16_l.append(nm.astype(BF16)); d_l.append(eye + jnp.where(lvl_subs[0], nm, 0.0))
        for lvl in range(1, n_levels):
            for i in range(hg):
                dh = d_l[i].astype(BF16)
                lh = m16_l[i] * lvl_masks[lvl]
                th = _dot(dh, lh).astype(BF16)
                d_l[i] = d_l[i] + _dot(th, dh)
        u_l, w_l, attn_l, qd_l, kd_l = [], [], [], [], []
        for i in range(hg):
            egc = jnp.exp(gc_l[i])
            rhs = jnp.concatenate([vbeta_l[i], kbeta_l[i] * egc], axis=1)
            sol = _dot(d_l[i].astype(BF16), rhs.astype(BF16))
            u_l.append(sol[:, :128])
            w_l.append(sol[:, 128:].astype(BF16))
            k16 = k_l[i].astype(BF16)
            attn_l.append(jnp.where(lower, _dot_nt(q_l[i].astype(BF16), k16) * decay_l[i], 0.0).astype(BF16))
            qd_l.append((q_l[i] * egc).astype(BF16))
            kd_l.append(k_l[i] * jnp.exp(gl_l[i] - gc_l[i]))
        st_l = [state_ref[h] for h in heads]
        outs = [[] for _ in range(hg)]
        for ci in range(n_chunks):
            rc = slice(ci * ck, (ci + 1) * ck)
            for i in range(hg):
                st16 = st_l[i].astype(BF16)
                v_new = u_l[i][rc] - _dot(w_l[i][rc], st16)
                v_new16 = v_new.astype(BF16)
                outs[i].append(_dot(qd_l[i][rc], st16) + _dot(attn_l[i][rc, rc], v_new16))
                kd_t = jnp.transpose(kd_l[i][rc]).astype(BF16)
                st_l[i] = (st_l[i] * jnp.exp(gl_l[i][ci * ck:ci * ck + 1])
                           + _dot(kd_t, v_new16))
        for i, h in enumerate(heads):
            state_ref[h] = st_l[i]
            y_ref[:, h * 128:(h + 1) * 128] = _rms(jnp.concatenate(outs[i], axis=0), gn)


def _gdn(qkv, ab, abt, conv_w, ac, dc, ar, dr, gn, tb):
    b, s, _ = qkv.shape
    blk = lambda c: pl.BlockSpec((None, tb, c), lambda bi, i: (bi, i, 0))
    return pl.pallas_call(
        functools.partial(_gdn_kernel, tb=tb),
        grid=(b, s // tb),
        in_specs=[blk(3072), blk(128), pl.BlockSpec((None, 16, tb), lambda bi, i: (bi, 0, i)),
                  _const_spec(conv_w.shape), _const_spec(ac.shape), _const_spec(dc.shape),
                  _const_spec(ar.shape), _const_spec(dr.shape), _const_spec(gn.shape)],
        out_specs=blk(1024),
        out_shape=jax.ShapeDtypeStruct((b, s, 1024), F32),
        scratch_shapes=[pltpu.VMEM((tb + 8, 3072), F32),
                        pltpu.VMEM((tb, 3072), F32),
                        pltpu.VMEM((GDN_HEADS, GDN_DK, GDN_DV), F32)],
        compiler_params=_cparams(("parallel", "arbitrary")),
        name="gated_delta",
    )(qkv, ab, abt, conv_w, ac, dc, ar, dr, gn)


def _sigmoid(x):
    return 1.0 / (1.0 + jnp.exp(-x))


def _merge_kernel(x_ref, ya_ref, yb_ref, gpre_ref, wg_ref, wo_ref, gpost_ref, o_ref):
    x = x_ref[...]
    h = _rms(x, gpre_ref[...]).astype(BF16)
    gates = _dot(h, wg_ref[...])
    y = _sigmoid(gates[:, :1024]) * ya_ref[...] + _sigmoid(gates[:, 1024:]) * yb_ref[...]
    z = _dot(y.astype(BF16), wo_ref[...])
    o_ref[...] = x + _rms(z, gpost_ref[...])


def _merge(x2, ya, yb, gpre, wg, wo, gpost, tm):
    n, d = x2.shape
    row = pl.BlockSpec((tm, d), lambda i: (i, 0))
    return pl.pallas_call(
        _merge_kernel,
        grid=(n // tm,),
        in_specs=[row, row, row, _const_spec((1, d)), _const_spec(wg.shape),
                  _const_spec(wo.shape), _const_spec((1, d))],
        out_specs=row,
        out_shape=jax.ShapeDtypeStruct((n, d), F32),
        compiler_params=_cparams(("parallel",)),
        name="merge_out",
    )(x2, ya, yb, gpre, wg, wo, gpost)


def _mem_kv_kernel(m_ref, g_ref, w_ref, kv_ref):
    h = _rms(m_ref[...], g_ref[...]).astype(BF16)
    kv_ref[...] = _dot(h, w_ref[...]).astype(BF16)


def _mem_kv(mem2, g, w):
    n, d = mem2.shape
    return pl.pallas_call(
        _mem_kv_kernel,
        out_shape=jax.ShapeDtypeStruct((n, w.shape[1]), BF16),
        compiler_params=pltpu.CompilerParams(vmem_limit_bytes=VMEM_LIMIT),
        name="mem_kv",
    )(mem2, g, w)


def _xattn_kernel(x_ref, kt_ref, v_ref, gpre_ref, wq_ref, wo_ref, gpost_ref, o_ref):
    x = x_ref[...]
    d = x.shape[1]
    hd = d // XATT_HEADS
    h = _rms(x, gpre_ref[...]).astype(BF16)
    q = (_dot(h, wq_ref[...]) * (hd ** -0.5)).astype(BF16)
    heads = []
    for i in range(XATT_HEADS):
        sl = slice(i * hd, (i + 1) * hd)
        s = _dot(q[:, sl], kt_ref[sl, :])
        p = jnp.exp(s - jnp.max(s, axis=1, keepdims=True))
        o = _dot(p.astype(BF16), v_ref[:, sl]) / jnp.sum(p, axis=1, keepdims=True)
        heads.append(o.astype(BF16))
    z = _dot(jnp.concatenate(heads, axis=1), wo_ref[...])
    o_ref[...] = x + _rms(z, gpost_ref[...])


def _xattn(x, kt, v, gpre, wq, wo, gpost, tm):
    b, s, d = x.shape
    n_mem = v.shape[1]
    row = pl.BlockSpec((None, tm, d), lambda bi, i: (bi, i, 0))
    return pl.pallas_call(
        _xattn_kernel,
        grid=(b, s // tm),
        in_specs=[row,
                  pl.BlockSpec((None, d, n_mem), lambda bi, i: (bi, 0, 0)),
                  pl.BlockSpec((None, n_mem, d), lambda bi, i: (bi, 0, 0)),
                  _const_spec((1, d)), _const_spec(wq.shape), _const_spec(wo.shape),
                  _const_spec((1, d))],
        out_specs=row,
        out_shape=jax.ShapeDtypeStruct((b, s, d), F32),
        compiler_params=_cparams(("parallel", "parallel")),
        name="mem_xattn",
    )(x, kt, v, gpre, wq, wo, gpost)


def _ffn_kernel(x_ref, gpre_ref, wgu_ref, wd_ref, gpost_ref, o_ref, *, ff, fc):
    x = x_ref[...]
    h = _rms(x, gpre_ref[...]).astype(BF16)
    z = jnp.zeros(x.shape, F32)
    for c0 in range(0, ff, fc):
        gate = _dot(h, wgu_ref[:, c0:c0 + fc])
        up = _dot(h, wgu_ref[:, ff + c0:ff + c0 + fc])
        act = (gate * _sigmoid(gate) * up).astype(BF16)
        z = z + _dot(act, wd_ref[c0:c0 + fc, :])
    o_ref[...] = x + _rms(z, gpost_ref[...])


def _ffn(x2, gpre, wgu, wd, gpost, tm):
    n, d = x2.shape
    ff = wd.shape[0]
    fc = ff // 2 if (ff // 2) % LANES == 0 else ff
    row = pl.BlockSpec((tm, d), lambda i: (i, 0))
    return pl.pallas_call(
        functools.partial(_ffn_kernel, ff=ff, fc=fc),
        grid=(n // tm,),
        in_specs=[row, _const_spec((1, d)), _const_spec(wgu.shape), _const_spec(wd.shape),
                  _const_spec((1, d))],
        out_specs=row,
        out_shape=jax.ShapeDtypeStruct((n, d), F32),
        compiler_params=_cparams(("parallel",)),
        name="swiglu",
    )(x2, gpre, wgu, wd, gpost)


def _rope_tables(positions):
    pos = positions.astype(F32).reshape(-1)[:, None]
    lane = jnp.arange(LANES)

    def tables(dim):
        half = dim // 2
        in_head = lane % dim
        inv_freq = ROPE_THETA ** (-(in_head % half).astype(F32) * 2.0 / dim)
        ang = pos * inv_freq
        return jnp.cos(ang), jnp.where(in_head < half, -1.0, 1.0) * jnp.sin(ang)

    return tables(ATT_HEAD_DIM) + tables(IDX_DIM)


def _pick_tile(n, pref):
    t = min(pref, n)
    while n % t:
        t //= 2
    return t


def kernel(x, mem, positions, norm_mix_pre, w_in, conv_w, a_log, dt_bias, gdn_norm, w_out,
           norm_mix_post, norm_x_pre, norm_mem, w_xq, w_xkv, w_xo, norm_x_post,
           norm_ffn_pre, w_gu, w_down, norm_ffn_post):
    b, s, d = x.shape
    n = b * s
    n_mem = mem.shape[1]
    depth = w_in.shape[0]
    topk = min(TOPK_MAX, s // 4)
    tm = _pick_tile(s, 512)
    tq = _pick_tile(s, 256)
    kb = _pick_tile(s, 512)
    tb = _pick_tile(s, 256)

    ca, sa, ci, si = (t.reshape(b, s, 128) for t in _rope_tables(positions))

    row1 = lambda v: v.reshape(1, -1).astype(F32)

    x2 = x.reshape(n, d)
    for l in range(depth):
        w_att, w_gdn, w_gate = _prep_w_in(w_in, l)

        g_pre = row1(norm_mix_pre[l])
        x3 = x2.reshape(b, s, d)
        q, kt, v, qi, kit, wi = _proj_att(x3, g_pre, w_att, ca, sa, ci, si, tm)
        ya = _dsa(q, qi, wi, kt, v, kit, topk, tq, kb)

        qkv, ab, abt = _proj_gdn(x3, g_pre, w_gdn, tm)
        pad_lanes = lambda vec: jnp.pad(vec.astype(F32), (0, 128 - GDN_HEADS)).reshape(1, 128)
        bcast_rows = lambda vec: jnp.broadcast_to(vec.astype(F32)[:, None], (GDN_HEADS, tb))
        yb = _gdn(qkv, ab, abt, conv_w[l].astype(F32),
                  pad_lanes(a_log[l]), pad_lanes(dt_bias[l]), bcast_rows(a_log[l]), bcast_rows(dt_bias[l]),
                  row1(gdn_norm[l]), tb)

        x2 = _merge(x2, ya.reshape(n, 1024), yb.reshape(n, 1024), g_pre, w_gate,
                    w_out[l].astype(BF16), row1(norm_mix_post[l]), tm)

        kv = _mem_kv(mem.reshape(b * n_mem, d), row1(norm_mem[l]), w_xkv[l].astype(BF16))
        kv = kv.reshape(b, n_mem, 2 * d)
        kt_mem = jnp.swapaxes(kv[:, :, :d], 1, 2)
        x2 = _xattn(x2.reshape(b, s, d), kt_mem, kv[:, :, d:], row1(norm_x_pre[l]),
                    w_xq[l].astype(BF16), w_xo[l].astype(BF16), row1(norm_x_post[l]), tm).reshape(n, d)

        x2 = _ffn(x2, row1(norm_ffn_pre[l]), w_gu[l].astype(BF16), w_down[l].astype(BF16),
                  row1(norm_ffn_post[l]), tm)
    return x2.reshape(b, s, d)
```

```python
import functools

import jax
import jax.numpy as jnp
import numpy as np
from jax import lax
from jax.experimental import pallas as pl
from jax.experimental.pallas import tpu as pltpu

EPS = 1e-6
ROPE_THETA = 10000.0
ATT_HEADS = 8
ATT_KV_HEADS = 2
ATT_HEAD_DIM = 128
IDX_HEADS = 4
IDX_DIM = 64
TOPK_MAX = 256
GDN_HEADS = 8
GDN_DK = 128
GDN_DV = 128
CONV_K = 4
XATT_HEADS = 4

LANES = 128
VMEM_LIMIT = 56 * 1024 * 1024

F32 = jnp.float32
BF16 = jnp.bfloat16
I32 = jnp.int32

NEG_BIG = -1e30
LOG2E = 1.4426950408889634
INT_MAX = 2147483647
NEG_INF_KEY = -2139095041


def _cparams(sem):
    return pltpu.CompilerParams(dimension_semantics=sem, vmem_limit_bytes=VMEM_LIMIT)


def _const_spec(shape):
    nd = len(shape)
    return pl.BlockSpec(shape, lambda *_: (0,) * nd, pipeline_mode=pl.Buffered(1))


def _rms(x, g):
    return x * lax.rsqrt(jnp.mean(x * x, axis=-1, keepdims=True) + EPS) * g


def _dot(a, b):
    return jnp.dot(a, b, preferred_element_type=F32)


def _dot_nt(a, b):
    return lax.dot_general(a, b, (((1,), (1,)), ((), ())), preferred_element_type=F32)


def _split3(a):
    a1 = a.astype(BF16)
    r1 = a - a1.astype(F32)
    a2 = r1.astype(BF16)
    a3 = (r1 - a2.astype(F32)).astype(BF16)
    return a1, a2, a3


A_Q, A_KV, I_Q, I_K, I_W, B_QKV = 1024, 256, 256, 64, 4, 3072
O_IQ = A_Q + 2 * A_KV
O_IK = O_IQ + I_Q
O_IW = O_IK + I_K
O_B = O_IW + I_W
O_BA = O_B + B_QKV
O_GA = O_BA + 2 * GDN_HEADS
IN_COLS = O_GA + 2 * 1024


def _prep_w_in_kernel(w_ref, att_ref, gdn_ref, gate_ref):
    w = w_ref[...]
    zeros = lambda c: jnp.zeros((w.shape[0], c), F32)
    att = [w[:, :O_IQ]]
    for h in range(IDX_HEADS):
        att += [w[:, O_IQ + h * IDX_DIM:O_IQ + (h + 1) * IDX_DIM], zeros(128 - IDX_DIM)]
    att += [w[:, O_IK:O_IW], zeros(128 - I_K), w[:, O_IW:O_B], zeros(128 - I_W)]
    att_ref[...] = jnp.concatenate(att, axis=1).astype(BF16)
    gdn_ref[...] = jnp.concatenate([w[:, O_B:O_GA], zeros(128 - 2 * GDN_HEADS)], axis=1).astype(BF16)
    gate_ref[...] = w[:, O_GA:].astype(BF16)


def _prep_w_in(w_in, layer):
    _, d, cols = w_in.shape
    assert cols == IN_COLS
    tr = 128
    widths = (O_IQ + IDX_HEADS * 128 + 256, B_QKV + 128, 2048)
    return pl.pallas_call(
        _prep_w_in_kernel,
        grid=(d // tr,),
        in_specs=[pl.BlockSpec((None, tr, cols), lambda i: (layer, i, 0))],
        out_specs=[pl.BlockSpec((tr, c), lambda i: (i, 0)) for c in widths],
        out_shape=[jax.ShapeDtypeStruct((d, c), BF16) for c in widths],
        compiler_params=_cparams(("parallel",)),
        name="prep_w_in",
    )(w_in)


def _rope128(x, cos, sin_signed):
    return x * cos + pltpu.roll(x, 64, 1) * sin_signed


def _rope64(x, cos, sin_signed, first_half):
    partner = jnp.where(first_half, pltpu.roll(x, 96, 1), pltpu.roll(x, 32, 1))
    return x * cos + partner * sin_signed


def _proj_att_kernel(x_ref, g_ref, w_ref, ca_ref, sa_ref, ci_ref, si_ref,
                     q_ref, kt_ref, v_ref, qi_ref, kit_ref, wi_ref):
    h = _rms(x_ref[...], g_ref[...]).astype(BF16)
    p = _dot(h, w_ref[...])
    ca, sa = ca_ref[...], sa_ref[...]
    ci, si = ci_ref[...], si_ref[...]
    q_scale = ATT_HEAD_DIM ** -0.5 * LOG2E
    for hd in range(ATT_HEADS):
        sl = slice(hd * 128, (hd + 1) * 128)
        q_ref[:, sl] = (_rope128(p[:, sl], ca, sa) * q_scale).astype(BF16)
    for g in range(ATT_KV_HEADS):
        k_g = _rope128(p[:, A_Q + g * 128:A_Q + (g + 1) * 128], ca, sa)
        kt_ref[g * 128:(g + 1) * 128, :] = jnp.transpose(k_g).astype(BF16)
    v_ref[...] = p[:, A_Q + A_KV:O_IQ].astype(BF16)
    lane = lax.broadcasted_iota(I32, ci.shape, 1)
    first_half = (lane & 63) < 32
    idx_scale = IDX_DIM ** -0.5
    for hd in range(IDX_HEADS):
        off = O_IQ + hd * 128
        qi_ref[:, hd * 128:(hd + 1) * 128] = (
            _rope64(p[:, off:off + 128], ci, si, first_half) * idx_scale).astype(BF16)
    o_ki = O_IQ + IDX_HEADS * 128
    kit_ref[...] = jnp.transpose(_rope64(p[:, o_ki:o_ki + 128], ci, si, first_half)).astype(BF16)
    wi_ref[...] = p[:, o_ki + 128:o_ki + 256] * (IDX_HEADS ** -0.5)


def _proj_att(x, g, w, ca, sa, ci, si, tm):
    b, s, d = x.shape
    row = lambda c: pl.BlockSpec((None, tm, c), lambda bi, i: (bi, i, 0))
    col = lambda r: pl.BlockSpec((None, r, tm), lambda bi, i: (bi, 0, i))
    sds = jax.ShapeDtypeStruct
    return pl.pallas_call(
        _proj_att_kernel,
        grid=(b, s // tm),
        in_specs=[row(d), _const_spec((1, d)), _const_spec(w.shape),
                  row(128), row(128), row(128), row(128)],
        out_specs=[row(1024), col(256), row(256), row(512), col(128), row(128)],
        out_shape=[sds((b, s, 1024), BF16), sds((b, 256, s), BF16), sds((b, s, 256), BF16),
                   sds((b, s, 512), BF16), sds((b, 128, s), BF16), sds((b, s, 128), F32)],
        compiler_params=_cparams(("parallel", "parallel")),
        name="proj_att",
    )(x, g, w, ca, sa, ci, si)


def _proj_gdn_kernel(x_ref, g_ref, w_ref, cw_ref, act_ref, ab_ref, abt_ref, xbuf_ref):
    tm = x_ref.shape[0]

    @pl.when(pl.program_id(1) == 0)
    def _():
        xbuf_ref[0:8, :] = jnp.zeros((8, xbuf_ref.shape[1]), F32)

    h = _rms(x_ref[...], g_ref[...]).astype(BF16)
    p = _dot(h, w_ref[...])
    ab = p[:, B_QKV:B_QKV + 128]
    ab_ref[...] = ab
    abt_ref[...] = jnp.transpose(ab)[:16, :]
    xbuf_ref[8:8 + tm, :] = p[:, :B_QKV]
    for c in range(B_QKV // LANES):
        sl = slice(c * LANES, (c + 1) * LANES)
        y = xbuf_ref[5:5 + tm, sl] * cw_ref[0:1, sl]
        for j in range(1, CONV_K):
            y = y + xbuf_ref[5 + j:5 + j + tm, sl] * cw_ref[j:j + 1, sl]
        act_ref[:, sl] = y * (1.0 / (1.0 + jnp.exp(-y)))
    xbuf_ref[0:8, :] = xbuf_ref[tm:tm + 8, :]


def _proj_gdn(x, g, w, conv_w, tm):
    b, s, d = x.shape
    row = lambda c: pl.BlockSpec((None, tm, c), lambda bi, i: (bi, i, 0))
    sds = jax.ShapeDtypeStruct
    return pl.pallas_call(
        _proj_gdn_kernel,
        grid=(b, s // tm),
        in_specs=[row(d), _const_spec((1, d)), _const_spec(w.shape), _const_spec(conv_w.shape)],
        out_specs=[row(B_QKV), row(128), pl.BlockSpec((None, 16, tm), lambda bi, i: (bi, 0, i))],
        out_shape=[sds((b, s, B_QKV), F32), sds((b, s, 128), F32), sds((b, 16, s), F32)],
        scratch_shapes=[pltpu.VMEM((tm + 8, B_QKV), F32)],
        compiler_params=_cparams(("parallel", "arbitrary")),
        name="proj_gdn",
    )(x, g, w, conv_w)


def _key_of(v):
    b = pltpu.bitcast(v, I32)
    return jnp.where(b < 0, b ^ 0x7FFFFFFF, b)


def _val_of(k):
    return pltpu.bitcast(jnp.where(k < 0, k ^ 0x7FFFFFFF, k), F32)


def _dsa_kernel(q_ref, qi_ref, wi_ref, kt_ref, v_ref, kit_ref, o_ref,
                keys_ref, qis_ref, qg_ref, bias_ref, s_ref, p_ref, acc_ref, m_ref, l_ref, run_ref, tri_ref,
                *, tq, kb, topk, seq):
    t0 = pl.program_id(1) * tq
    nkb = (t0 + tq + kb - 1) // kb
    n_lane_tiles = kb // LANES
    row_ids = t0 + lax.broadcasted_iota(I32, (tq, kb), 0)
    col_iota = lax.broadcasted_iota(I32, (tq, kb), 1)
    kf = float(topk)

    for h in range(IDX_HEADS):
        qis_ref[h * tq:(h + 1) * tq, :] = qi_ref[:, h * 128:(h + 1) * 128]
    wi = wi_ref[...]
    w_heads = [wi[:, h:h + 1] for h in range(IDX_HEADS)]

    def score_block(j, carry, masked):
        mx, mn = carry
        c0 = pl.multiple_of(j * kb, kb)
        halves = []
        for a0 in range(0, kb, 256):
            lg = _dot(qis_ref[...], kit_ref[:, pl.ds(c0 + a0, 256)])
            part = jnp.zeros((tq, 256), F32)
            for h in range(IDX_HEADS):
                part = part + w_heads[h] * jnp.maximum(lg[h * tq:(h + 1) * tq], 0.0)
            halves.append(part)
        sc = jnp.concatenate(halves, axis=1)
        sc = jnp.where(sc == 0.0, 0.0, sc)
        for i in range(n_lane_tiles):
            mx = jnp.maximum(mx, sc[:, i * LANES:(i + 1) * LANES])
            mn = jnp.minimum(mn, sc[:, i * LANES:(i + 1) * LANES])
        if masked:
            sc = jnp.where(c0 + col_iota <= row_ids, sc, -jnp.inf)
        keys_ref[:, pl.ds(c0, kb)] = _key_of(sc)
        return mx, mn

    n_full = t0 // kb
    ext = (jnp.full((tq, LANES), -jnp.inf, F32), jnp.full((tq, LANES), jnp.inf, F32))

    done_blocks = 0
    for group in (4, 2, 1):
        def score_group(jj, carry, group=group, first=done_blocks):
            for u in range(group):
                carry = score_block(first + group * jj + u, carry, False)
            return carry

        n_groups = (n_full - done_blocks) // group
        ext = lax.fori_loop(0, n_groups, score_group, ext)
        done_blocks = done_blocks + group * n_groups
    mx, mn = lax.fori_loop(n_full, nkb, functools.partial(score_block, masked=True), ext)
    row_max = jnp.max(mx, axis=1, keepdims=True)
    row_min = jnp.min(mn, axis=1, keepdims=True)

    n_row_tiles = tq // LANES if tq % LANES == 0 else 1
    rt = tq // n_row_tiles

    def as_col(row):
        return jnp.transpose(row)[:, :1]

    def count_ge(cands):
        nc = len(cands)
        tiles = range(n_row_tiles)
        cb = [[c if isinstance(c, int) else
               jnp.transpose(jnp.broadcast_to(c[:1, t * rt:(t + 1) * rt], (LANES, rt)))
               for t in tiles] for c in cands]

        def tile_counts(t):
            def body(j, accs):
                c0 = pl.multiple_of(j * kb, kb)
                accs = list(accs)
                blk = keys_ref[t * rt:(t + 1) * rt, pl.ds(c0, kb)]
                for i in range(n_lane_tiles):
                    for k in range(nc):
                        accs[k] = accs[k] + jnp.where(blk[:, i * LANES:(i + 1) * LANES] >= cb[k][t], 1.0, 0.0)
                return tuple(accs)

            z = jnp.zeros((rt, LANES), F32)
            return lax.fori_loop(0, nkb, body, (z,) * nc)

        accs = [tile_counts(t) for t in tiles]
        row_sums = lambda a: jnp.broadcast_to(jnp.sum(jnp.transpose(a), axis=0, keepdims=True), (8, rt))
        return [jnp.concatenate([row_sums(accs[t][k]) for t in tiles], axis=1) for k in range(nc)]

    as_row = lambda col: jnp.transpose(jnp.broadcast_to(col, (tq, 8)))
    n_visible = (t0 + lax.broadcasted_iota(I32, (8, tq), 1) + 1).astype(F32)
    few = n_visible < kf
    f_pos, f_zero = count_ge([1, 0])
    above = f_pos >= kf
    below = f_zero < kf
    at_zero = jnp.logical_and(jnp.logical_not(above), jnp.logical_not(below))
    key_min = _key_of(as_row(row_min))
    key_max = _key_of(as_row(row_max))
    lo0 = jnp.where(few, NEG_INF_KEY + 1, jnp.where(above, 1, jnp.where(below, key_min, 0)))
    flo0 = jnp.where(few, n_visible, jnp.where(above, f_pos, jnp.where(below, n_visible, f_zero)))
    hi0 = jnp.where(above, jnp.minimum(key_max, 2147483646) + 1, jnp.where(below, 0, 1))
    fhi0 = jnp.where(above, 0.0, jnp.where(below, f_zero, f_pos))
    log_target = float(np.log(topk + 0.5))
    glog = lambda c: jnp.log(jnp.maximum(c, 0.5)) - log_target
    done0 = jnp.where(jnp.logical_or(jnp.logical_or(few, at_zero), flo0 == kf), 1.0, 0.0)
    n_interp = 24

    def peel_min(base, flip):
        tiles = range(n_row_tiles)
        bb = [jnp.broadcast_to(base[t * rt:(t + 1) * rt], (rt, LANES)) for t in tiles]
        fb = [jnp.broadcast_to(flip[t * rt:(t + 1) * rt], (rt, LANES)) for t in tiles]

        def tile_min(t):
            def body(j, acc):
                c0 = pl.multiple_of(j * kb, kb)
                blk = keys_ref[t * rt:(t + 1) * rt, pl.ds(c0, kb)]
                for i in range(n_lane_tiles):
                    x = blk[:, i * LANES:(i + 1) * LANES] ^ fb[t]
                    acc = jnp.minimum(acc, jnp.where(x >= bb[t], x, INT_MAX))
                return acc

            acc = lax.fori_loop(0, nkb, body, jnp.full((rt, LANES), INT_MAX, I32))
            return jnp.min(acc, axis=1, keepdims=True)

        col = jnp.concatenate([tile_min(t) for t in tiles], axis=0)
        return jnp.transpose(jnp.broadcast_to(col, (tq, 8)))

    def probe(st, forced=None, use_forced=None):
        it, _, _, lo, hi, flo, fhi, glo, ghi, side, done = st
        v_lo, v_hi = _val_of(lo), _val_of(hi)
        frac = glo / (glo - ghi)
        frac = jnp.where(jnp.logical_and(it == 0, lo == 1), 0.5 * (frac + jnp.sqrt(jnp.maximum(frac, 0.0))), frac)
        v_model = v_lo + (v_hi - v_lo) * frac
        inside = flo - fhi
        v_even = v_lo + (v_hi - v_lo) * ((flo - kf + 0.5) / (inside + 1.0))
        c_interp = _key_of(jnp.where(inside <= 64.0, v_even, v_model))
        c_mid = jnp.right_shift(lo, 1) + jnp.right_shift(hi, 1) + (lo & hi & 1)
        cand = jnp.where(it >= n_interp, c_mid, c_interp)
        if forced is not None:
            cand = jnp.where(use_forced, forced, cand)
        cand = jnp.minimum(jnp.maximum(cand, lo + 1), hi - 1)
        c = count_ge([cand])[0]
        active = done < 0.5
        to_lo = jnp.logical_and(active, c >= kf)
        to_hi = jnp.logical_and(active, c < kf)
        gc = glog(c)
        ghi_n = jnp.where(to_hi, gc, jnp.where(jnp.logical_and(to_lo, side > 0.0), ghi * 0.5, ghi))
        glo_n = jnp.where(to_lo, gc, jnp.where(jnp.logical_and(to_hi, side < 0.0), glo * 0.5, glo))
        side = jnp.where(to_lo, 1.0, jnp.where(to_hi, -1.0, side))
        lo = jnp.where(to_lo, cand, lo)
        flo = jnp.where(to_lo, c, flo)
        hi = jnp.where(to_hi, cand, hi)
        fhi = jnp.where(to_hi, c, fhi)
        fin = jnp.logical_or(flo == kf, hi - 1 == lo)
        done = jnp.where(fin, 1.0, done)
        near = jnp.logical_or(flo - kf <= 1.0, kf - fhi <= 1.0)
        n_far = jnp.sum(jnp.where(near, 0.0, 1.0 - done))
        return it + 1, n_far, jnp.sum(1.0 - done), lo, hi, flo, fhi, glo_n, ghi_n, side, done

    n_pass_cap = n_interp + 34
    near0 = jnp.logical_or(flo0 - kf <= 1.0, kf - fhi0 <= 1.0)
    st = (jnp.int32(0), jnp.sum(jnp.where(near0, 0.0, 1.0 - done0)), jnp.sum(1.0 - done0),
          lo0, hi0, flo0, fhi0, glog(flo0), glog(fhi0), jnp.zeros((8, tq), F32), done0)
    st = lax.while_loop(lambda st: jnp.logical_and(st[0] < n_pass_cap, st[1] > 0.0), probe, st)
    lo, hi, flo, fhi, done = st[3], st[4], st[5], st[6], st[10]
    active = done < 0.5
    drop_one = jnp.logical_and(active, flo - kf <= 1.0)
    add_one = jnp.logical_and(jnp.logical_and(active, jnp.logical_not(drop_one)), kf - fhi <= 1.0)
    flip = jnp.where(add_one, -1, 0)
    edge = peel_min(as_col(jnp.where(add_one, -hi, lo)), as_col(flip))
    forced = jnp.where(add_one, edge ^ flip, edge + 1)
    st = probe(st, forced, jnp.logical_or(drop_one, add_one))
    st = lax.while_loop(lambda st: jnp.logical_and(st[0] < n_pass_cap + 2, st[2] > 0.0), probe, st)
    thr = as_col(st[3])
    tie_rows = jnp.logical_and(st[5] > kf, jnp.logical_not(few))
    need = as_col(jnp.where(tie_rows, kf - st[6], float(2 * seq)))

    run_ref[...] = jnp.zeros(run_ref.shape, F32)
    rr = lax.broadcasted_iota(I32, (LANES, 2 * LANES), 0)
    cc = lax.broadcasted_iota(I32, (LANES, 2 * LANES), 1)
    tri_ref[...] = jnp.where(jnp.logical_or(rr <= cc, cc >= LANES), 1.0, 0.0).astype(BF16)

    rep = ATT_HEADS // ATT_KV_HEADS
    rows = rep * tq
    rc = 32
    for g in range(ATT_KV_HEADS):
        for r in range(rep):
            hd = g * rep + r
            qg_ref[g, r * tq:(r + 1) * tq, :] = q_ref[:, hd * 128:(hd + 1) * 128]
    acc_ref[...] = jnp.zeros(acc_ref.shape, F32)
    l_ref[...] = jnp.zeros(l_ref.shape, F32)
    m_ref[...] = jnp.full(m_ref.shape, NEG_BIG, F32)

    def select_block(j, buf, count_ties):
        c0 = pl.multiple_of(j * kb, kb)
        run = run_ref[...]
        left = jnp.broadcast_to(need - run, (tq, LANES))
        for i in range(n_lane_tiles):
            keys = keys_ref[:, pl.ds(c0 + i * LANES, LANES)]
            tied = keys == thr
            cnt = _dot(jnp.where(tied, 1.0, 0.0).astype(BF16), tri_ref[...])
            ok = jnp.logical_and(tied, cnt[:, :LANES] <= left)
            bias_ref[buf, :, i * LANES:(i + 1) * LANES] = jnp.where(keys > thr, 0.0,
                                                                      jnp.where(ok, 0.0, NEG_BIG))
            left = left - cnt[:, LANES:]
        run_ref[...] = jnp.where(count_ties, need - left[:, :1], run)

    select_block(0, 0, True)

    def attend_block(j, carry):
        c0 = pl.multiple_of(j * kb, kb)
        buf = j & 1
        for g in range(ATT_KV_HEADS):
            for a0 in range(0, kb, 256):
                s_ref[g, :, a0:a0 + 256] = _dot(qg_ref[g], kt_ref[g * 128:(g + 1) * 128, pl.ds(c0 + a0, 256)])
        for g in range(ATT_KV_HEADS):
            for r0 in range(0, rows, rc):
                rs = slice(r0, r0 + rc)
                b0 = r0 % tq
                s = s_ref[g, rs, :] + bias_ref[buf, pl.ds(b0, rc), :]
                m_prev = m_ref[g, rs, :]
                m_next = jnp.maximum(m_prev, jnp.max(s, axis=1, keepdims=True))
                p = jnp.exp2(s - jnp.concatenate([m_next] * n_lane_tiles, axis=1))
                alpha = jnp.exp2(m_prev - m_next)
                p_lanes = p[:, :LANES]
                for i in range(1, n_lane_tiles):
                    p_lanes = p_lanes + p[:, i * LANES:(i + 1) * LANES]
                l_ref[g, rs, :] = alpha * l_ref[g, rs, :] + p_lanes
                m_ref[g, rs, :] = m_next
                acc_ref[g, rs, :] = acc_ref[g, rs, :] * alpha
                p_ref[g, rs, :] = p.astype(BF16)
            acc_ref[g] += _dot(p_ref[g], v_ref[pl.ds(c0, kb), g * 128:(g + 1) * 128])
        select_block(jnp.minimum(j + 1, nkb - 1), 1 - buf, j + 1 < nkb)
        return carry

    lax.fori_loop(0, nkb, attend_block, 0)

    for g in range(ATT_KV_HEADS):
        out = acc_ref[g] / jnp.sum(l_ref[g], axis=1, keepdims=True)
        for r in range(rep):
            hd = g * rep + r
            o_ref[:, hd * 128:(hd + 1) * 128] = out[r * tq:(r + 1) * tq]


def _dsa(q, qi, wi, kt, v, kit, topk, tq, kb):
    b, s, _ = q.shape
    rep = ATT_HEADS // ATT_KV_HEADS
    qspec = lambda c: pl.BlockSpec((None, tq, c), lambda bi, i: (bi, i, 0))
    per_batch = lambda r, c: pl.BlockSpec((None, r, c), lambda bi, i: (bi, 0, 0),
                                          pipeline_mode=pl.Buffered(1))
    return pl.pallas_call(
        functools.partial(_dsa_kernel, tq=tq, kb=kb, topk=topk, seq=s),
        grid=(b, s // tq),
        in_specs=[qspec(1024), qspec(512), qspec(128),
                  per_batch(256, s), per_batch(s, 256), per_batch(128, s)],
        out_specs=qspec(1024),
        out_shape=jax.ShapeDtypeStruct((b, s, 1024), F32),
        scratch_shapes=[pltpu.VMEM((tq, s), I32),
                        pltpu.VMEM((IDX_HEADS * tq, 128), BF16),
                        pltpu.VMEM((ATT_KV_HEADS, rep * tq, 128), BF16),
                        pltpu.VMEM((2, tq, kb), F32),
                        pltpu.VMEM((ATT_KV_HEADS, rep * tq, kb), F32),
                        pltpu.VMEM((ATT_KV_HEADS, rep * tq, kb), BF16),
                        pltpu.VMEM((ATT_KV_HEADS, rep * tq, 128), F32),
                        pltpu.VMEM((ATT_KV_HEADS, rep * tq, 128), F32),
                        pltpu.VMEM((ATT_KV_HEADS, rep * tq, 128), F32),
                        pltpu.VMEM((tq, 1), F32),
                        pltpu.VMEM((LANES, 2 * LANES), BF16)],
        compiler_params=_cparams(("parallel", "arbitrary")),
        name="dsa_attention",
    )(q, qi, wi, kt, v, kit)


def _softplus(x):
    return jnp.maximum(x, 0.0) + jnp.log1p(jnp.exp(-jnp.abs(x)))


def _gdn_kernel(act_ref, ab_ref, abt_ref, ac_ref, dc_ref, ar_ref, dr_ref, gn_ref,
                y_ref, state_ref, *, tb):
    ck = tb
    n_chunks = tb // ck

    @pl.when(pl.program_id(1) == 0)
    def _():
        state_ref[...] = jnp.zeros(state_ref.shape, F32)

    ab = ab_ref[...]
    g_col = -jnp.exp(ac_ref[...]) * _softplus(ab + dc_ref[...])
    beta_col = 1.0 / (1.0 + jnp.exp(-ab))
    g_row = -jnp.exp(ar_ref[...]) * _softplus(abt_ref[0:8, :] + dr_ref[...])

    r = lax.broadcasted_iota(I32, (tb, tb), 0)
    c = lax.broadcasted_iota(I32, (tb, tb), 1)
    chunk_shift = ck.bit_length() - 1
    same = jnp.right_shift(r, chunk_shift) == jnp.right_shift(c, chunk_shift)
    lower = jnp.logical_and(same, r >= c)
    strict = jnp.logical_and(same, r > c)
    lower_m = jnp.where(lower, 1.0, 0.0).astype(BF16)
    upper_m = jnp.where(jnp.logical_and(same, r <= c), 1.0, 0.0).astype(BF16)
    same_m = jnp.where(same, 1.0, 0.0).astype(BF16)
    eye = jnp.where(r == c, 1.0, 0.0)
    n_levels = ck.bit_length() - 1
    lvl_subs = []
    for lvl in range(n_levels):
        lvl_subs.append(jnp.logical_and(jnp.right_shift(r, lvl + 1) == jnp.right_shift(c, lvl + 1),
                                        jnp.logical_and((jnp.right_shift(r, lvl) & 1) == 1,
                                                        (jnp.right_shift(c, lvl) & 1) == 0)))
    lvl_masks = [jnp.where(sub, 1.0, 0.0).astype(BF16) for sub in lvl_subs]

    gparts = _split3(g_col)
    gc_col = _dot(lower_m, gparts[0]) + (_dot(lower_m, gparts[1]) + _dot(lower_m, gparts[2]))
    gl_col = _dot(same_m, gparts[0]) + (_dot(same_m, gparts[1]) + _dot(same_m, gparts[2]))
    rparts = _split3(g_row)
    gc_row = _dot(rparts[0], upper_m) + (_dot(rparts[1], upper_m) + _dot(rparts[2], upper_m))

    gn = gn_ref[...]
    hg = 4
    for h0 in range(0, GDN_HEADS, hg):
        heads = range(h0, h0 + hg)
        q_l, k_l, kbeta_l, vbeta_l, decay_l, gc_l, gl_l, m16_l, d_l = ([] for _ in range(9))
        for h in heads:
            q = act_ref[:, h * 128:(h + 1) * 128]
            k = act_ref[:, 1024 + h * 128:1024 + (h + 1) * 128]
            v = act_ref[:, 2048 + h * 128:2048 + (h + 1) * 128]
            q = q * lax.rsqrt(jnp.sum(q * q, axis=-1, keepdims=True) + EPS) * (GDN_DK ** -0.5)
            k = k * lax.rsqrt(jnp.sum(k * k, axis=-1, keepdims=True) + EPS)
            gc = gc_col[:, h:h + 1]
            beta = beta_col[:, 8 + h:9 + h]
            diff = gc - gc_row[h:h + 1, :]
            decay = jnp.where(lower, jnp.exp(jnp.where(lower, diff, 0.0)), 0.0)
            kbeta = k * beta
            nm = jnp.where(strict, -(_dot_nt(kbeta.astype(BF16), k.astype(BF16)) * decay), 0.0)
            q_l.append(q); k_l.append(k); kbeta_l.append(kbeta); vbeta_l.append(v * beta)
            decay_l.append(decay); gc_l.append(gc); gl_l.append(gl_col[:, h:h + 1])
            m16_l.append(nm.astype(BF16)); d_l.append(eye + jnp.where(lvl_subs[0], nm, 0.0))
        for lvl in range(1, n_levels):
            for i in range(hg):
                dh = d_l[i].astype(BF16)
                lh = m16_l[i] * lvl_masks[lvl]
                th = _dot(dh, lh).astype(BF16)
                d_l[i] = d_l[i] + _dot(th, dh)
        u_l, w_l, attn_l, qd_l, kd_l = [], [], [], [], []
        for i in range(hg):
            egc = jnp.exp(gc_l[i])
            rhs = jnp.concatenate([vbeta_l[i], kbeta_l[i] * egc], axis=1)
            sol = _dot(d_l[i].astype(BF16), rhs.astype(BF16))
            u_l.append(sol[:, :128])
            w_l.append(sol[:, 128:].astype(BF16))
            k16 = k_l[i].astype(BF16)
            attn_l.append(jnp.where(lower, _dot_nt(q_l[i].astype(BF16), k16) * decay_l[i], 0.0).astype(BF16))
            qd_l.append((q_l[i] * egc).astype(BF16))
            kd_l.append(k_l[i] * jnp.exp(gl_l[i] - gc_l[i]))
        st_l = [state_ref[h] for h in heads]
        outs = [[] for _ in range(hg)]
        for ci in range(n_chunks):
            rc = slice(ci * ck, (ci + 1) * ck)
            for i in range(hg):
                st16 = st_l[i].astype(BF16)
                v_new = u_l[i][rc] - _dot(w_l[i][rc], st16)
                v_new16 = v_new.astype(BF16)
                outs[i].append(_dot(qd_l[i][rc], st16) + _dot(attn_l[i][rc, rc], v_new16))
                kd_t = jnp.transpose(kd_l[i][rc]).astype(BF16)
                st_l[i] = (st_l[i] * jnp.exp(gl_l[i][ci * ck:ci * ck + 1])
                           + _dot(kd_t, v_new16))
        for i, h in enumerate(heads):
            state_ref[h] = st_l[i]
            y_ref[:, h * 128:(h + 1) * 128] = _rms(jnp.concatenate(outs[i], axis=0), gn)


def _gdn(act, ab, abt, ac, dc, ar, dr, gn, tb):
    b, s, _ = act.shape
    blk = lambda c: pl.BlockSpec((None, tb, c), lambda bi, i: (bi, i, 0))
    return pl.pallas_call(
        functools.partial(_gdn_kernel, tb=tb),
        grid=(b, s // tb),
        in_specs=[blk(3072), blk(128), pl.BlockSpec((None, 16, tb), lambda bi, i: (bi, 0, i)),
                  _const_spec(ac.shape), _const_spec(dc.shape),
                  _const_spec(ar.shape), _const_spec(dr.shape), _const_spec(gn.shape)],
        out_specs=blk(1024),
        out_shape=jax.ShapeDtypeStruct((b, s, 1024), F32),
        scratch_shapes=[pltpu.VMEM((GDN_HEADS, GDN_DK, GDN_DV), F32)],
        compiler_params=_cparams(("parallel", "arbitrary")),
        name="gated_delta",
    )(act, ab, abt, ac, dc, ar, dr, gn)


def _sigmoid(x):
    return 1.0 / (1.0 + jnp.exp(-x))


def _merge_kernel(x_ref, ya_ref, yb_ref, gpre_ref, wg_ref, wo_ref, gpost_ref, o_ref):
    x = x_ref[...]
    h = _rms(x, gpre_ref[...]).astype(BF16)
    gates = _dot(h, wg_ref[...])
    y = _sigmoid(gates[:, :1024]) * ya_ref[...] + _sigmoid(gates[:, 1024:]) * yb_ref[...]
    z = _dot(y.astype(BF16), wo_ref[...])
    o_ref[...] = x + _rms(z, gpost_ref[...])


def _merge(x2, ya, yb, gpre, wg, wo, gpost, tm):
    n, d = x2.shape
    row = pl.BlockSpec((tm, d), lambda i: (i, 0))
    return pl.pallas_call(
        _merge_kernel,
        grid=(n // tm,),
        in_specs=[row, row, row, _const_spec((1, d)), _const_spec(wg.shape),
                  _const_spec(wo.shape), _const_spec((1, d))],
        out_specs=row,
        out_shape=jax.ShapeDtypeStruct((n, d), F32),
        compiler_params=_cparams(("parallel",)),
        name="merge_out",
    )(x2, ya, yb, gpre, wg, wo, gpost)


def _mem_kv_kernel(m_ref, g_ref, w_ref, kv_ref):
    h = _rms(m_ref[...], g_ref[...]).astype(BF16)
    kv_ref[...] = _dot(h, w_ref[...]).astype(BF16)


def _mem_kv(mem2, g, w):
    n, d = mem2.shape
    return pl.pallas_call(
        _mem_kv_kernel,
        out_shape=jax.ShapeDtypeStruct((n, w.shape[1]), BF16),
        compiler_params=pltpu.CompilerParams(vmem_limit_bytes=VMEM_LIMIT),
        name="mem_kv",
    )(mem2, g, w)


def _xattn_kernel(x_ref, kt_ref, v_ref, gpre_ref, wq_ref, wo_ref, gpost_ref, o_ref):
    x = x_ref[...]
    d = x.shape[1]
    hd = d // XATT_HEADS
    h = _rms(x, gpre_ref[...]).astype(BF16)
    q = (_dot(h, wq_ref[...]) * (hd ** -0.5)).astype(BF16)
    heads = []
    for i in range(XATT_HEADS):
        sl = slice(i * hd, (i + 1) * hd)
        s = _dot(q[:, sl], kt_ref[sl, :])
        p = jnp.exp(s - jnp.max(s, axis=1, keepdims=True))
        o = _dot(p.astype(BF16), v_ref[:, sl]) / jnp.sum(p, axis=1, keepdims=True)
        heads.append(o.astype(BF16))
    z = _dot(jnp.concatenate(heads, axis=1), wo_ref[...])
    o_ref[...] = x + _rms(z, gpost_ref[...])


def _xattn(x, kt, v, gpre, wq, wo, gpost, tm):
    b, s, d = x.shape
    n_mem = v.shape[1]
    row = pl.BlockSpec((None, tm, d), lambda bi, i: (bi, i, 0))
    return pl.pallas_call(
        _xattn_kernel,
        grid=(b, s // tm),
        in_specs=[row,
                  pl.BlockSpec((None, d, n_mem), lambda bi, i: (bi, 0, 0)),
                  pl.BlockSpec((None, n_mem, d), lambda bi, i: (bi, 0, 0)),
                  _const_spec((1, d)), _const_spec(wq.shape), _const_spec(wo.shape),
                  _const_spec((1, d))],
        out_specs=row,
        out_shape=jax.ShapeDtypeStruct((b, s, d), F32),
        compiler_params=_cparams(("parallel", "parallel")),
        name="mem_xattn",
    )(x, kt, v, gpre, wq, wo, gpost)


def _ffn_kernel(x_ref, gpre_ref, wgu_ref, wd_ref, gpost_ref, o_ref, *, ff, fc):
    x = x_ref[...]
    h = _rms(x, gpre_ref[...]).astype(BF16)
    z = jnp.zeros(x.shape, F32)
    for c0 in range(0, ff, fc):
        gate = _dot(h, wgu_ref[:, c0:c0 + fc])
        up = _dot(h, wgu_ref[:, ff + c0:ff + c0 + fc])
        act = (gate * _sigmoid(gate) * up).astype(BF16)
        z = z + _dot(act, wd_ref[c0:c0 + fc, :])
    o_ref[...] = x + _rms(z, gpost_ref[...])


def _ffn(x2, gpre, wgu, wd, gpost, tm):
    n, d = x2.shape
    ff = wd.shape[0]
    fc = ff // 2 if (ff // 2) % LANES == 0 else ff
    row = pl.BlockSpec((tm, d), lambda i: (i, 0))
    return pl.pallas_call(
        functools.partial(_ffn_kernel, ff=ff, fc=fc),
        grid=(n // tm,),
        in_specs=[row, _const_spec((1, d)), _const_spec(wgu.shape), _const_spec(wd.shape),
                  _const_spec((1, d))],
        out_specs=row,
        out_shape=jax.ShapeDtypeStruct((n, d), F32),
        compiler_params=_cparams(("parallel",)),
        name="swiglu",
    )(x2, gpre, wgu, wd, gpost)


def _rope_tables(positions):
    pos = positions.astype(F32).reshape(-1)[:, None]
    lane = jnp.arange(LANES)

    def tables(dim):
        half = dim // 2
        in_head = lane % dim
        inv_freq = ROPE_THETA ** (-(in_head % half).astype(F32) * 2.0 / dim)
        ang = pos * inv_freq
        return jnp.cos(ang), jnp.where(in_head < half, -1.0, 1.0) * jnp.sin(ang)

    return tables(ATT_HEAD_DIM) + tables(IDX_DIM)


def _pick_tile(n, pref):
    t = min(pref, n)
    while n % t:
        t //= 2
    return t


def kernel(x, mem, positions, norm_mix_pre, w_in, conv_w, a_log, dt_bias, gdn_norm, w_out,
           norm_mix_post, norm_x_pre, norm_mem, w_xq, w_xkv, w_xo, norm_x_post,
           norm_ffn_pre, w_gu, w_down, norm_ffn_post):
    b, s, d = x.shape
    n = b * s
    n_mem = mem.shape[1]
    depth = w_in.shape[0]
    topk = min(TOPK_MAX, s // 4)
    tm = _pick_tile(s, 512)
    tq = _pick_tile(s, 256)
    kb = _pick_tile(s, 512)
    tb = _pick_tile(s, 256)

    ca, sa, ci, si = (t.reshape(b, s, 128) for t in _rope_tables(positions))

    row1 = lambda v: v.reshape(1, -1).astype(F32)

    x2 = x.reshape(n, d)
    for l in range(depth):
        w_att, w_gdn, w_gate = _prep_w_in(w_in, l)

        g_pre = row1(norm_mix_pre[l])
        x3 = x2.reshape(b, s, d)
        q, kt, v, qi, kit, wi = _proj_att(x3, g_pre, w_att, ca, sa, ci, si, tm)
        ya = _dsa(q, qi, wi, kt, v, kit, topk, tq, kb)

        act, ab, abt = _proj_gdn(x3, g_pre, w_gdn, conv_w[l].astype(F32), tm)
        pad_lanes = lambda vec: jnp.pad(vec.astype(F32), (0, 128 - GDN_HEADS)).reshape(1, 128)
        bcast_rows = lambda vec: jnp.broadcast_to(vec.astype(F32)[:, None], (GDN_HEADS, tb))
        yb = _gdn(act, ab, abt,
                  pad_lanes(a_log[l]), pad_lanes(dt_bias[l]), bcast_rows(a_log[l]), bcast_rows(dt_bias[l]),
                  row1(gdn_norm[l]), tb)

        x2 = _merge(x2, ya.reshape(n, 1024), yb.reshape(n, 1024), g_pre, w_gate,
                    w_out[l].astype(BF16), row1(norm_mix_post[l]), tm)

        kv = _mem_kv(mem.reshape(b * n_mem, d), row1(norm_mem[l]), w_xkv[l].astype(BF16))
        kv = kv.reshape(b, n_mem, 2 * d)
        kt_mem = jnp.swapaxes(kv[:, :, :d], 1, 2)
        x2 = _xattn(x2.reshape(b, s, d), kt_mem, kv[:, :, d:], row1(norm_x_pre[l]),
                    w_xq[l].astype(BF16), w_xo[l].astype(BF16), row1(norm_x_post[l]), tm).reshape(n, d)

        x2 = _ffn(x2, row1(norm_ffn_pre[l]), w_gu[l].astype(BF16), w_down[l].astype(BF16),
                  row1(norm_ffn_post[l]), tm)
    return x2.reshape(b, s, d)
```

```python
import functools

import jax
import jax.numpy as jnp
import numpy as np
from jax import lax
from jax.experimental import pallas as pl
from jax.experimental.pallas import tpu as pltpu

EPS = 1e-6
ROPE_THETA = 10000.0
ATT_HEADS = 8
ATT_KV_HEADS = 2
ATT_HEAD_DIM = 128
IDX_HEADS = 4
IDX_DIM = 64
TOPK_MAX = 256
GDN_HEADS = 8
GDN_DK = 128
GDN_DV = 128
CONV_K = 4
XATT_HEADS = 4

LANES = 128
VMEM_LIMIT = 56 * 1024 * 1024

F32 = jnp.float32
BF16 = jnp.bfloat16
I32 = jnp.int32

NEG_BIG = -1e30
LOG2E = 1.4426950408889634
INT_MAX = 2147483647
NEG_INF_KEY = -2139095041


def _cparams(sem):
    return pltpu.CompilerParams(dimension_semantics=sem, vmem_limit_bytes=VMEM_LIMIT)


def _const_spec(shape):
    nd = len(shape)
    return pl.BlockSpec(shape, lambda *_: (0,) * nd, pipeline_mode=pl.Buffered(1))


def _rms(x, g):
    return x * lax.rsqrt(jnp.mean(x * x, axis=-1, keepdims=True) + EPS) * g


def _dot(a, b):
    return jnp.dot(a, b, preferred_element_type=F32)


def _dot_nt(a, b):
    return lax.dot_general(a, b, (((1,), (1,)), ((), ())), preferred_element_type=F32)


def _split3(a):
    a1 = a.astype(BF16)
    r1 = a - a1.astype(F32)
    a2 = r1.astype(BF16)
    a3 = (r1 - a2.astype(F32)).astype(BF16)
    return a1, a2, a3


A_Q, A_KV, I_Q, I_K, I_W, B_QKV = 1024, 256, 256, 64, 4, 3072
O_IQ = A_Q + 2 * A_KV
O_IK = O_IQ + I_Q
O_IW = O_IK + I_K
O_B = O_IW + I_W
O_BA = O_B + B_QKV
O_GA = O_BA + 2 * GDN_HEADS
IN_COLS = O_GA + 2 * 1024


def _prep_w_in_kernel(w_ref, att_ref, gdn_ref, gate_ref):
    w = w_ref[...]
    zeros = lambda c: jnp.zeros((w.shape[0], c), F32)
    att = [w[:, :O_IQ]]
    for h in range(IDX_HEADS):
        att += [w[:, O_IQ + h * IDX_DIM:O_IQ + (h + 1) * IDX_DIM], zeros(128 - IDX_DIM)]
    att += [w[:, O_IK:O_IW], zeros(128 - I_K), w[:, O_IW:O_B], zeros(128 - I_W)]
    att_ref[...] = jnp.concatenate(att, axis=1).astype(BF16)
    gdn_ref[...] = jnp.concatenate([w[:, O_B:O_GA], zeros(128 - 2 * GDN_HEADS)], axis=1).astype(BF16)
    gate_ref[...] = w[:, O_GA:].astype(BF16)


def _prep_w_in(w_in, layer):
    _, d, cols = w_in.shape
    assert cols == IN_COLS
    tr = 128
    widths = (O_IQ + IDX_HEADS * 128 + 256, B_QKV + 128, 2048)
    return pl.pallas_call(
        _prep_w_in_kernel,
        grid=(d // tr,),
        in_specs=[pl.BlockSpec((None, tr, cols), lambda i: (layer, i, 0))],
        out_specs=[pl.BlockSpec((tr, c), lambda i: (i, 0)) for c in widths],
        out_shape=[jax.ShapeDtypeStruct((d, c), BF16) for c in widths],
        compiler_params=_cparams(("parallel",)),
        name="prep_w_in",
    )(w_in)


def _rope128(x, cos, sin_signed):
    return x * cos + pltpu.roll(x, 64, 1) * sin_signed


def _rope64(x, cos, sin_signed, first_half):
    partner = jnp.where(first_half, pltpu.roll(x, 96, 1), pltpu.roll(x, 32, 1))
    return x * cos + partner * sin_signed


def _proj_att_kernel(x_ref, g_ref, w_ref, ca_ref, sa_ref, ci_ref, si_ref,
                     q_ref, kt_ref, v_ref, qi_ref, kit_ref, wi_ref):
    h = _rms(x_ref[...], g_ref[...]).astype(BF16)
    p = _dot(h, w_ref[...])
    ca, sa = ca_ref[...], sa_ref[...]
    ci, si = ci_ref[...], si_ref[...]
    q_scale = ATT_HEAD_DIM ** -0.5 * LOG2E
    for hd in range(ATT_HEADS):
        sl = slice(hd * 128, (hd + 1) * 128)
        q_ref[:, sl] = (_rope128(p[:, sl], ca, sa) * q_scale).astype(BF16)
    for g in range(ATT_KV_HEADS):
        k_g = _rope128(p[:, A_Q + g * 128:A_Q + (g + 1) * 128], ca, sa)
        kt_ref[g * 128:(g + 1) * 128, :] = jnp.transpose(k_g).astype(BF16)
    v_ref[...] = p[:, A_Q + A_KV:O_IQ].astype(BF16)
    lane = lax.broadcasted_iota(I32, ci.shape, 1)
    first_half = (lane & 63) < 32
    idx_scale = IDX_DIM ** -0.5
    for hd in range(IDX_HEADS):
        off = O_IQ + hd * 128
        qi_ref[:, hd * 128:(hd + 1) * 128] = (
            _rope64(p[:, off:off + 128], ci, si, first_half) * idx_scale).astype(BF16)
    o_ki = O_IQ + IDX_HEADS * 128
    kit_ref[...] = jnp.transpose(_rope64(p[:, o_ki:o_ki + 128], ci, si, first_half)).astype(BF16)
    wi_ref[...] = p[:, o_ki + 128:o_ki + 256] * (IDX_HEADS ** -0.5)


def _proj_att(x, g, w, ca, sa, ci, si, tm):
    b, s, d = x.shape
    row = lambda c: pl.BlockSpec((None, tm, c), lambda bi, i: (bi, i, 0))
    col = lambda r: pl.BlockSpec((None, r, tm), lambda bi, i: (bi, 0, i))
    sds = jax.ShapeDtypeStruct
    return pl.pallas_call(
        _proj_att_kernel,
        grid=(b, s // tm),
        in_specs=[row(d), _const_spec((1, d)), _const_spec(w.shape),
                  row(128), row(128), row(128), row(128)],
        out_specs=[row(1024), col(256), row(256), row(512), col(128), row(128)],
        out_shape=[sds((b, s, 1024), BF16), sds((b, 256, s), BF16), sds((b, s, 256), BF16),
                   sds((b, s, 512), BF16), sds((b, 128, s), BF16), sds((b, s, 128), F32)],
        compiler_params=_cparams(("parallel", "parallel")),
        name="proj_att",
    )(x, g, w, ca, sa, ci, si)


def _proj_gdn_kernel(x_ref, g_ref, w_ref, cw_ref, act_ref, ab_ref, abt_ref, xbuf_ref):
    tm = x_ref.shape[0]

    @pl.when(pl.program_id(1) == 0)
    def _():
        xbuf_ref[0:8, :] = jnp.zeros((8, xbuf_ref.shape[1]), F32)

    h = _rms(x_ref[...], g_ref[...]).astype(BF16)
    p = _dot(h, w_ref[...])
    ab = p[:, B_QKV:B_QKV + 128]
    ab_ref[...] = ab
    abt_ref[...] = jnp.transpose(ab)[:16, :]
    xbuf_ref[8:8 + tm, :] = p[:, :B_QKV]
    for c in range(B_QKV // LANES):
        sl = slice(c * LANES, (c + 1) * LANES)
        y = xbuf_ref[5:5 + tm, sl] * cw_ref[0:1, sl]
        for j in range(1, CONV_K):
            y = y + xbuf_ref[5 + j:5 + j + tm, sl] * cw_ref[j:j + 1, sl]
        half = 0.5 * y
        act_ref[:, sl] = half + half * jnp.tanh(half)
    xbuf_ref[0:8, :] = xbuf_ref[tm:tm + 8, :]


def _proj_gdn(x, g, w, conv_w, tm):
    b, s, d = x.shape
    row = lambda c: pl.BlockSpec((None, tm, c), lambda bi, i: (bi, i, 0))
    sds = jax.ShapeDtypeStruct
    return pl.pallas_call(
        _proj_gdn_kernel,
        grid=(b, s // tm),
        in_specs=[row(d), _const_spec((1, d)), _const_spec(w.shape), _const_spec(conv_w.shape)],
        out_specs=[row(B_QKV), row(128), pl.BlockSpec((None, 16, tm), lambda bi, i: (bi, 0, i))],
        out_shape=[sds((b, s, B_QKV), F32), sds((b, s, 128), F32), sds((b, 16, s), F32)],
        scratch_shapes=[pltpu.VMEM((tm + 8, B_QKV), F32)],
        compiler_params=_cparams(("parallel", "arbitrary")),
        name="proj_gdn",
    )(x, g, w, conv_w)


def _key_of(v):
    b = pltpu.bitcast(v, I32)
    return jnp.where(b < 0, b ^ 0x7FFFFFFF, b)


def _val_of(k):
    return pltpu.bitcast(jnp.where(k < 0, k ^ 0x7FFFFFFF, k), F32)


def _dsa_kernel(q_ref, qi_ref, wi_ref, kt_ref, v_ref, kit_ref, o_ref,
                keys_ref, qis_ref, qg_ref, bias_ref, s_ref, p_ref, acc_ref, m_ref, l_ref, run_ref, tri_ref,
                *, tq, kb, topk, seq):
    t0 = pl.program_id(1) * tq
    nkb = (t0 + tq + kb - 1) // kb
    n_lane_tiles = kb // LANES
    row_ids = t0 + lax.broadcasted_iota(I32, (tq, kb), 0)
    col_iota = lax.broadcasted_iota(I32, (tq, kb), 1)
    kf = float(topk)

    for h in range(IDX_HEADS):
        qis_ref[h * tq:(h + 1) * tq, :] = qi_ref[:, h * 128:(h + 1) * 128]
    wi = wi_ref[...]
    w_heads = [wi[:, h:h + 1] for h in range(IDX_HEADS)]

    def score_block(j, carry, masked):
        mx, mn = carry
        c0 = pl.multiple_of(j * kb, kb)
        halves = []
        for a0 in range(0, kb, 256):
            lg = _dot(qis_ref[...], kit_ref[:, pl.ds(c0 + a0, 256)])
            part = jnp.zeros((tq, 256), F32)
            for h in range(IDX_HEADS):
                part = part + w_heads[h] * jnp.maximum(lg[h * tq:(h + 1) * tq], 0.0)
            halves.append(part)
        sc = jnp.concatenate(halves, axis=1)
        sc = jnp.where(sc == 0.0, 0.0, sc)
        for i in range(n_lane_tiles):
            mx = jnp.maximum(mx, sc[:, i * LANES:(i + 1) * LANES])
            mn = jnp.minimum(mn, sc[:, i * LANES:(i + 1) * LANES])
        if masked:
            sc = jnp.where(c0 + col_iota <= row_ids, sc, -jnp.inf)
        keys_ref[:, pl.ds(c0, kb)] = _key_of(sc)
        return mx, mn

    n_full = t0 // kb
    ext = (jnp.full((tq, LANES), -jnp.inf, F32), jnp.full((tq, LANES), jnp.inf, F32))

    done_blocks = 0
    for group in (4, 2, 1):
        def score_group(jj, carry, group=group, first=done_blocks):
            for u in range(group):
                carry = score_block(first + group * jj + u, carry, False)
            return carry

        n_groups = (n_full - done_blocks) // group
        ext = lax.fori_loop(0, n_groups, score_group, ext)
        done_blocks = done_blocks + group * n_groups
    mx, mn = lax.fori_loop(n_full, nkb, functools.partial(score_block, masked=True), ext)
    row_max = jnp.max(mx, axis=1, keepdims=True)
    row_min = jnp.min(mn, axis=1, keepdims=True)

    n_row_tiles = tq // LANES if tq % LANES == 0 else 1
    rt = tq // n_row_tiles

    def as_col(row):
        return jnp.transpose(row)[:, :1]

    def count_ge(cands):
        nc = len(cands)
        tiles = range(n_row_tiles)
        cb = [[c if isinstance(c, int) else
               jnp.transpose(jnp.broadcast_to(c[:1, t * rt:(t + 1) * rt], (LANES, rt)))
               for t in tiles] for c in cands]

        def tile_counts(t):
            def one_block(j, accs):
                c0 = pl.multiple_of(j * kb, kb)
                accs = list(accs)
                blk = keys_ref[t * rt:(t + 1) * rt, pl.ds(c0, kb)]
                for i in range(n_lane_tiles):
                    for k in range(nc):
                        accs[k] = accs[k] + jnp.where(blk[:, i * LANES:(i + 1) * LANES] >= cb[k][t], 1.0, 0.0)
                return tuple(accs)

            def two_blocks(jj, accs):
                return one_block(2 * jj + 1, one_block(2 * jj, accs))

            z = jnp.zeros((rt, LANES), F32)
            accs = lax.fori_loop(0, nkb // 2, two_blocks, (z,) * nc)
            return lax.fori_loop(2 * (nkb // 2), nkb, one_block, accs)

        accs = [tile_counts(t) for t in tiles]
        row_sums = lambda a: jnp.broadcast_to(jnp.sum(jnp.transpose(a), axis=0, keepdims=True), (8, rt))
        return [jnp.concatenate([row_sums(accs[t][k]) for t in tiles], axis=1) for k in range(nc)]

    as_row = lambda col: jnp.transpose(jnp.broadcast_to(col, (tq, 8)))
    n_visible = (t0 + lax.broadcasted_iota(I32, (8, tq), 1) + 1).astype(F32)
    few = n_visible < kf
    f_pos, f_zero = count_ge([1, 0])
    above = f_pos >= kf
    below = f_zero < kf
    at_zero = jnp.logical_and(jnp.logical_not(above), jnp.logical_not(below))
    key_min = _key_of(as_row(row_min))
    key_max = _key_of(as_row(row_max))
    lo0 = jnp.where(few, NEG_INF_KEY + 1, jnp.where(above, 1, jnp.where(below, key_min, 0)))
    flo0 = jnp.where(few, n_visible, jnp.where(above, f_pos, jnp.where(below, n_visible, f_zero)))
    hi0 = jnp.where(above, jnp.minimum(key_max, 2147483646) + 1, jnp.where(below, 0, 1))
    fhi0 = jnp.where(above, 0.0, jnp.where(below, f_zero, f_pos))
    log_target = float(np.log(topk + 0.5))
    glog = lambda c: jnp.log(jnp.maximum(c, 0.5)) - log_target
    done0 = jnp.where(jnp.logical_or(jnp.logical_or(few, at_zero), flo0 == kf), 1.0, 0.0)
    n_interp = 24

    def peel_min(base, flip):
        tiles = range(n_row_tiles)
        bb = [jnp.broadcast_to(base[t * rt:(t + 1) * rt], (rt, LANES)) for t in tiles]
        fb = [jnp.broadcast_to(flip[t * rt:(t + 1) * rt], (rt, LANES)) for t in tiles]

        def tile_min(t):
            def body(j, acc):
                c0 = pl.multiple_of(j * kb, kb)
                blk = keys_ref[t * rt:(t + 1) * rt, pl.ds(c0, kb)]
                for i in range(n_lane_tiles):
                    x = blk[:, i * LANES:(i + 1) * LANES] ^ fb[t]
                    acc = jnp.minimum(acc, jnp.where(x >= bb[t], x, INT_MAX))
                return acc

            acc = lax.fori_loop(0, nkb, body, jnp.full((rt, LANES), INT_MAX, I32))
            return jnp.min(acc, axis=1, keepdims=True)

        col = jnp.concatenate([tile_min(t) for t in tiles], axis=0)
        return jnp.transpose(jnp.broadcast_to(col, (tq, 8)))

    def probe(st, forced=None, use_forced=None):
        it, _, _, lo, hi, flo, fhi, glo, ghi, side, done = st
        v_lo, v_hi = _val_of(lo), _val_of(hi)
        frac = glo / (glo - ghi)
        frac = jnp.where(jnp.logical_and(it == 0, lo == 1), 0.5 * (frac + jnp.sqrt(jnp.maximum(frac, 0.0))), frac)
        v_model = v_lo + (v_hi - v_lo) * frac
        inside = flo - fhi
        v_even = v_lo + (v_hi - v_lo) * ((flo - kf + 0.5) / (inside + 1.0))
        c_interp = _key_of(jnp.where(inside <= 64.0, v_even, v_model))
        c_mid = jnp.right_shift(lo, 1) + jnp.right_shift(hi, 1) + (lo & hi & 1)
        cand = jnp.where(it >= n_interp, c_mid, c_interp)
        if forced is not None:
            cand = jnp.where(use_forced, forced, cand)
        cand = jnp.minimum(jnp.maximum(cand, lo + 1), hi - 1)
        c = count_ge([cand])[0]
        active = done < 0.5
        to_lo = jnp.logical_and(active, c >= kf)
        to_hi = jnp.logical_and(active, c < kf)
        gc = glog(c)
        ghi_n = jnp.where(to_hi, gc, jnp.where(jnp.logical_and(to_lo, side > 0.0), ghi * 0.5, ghi))
        glo_n = jnp.where(to_lo, gc, jnp.where(jnp.logical_and(to_hi, side < 0.0), glo * 0.5, glo))
        side = jnp.where(to_lo, 1.0, jnp.where(to_hi, -1.0, side))
        lo = jnp.where(to_lo, cand, lo)
        flo = jnp.where(to_lo, c, flo)
        hi = jnp.where(to_hi, cand, hi)
        fhi = jnp.where(to_hi, c, fhi)
        fin = jnp.logical_or(flo == kf, hi - 1 == lo)
        done = jnp.where(fin, 1.0, done)
        near = jnp.logical_or(flo - kf <= 1.0, kf - fhi <= 1.0)
        n_far = jnp.sum(jnp.where(near, 0.0, 1.0 - done))
        return it + 1, n_far, jnp.sum(1.0 - done), lo, hi, flo, fhi, glo_n, ghi_n, side, done

    n_pass_cap = n_interp + 34
    near0 = jnp.logical_or(flo0 - kf <= 1.0, kf - fhi0 <= 1.0)
    st = (jnp.int32(0), jnp.sum(jnp.where(near0, 0.0, 1.0 - done0)), jnp.sum(1.0 - done0),
          lo0, hi0, flo0, fhi0, glog(flo0), glog(fhi0), jnp.zeros((8, tq), F32), done0)
    st = lax.while_loop(lambda st: jnp.logical_and(st[0] < n_pass_cap, st[1] > 0.0), probe, st)
    lo, hi, flo, fhi, done = st[3], st[4], st[5], st[6], st[10]
    active = done < 0.5
    drop_one = jnp.logical_and(active, flo - kf <= 1.0)
    add_one = jnp.logical_and(jnp.logical_and(active, jnp.logical_not(drop_one)), kf - fhi <= 1.0)
    flip = jnp.where(add_one, -1, 0)
    edge = peel_min(as_col(jnp.where(add_one, -hi, lo)), as_col(flip))
    forced = jnp.where(add_one, edge ^ flip, edge + 1)
    st = probe(st, forced, jnp.logical_or(drop_one, add_one))
    st = lax.while_loop(lambda st: jnp.logical_and(st[0] < n_pass_cap + 2, st[2] > 0.0), probe, st)
    thr = as_col(st[3])
    tie_rows = jnp.logical_and(st[5] > kf, jnp.logical_not(few))
    need = as_col(jnp.where(tie_rows, kf - st[6], float(2 * seq)))

    run_ref[...] = jnp.zeros(run_ref.shape, F32)
    rr = lax.broadcasted_iota(I32, (LANES, 2 * LANES), 0)
    cc = lax.broadcasted_iota(I32, (LANES, 2 * LANES), 1)
    tri_ref[...] = jnp.where(jnp.logical_or(rr <= cc, cc >= LANES), 1.0, 0.0).astype(BF16)

    rep = ATT_HEADS // ATT_KV_HEADS
    rows = rep * tq
    rc = 32
    for g in range(ATT_KV_HEADS):
        for r in range(rep):
            hd = g * rep + r
            qg_ref[g, r * tq:(r + 1) * tq, :] = q_ref[:, hd * 128:(hd + 1) * 128]
    acc_ref[...] = jnp.zeros(acc_ref.shape, F32)
    l_ref[...] = jnp.zeros(l_ref.shape, F32)
    m_ref[...] = jnp.full(m_ref.shape, NEG_BIG, F32)

    def select_block(j, buf, count_ties):
        c0 = pl.multiple_of(j * kb, kb)
        run = run_ref[...]
        left = jnp.broadcast_to(need - run, (tq, LANES))
        for i in range(n_lane_tiles):
            keys = keys_ref[:, pl.ds(c0 + i * LANES, LANES)]
            tied = keys == thr
            cnt = _dot(jnp.where(tied, 1.0, 0.0).astype(BF16), tri_ref[...])
            ok = jnp.logical_and(tied, cnt[:, :LANES] <= left)
            bias_ref[buf, :, i * LANES:(i + 1) * LANES] = jnp.where(keys > thr, 0.0,
                                                                      jnp.where(ok, 0.0, NEG_BIG))
            left = left - cnt[:, LANES:]
        run_ref[...] = jnp.where(count_ties, need - left[:, :1], run)

    select_block(0, 0, True)

    def attend_block(j, carry):
        c0 = pl.multiple_of(j * kb, kb)
        buf = j & 1
        for g in range(ATT_KV_HEADS):
            for a0 in range(0, kb, 256):
                s_ref[g, :, a0:a0 + 256] = _dot(qg_ref[g], kt_ref[g * 128:(g + 1) * 128, pl.ds(c0 + a0, 256)])
        for g in range(ATT_KV_HEADS):
            for r0 in range(0, rows, rc):
                rs = slice(r0, r0 + rc)
                b0 = r0 % tq
                s = s_ref[g, rs, :] + bias_ref[buf, pl.ds(b0, rc), :]
                m_prev = m_ref[g, rs, :]
                m_next = jnp.maximum(m_prev, jnp.max(s, axis=1, keepdims=True))
                p = jnp.exp2(s - jnp.concatenate([m_next] * n_lane_tiles, axis=1))
                alpha = jnp.exp2(m_prev - m_next)
                p_lanes = p[:, :LANES]
                for i in range(1, n_lane_tiles):
                    p_lanes = p_lanes + p[:, i * LANES:(i + 1) * LANES]
                l_ref[g, rs, :] = alpha * l_ref[g, rs, :] + p_lanes
                m_ref[g, rs, :] = m_next
                acc_ref[g, rs, :] = acc_ref[g, rs, :] * alpha
                p_ref[g, rs, :] = p.astype(BF16)
            acc_ref[g] += _dot(p_ref[g], v_ref[pl.ds(c0, kb), g * 128:(g + 1) * 128])
        select_block(jnp.minimum(j + 1, nkb - 1), 1 - buf, j + 1 < nkb)
        return carry

    lax.fori_loop(0, nkb, attend_block, 0)

    for g in range(ATT_KV_HEADS):
        out = acc_ref[g] / jnp.sum(l_ref[g], axis=1, keepdims=True)
        for r in range(rep):
            hd = g * rep + r
            o_ref[:, hd * 128:(hd + 1) * 128] = out[r * tq:(r + 1) * tq]


def _dsa(q, qi, wi, kt, v, kit, topk, tq, kb):
    b, s, _ = q.shape
    rep = ATT_HEADS // ATT_KV_HEADS
    qspec = lambda c: pl.BlockSpec((None, tq, c), lambda bi, i: (bi, i, 0))
    per_batch = lambda r, c: pl.BlockSpec((None, r, c), lambda bi, i: (bi, 0, 0),
                                          pipeline_mode=pl.Buffered(1))
    return pl.pallas_call(
        functools.partial(_dsa_kernel, tq=tq, kb=kb, topk=topk, seq=s),
        grid=(b, s // tq),
        in_specs=[qspec(1024), qspec(512), qspec(128),
                  per_batch(256, s), per_batch(s, 256), per_batch(128, s)],
        out_specs=qspec(1024),
        out_shape=jax.ShapeDtypeStruct((b, s, 1024), F32),
        scratch_shapes=[pltpu.VMEM((tq, s), I32),
                        pltpu.VMEM((IDX_HEADS * tq, 128), BF16),
                        pltpu.VMEM((ATT_KV_HEADS, rep * tq, 128), BF16),
                        pltpu.VMEM((2, tq, kb), F32),
                        pltpu.VMEM((ATT_KV_HEADS, rep * tq, kb), F32),
                        pltpu.VMEM((ATT_KV_HEADS, rep * tq, kb), BF16),
                        pltpu.VMEM((ATT_KV_HEADS, rep * tq, 128), F32),
                        pltpu.VMEM((ATT_KV_HEADS, rep * tq, 128), F32),
                        pltpu.VMEM((ATT_KV_HEADS, rep * tq, 128), F32),
                        pltpu.VMEM((tq, 1), F32),
                        pltpu.VMEM((LANES, 2 * LANES), BF16)],
        compiler_params=_cparams(("parallel", "arbitrary")),
        name="dsa_attention",
    )(q, qi, wi, kt, v, kit)


def _softplus(x):
    return jnp.maximum(x, 0.0) + jnp.log1p(jnp.exp(-jnp.abs(x)))


def _gdn_kernel(act_ref, ab_ref, abt_ref, ac_ref, dc_ref, ar_ref, dr_ref, gn_ref,
                y_ref, state_ref, *, tb):
    ck = tb
    n_chunks = tb // ck

    @pl.when(pl.program_id(1) == 0)
    def _():
        state_ref[...] = jnp.zeros(state_ref.shape, F32)

    ab = ab_ref[...]
    g_col = -jnp.exp(ac_ref[...]) * _softplus(ab + dc_ref[...])
    beta_col = 1.0 / (1.0 + jnp.exp(-ab))
    g_row = -jnp.exp(ar_ref[...]) * _softplus(abt_ref[0:8, :] + dr_ref[...])

    r = lax.broadcasted_iota(I32, (tb, tb), 0)
    c = lax.broadcasted_iota(I32, (tb, tb), 1)
    chunk_shift = ck.bit_length() - 1
    same = jnp.right_shift(r, chunk_shift) == jnp.right_shift(c, chunk_shift)
    lower = jnp.logical_and(same, r >= c)
    strict = jnp.logical_and(same, r > c)
    lower_m = jnp.where(lower, 1.0, 0.0).astype(BF16)
    upper_m = jnp.where(jnp.logical_and(same, r <= c), 1.0, 0.0).astype(BF16)
    same_m = jnp.where(same, 1.0, 0.0).astype(BF16)
    eye = jnp.where(r == c, 1.0, 0.0)
    n_levels = ck.bit_length() - 1
    lvl_subs = []
    for lvl in range(n_levels):
        lvl_subs.append(jnp.logical_and(jnp.right_shift(r, lvl + 1) == jnp.right_shift(c, lvl + 1),
                                        jnp.logical_and((jnp.right_shift(r, lvl) & 1) == 1,
                                                        (jnp.right_shift(c, lvl) & 1) == 0)))
    lvl_masks = [jnp.where(sub, 1.0, 0.0).astype(BF16) for sub in lvl_subs]

    gparts = _split3(g_col)
    gc_col = _dot(lower_m, gparts[0]) + (_dot(lower_m, gparts[1]) + _dot(lower_m, gparts[2]))
    gl_col = _dot(same_m, gparts[0]) + (_dot(same_m, gparts[1]) + _dot(same_m, gparts[2]))
    rparts = _split3(g_row)
    gc_row = _dot(rparts[0], upper_m) + (_dot(rparts[1], upper_m) + _dot(rparts[2], upper_m))

    gn = gn_ref[...]
    hg = 4
    for h0 in range(0, GDN_HEADS, hg):
        heads = range(h0, h0 + hg)
        q_l, k_l, kbeta_l, vbeta_l, decay_l, gc_l, gl_l, m16_l, d_l = ([] for _ in range(9))
        for h in heads:
            q = act_ref[:, h * 128:(h + 1) * 128]
            k = act_ref[:, 1024 + h * 128:1024 + (h + 1) * 128]
            v = act_ref[:, 2048 + h * 128:2048 + (h + 1) * 128]
            q = q * lax.rsqrt(jnp.sum(q * q, axis=-1, keepdims=True) + EPS) * (GDN_DK ** -0.5)
            k = k * lax.rsqrt(jnp.sum(k * k, axis=-1, keepdims=True) + EPS)
            gc = gc_col[:, h:h + 1]
            beta = beta_col[:, 8 + h:9 + h]
            diff = gc - gc_row[h:h + 1, :]
            decay = jnp.where(lower, jnp.exp(jnp.where(lower, diff, 0.0)), 0.0)
            kbeta = k * beta
            nm = jnp.where(strict, -(_dot_nt(kbeta.astype(BF16), k.astype(BF16)) * decay), 0.0)
            q_l.append(q); k_l.append(k); kbeta_l.append(kbeta); vbeta_l.append(v * beta)
            decay_l.append(decay); gc_l.append(gc); gl_l.append(gl_col[:, h:h + 1])
            m16_l.append(nm.astype(BF16)); d_l.append(eye + jnp.where(lvl_subs[0], nm, 0.0))
        for lvl in range(1, n_levels):
            for i in range(hg):
                dh = d_l[i].astype(BF16)
                lh = m16_l[i] * lvl_masks[lvl]
                th = _dot(dh, lh).astype(BF16)
                d_l[i] = d_l[i] + _dot(th, dh)
        u_l, w_l, attn_l, qd_l, kd_l = [], [], [], [], []
        for i in range(hg):
            egc = jnp.exp(gc_l[i])
            rhs = jnp.concatenate([vbeta_l[i], kbeta_l[i] * egc], axis=1)
            sol = _dot(d_l[i].astype(BF16), rhs.astype(BF16))
            u_l.append(sol[:, :128])
            w_l.append(sol[:, 128:].astype(BF16))
            k16 = k_l[i].astype(BF16)
            attn_l.append(jnp.where(lower, _dot_nt(q_l[i].astype(BF16), k16) * decay_l[i], 0.0).astype(BF16))
            qd_l.append((q_l[i] * egc).astype(BF16))
            kd_l.append(k_l[i] * jnp.exp(gl_l[i] - gc_l[i]))
        st_l = [state_ref[h] for h in heads]
        outs = [[] for _ in range(hg)]
        for ci in range(n_chunks):
            rc = slice(ci * ck, (ci + 1) * ck)
            for i in range(hg):
                st16 = st_l[i].astype(BF16)
                v_new = u_l[i][rc] - _dot(w_l[i][rc], st16)
                v_new16 = v_new.astype(BF16)
                outs[i].append(_dot(qd_l[i][rc], st16) + _dot(attn_l[i][rc, rc], v_new16))
                kd_t = jnp.transpose(kd_l[i][rc]).astype(BF16)
                st_l[i] = (st_l[i] * jnp.exp(gl_l[i][ci * ck:ci * ck + 1])
                           + _dot(kd_t, v_new16))
        for i, h in enumerate(heads):
            state_ref[h] = st_l[i]
            y_ref[:, h * 128:(h + 1) * 128] = _rms(jnp.concatenate(outs[i], axis=0), gn)


def _gdn(act, ab, abt, ac, dc, ar, dr, gn, tb):
    b, s, _ = act.shape
    blk = lambda c: pl.BlockSpec((None, tb, c), lambda bi, i: (bi, i, 0))
    return pl.pallas_call(
        functools.partial(_gdn_kernel, tb=tb),
        grid=(b, s // tb),
        in_specs=[blk(3072), blk(128), pl.BlockSpec((None, 16, tb), lambda bi, i: (bi, 0, i)),
                  _const_spec(ac.shape), _const_spec(dc.shape),
                  _const_spec(ar.shape), _const_spec(dr.shape), _const_spec(gn.shape)],
        out_specs=blk(1024),
        out_shape=jax.ShapeDtypeStruct((b, s, 1024), F32),
        scratch_shapes=[pltpu.VMEM((GDN_HEADS, GDN_DK, GDN_DV), F32)],
        compiler_params=_cparams(("parallel", "arbitrary")),
        name="gated_delta",
    )(act, ab, abt, ac, dc, ar, dr, gn)


def _sigmoid(x):
    return 1.0 / (1.0 + jnp.exp(-x))


def _merge_kernel(x_ref, ya_ref, yb_ref, gpre_ref, wg_ref, wo_ref, gpost_ref, o_ref):
    x = x_ref[...]
    h = _rms(x, gpre_ref[...]).astype(BF16)
    gates = _dot(h, wg_ref[...])
    y = _sigmoid(gates[:, :1024]) * ya_ref[...] + _sigmoid(gates[:, 1024:]) * yb_ref[...]
    z = _dot(y.astype(BF16), wo_ref[...])
    o_ref[...] = x + _rms(z, gpost_ref[...])


def _merge(x2, ya, yb, gpre, wg, wo, gpost, tm):
    n, d = x2.shape
    row = pl.BlockSpec((tm, d), lambda i: (i, 0))
    return pl.pallas_call(
        _merge_kernel,
        grid=(n // tm,),
        in_specs=[row, row, row, _const_spec((1, d)), _const_spec(wg.shape),
                  _const_spec(wo.shape), _const_spec((1, d))],
        out_specs=row,
        out_shape=jax.ShapeDtypeStruct((n, d), F32),
        compiler_params=_cparams(("parallel",)),
        name="merge_out",
    )(x2, ya, yb, gpre, wg, wo, gpost)


def _mem_kv_kernel(m_ref, g_ref, w_ref, kv_ref):
    h = _rms(m_ref[...], g_ref[...]).astype(BF16)
    kv_ref[...] = _dot(h, w_ref[...]).astype(BF16)


def _mem_kv(mem2, g, w):
    n, d = mem2.shape
    return pl.pallas_call(
        _mem_kv_kernel,
        out_shape=jax.ShapeDtypeStruct((n, w.shape[1]), BF16),
        compiler_params=pltpu.CompilerParams(vmem_limit_bytes=VMEM_LIMIT),
        name="mem_kv",
    )(mem2, g, w)


def _xattn_kernel(x_ref, kt_ref, v_ref, gpre_ref, wq_ref, wo_ref, gpost_ref, o_ref):
    x = x_ref[...]
    d = x.shape[1]
    hd = d // XATT_HEADS
    h = _rms(x, gpre_ref[...]).astype(BF16)
    q = (_dot(h, wq_ref[...]) * (hd ** -0.5)).astype(BF16)
    heads = []
    for i in range(XATT_HEADS):
        sl = slice(i * hd, (i + 1) * hd)
        s = _dot(q[:, sl], kt_ref[sl, :])
        p = jnp.exp(s - jnp.max(s, axis=1, keepdims=True))
        o = _dot(p.astype(BF16), v_ref[:, sl]) / jnp.sum(p, axis=1, keepdims=True)
        heads.append(o.astype(BF16))
    z = _dot(jnp.concatenate(heads, axis=1), wo_ref[...])
    o_ref[...] = x + _rms(z, gpost_ref[...])


def _xattn(x, kt, v, gpre, wq, wo, gpost, tm):
    b, s, d = x.shape
    n_mem = v.shape[1]
    row = pl.BlockSpec((None, tm, d), lambda bi, i: (bi, i, 0))
    return pl.pallas_call(
        _xattn_kernel,
        grid=(b, s // tm),
        in_specs=[row,
                  pl.BlockSpec((None, d, n_mem), lambda bi, i: (bi, 0, 0)),
                  pl.BlockSpec((None, n_mem, d), lambda bi, i: (bi, 0, 0)),
                  _const_spec((1, d)), _const_spec(wq.shape), _const_spec(wo.shape),
                  _const_spec((1, d))],
        out_specs=row,
        out_shape=jax.ShapeDtypeStruct((b, s, d), F32),
        compiler_params=_cparams(("parallel", "parallel")),
        name="mem_xattn",
    )(x, kt, v, gpre, wq, wo, gpost)


def _ffn_kernel(x_ref, gpre_ref, wgu_ref, wd_ref, gpost_ref, o_ref, *, ff, fc):
    x = x_ref[...]
    h = _rms(x, gpre_ref[...]).astype(BF16)
    z = jnp.zeros(x.shape, F32)
    for c0 in range(0, ff, fc):
        gate = _dot(h, wgu_ref[:, c0:c0 + fc])
        up = _dot(h, wgu_ref[:, ff + c0:ff + c0 + fc])
        act = (gate * _sigmoid(gate) * up).astype(BF16)
        z = z + _dot(act, wd_ref[c0:c0 + fc, :])
    o_ref[...] = x + _rms(z, gpost_ref[...])


def _ffn(x2, gpre, wgu, wd, gpost, tm):
    n, d = x2.shape
    ff = wd.shape[0]
    fc = ff // 2 if (ff // 2) % LANES == 0 else ff
    row = pl.BlockSpec((tm, d), lambda i: (i, 0))
    return pl.pallas_call(
        functools.partial(_ffn_kernel, ff=ff, fc=fc),
        grid=(n // tm,),
        in_specs=[row, _const_spec((1, d)), _const_spec(wgu.shape), _const_spec(wd.shape),
                  _const_spec((1, d))],
        out_specs=row,
        out_shape=jax.ShapeDtypeStruct((n, d), F32),
        compiler_params=_cparams(("parallel",)),
        name="swiglu",
    )(x2, gpre, wgu, wd, gpost)


def _rope_tables(positions):
    pos = positions.astype(F32).reshape(-1)[:, None]
    lane = jnp.arange(LANES)

    def tables(dim):
        half = dim // 2
        in_head = lane % dim
        inv_freq = ROPE_THETA ** (-(in_head % half).astype(F32) * 2.0 / dim)
        ang = pos * inv_freq
        return jnp.cos(ang), jnp.where(in_head < half, -1.0, 1.0) * jnp.sin(ang)

    return tables(ATT_HEAD_DIM) + tables(IDX_DIM)


def _pick_tile(n, pref):
    t = min(pref, n)
    while n % t:
        t //= 2
    return t


def kernel(x, mem, positions, norm_mix_pre, w_in, conv_w, a_log, dt_bias, gdn_norm, w_out,
           norm_mix_post, norm_x_pre, norm_mem, w_xq, w_xkv, w_xo, norm_x_post,
           norm_ffn_pre, w_gu, w_down, norm_ffn_post):
    b, s, d = x.shape
    n = b * s
    n_mem = mem.shape[1]
    depth = w_in.shape[0]
    topk = min(TOPK_MAX, s // 4)
    tm = _pick_tile(s, 512)
    tq = _pick_tile(s, 256)
    kb = _pick_tile(s, 512)
    tb = _pick_tile(s, 256)

    ca, sa, ci, si = (t.reshape(b, s, 128) for t in _rope_tables(positions))

    row1 = lambda v: v.reshape(1, -1).astype(F32)

    x2 = x.reshape(n, d)
    for l in range(depth):
        w_att, w_gdn, w_gate = _prep_w_in(w_in, l)

        g_pre = row1(norm_mix_pre[l])
        x3 = x2.reshape(b, s, d)
        q, kt, v, qi, kit, wi = _proj_att(x3, g_pre, w_att, ca, sa, ci, si, tm)
        ya = _dsa(q, qi, wi, kt, v, kit, topk, tq, kb)

        act, ab, abt = _proj_gdn(x3, g_pre, w_gdn, conv_w[l].astype(F32), tm)
        pad_lanes = lambda vec: jnp.pad(vec.astype(F32), (0, 128 - GDN_HEADS)).reshape(1, 128)
        bcast_rows = lambda vec: jnp.broadcast_to(vec.astype(F32)[:, None], (GDN_HEADS, tb))
        yb = _gdn(act, ab, abt,
                  pad_lanes(a_log[l]), pad_lanes(dt_bias[l]), bcast_rows(a_log[l]), bcast_rows(dt_bias[l]),
                  row1(gdn_norm[l]), tb)

        x2 = _merge(x2, ya.reshape(n, 1024), yb.reshape(n, 1024), g_pre, w_gate,
                    w_out[l].astype(BF16), row1(norm_mix_post[l]), tm)

        kv = _mem_kv(mem.reshape(b * n_mem, d), row1(norm_mem[l]), w_xkv[l].astype(BF16))
        kv = kv.reshape(b, n_mem, 2 * d)
        kt_mem = jnp.swapaxes(kv[:, :, :d], 1, 2)
        x2 = _xattn(x2.reshape(b, s, d), kt_mem, kv[:, :, d:], row1(norm_x_pre[l]),
                    w_xq[l].astype(BF16), w_xo[l].astype(BF16), row1(norm_x_post[l]), tm).reshape(n, d)

        x2 = _ffn(x2, row1(norm_ffn_pre[l]), w_gu[l].astype(BF16), w_down[l].astype(BF16),
                  row1(norm_ffn_post[l]), tm)
    return x2.reshape(b, s, d)
```
